```python
import jax, jax.numpy as jnp
from jax import lax
import numpy as np

D_MODEL = 2048
BATCH = 4
SEQ = 2048
DEPTH = 4
DEC_BATCH = 32
DEC_SEQ = 1
PAST_LEN = 16384
PAGE_SIZE = 128

BRANCH_WIDTH = D_MODEL // 2
N_BRANCH = 3
GLA_HEADS = 4
GLA_DV = BRANCH_WIDTH // GLA_HEADS
GLA_DK = GLA_DV // 2
GLA_RANK = 16
GLA_TAU = 16.0
GLA_CHUNK = 64
CONV_WIDTH = BRANCH_WIDTH
CONV_K = 3
SWA_HD = 64
SWA_HEADS = BRANCH_WIDTH // SWA_HD
SWA_KV_HEADS = SWA_HEADS // 4
SWA_GROUP = SWA_HEADS // SWA_KV_HEADS
WINDOW = 128
D_FF = 4 * D_MODEL
EPS = 1e-6

SPLIT_SIZES = (
    GLA_HEADS * GLA_DK,
    GLA_HEADS * GLA_DK,
    GLA_HEADS * GLA_DV,
    GLA_HEADS * GLA_DV,
    GLA_RANK,
    CONV_WIDTH,
    CONV_WIDTH,
    CONV_WIDTH,
    SWA_HEADS * SWA_HD,
    SWA_KV_HEADS * SWA_HD,
    SWA_KV_HEADS * SWA_HD,
    N_BRANCH * D_MODEL,
)
IN_COLS = sum(SPLIT_SIZES)

kernel_name = 'hybrid_gla_shortconv_swa_gated_decode_step'


def _split_points():
    pts, acc = [], 0
    for s in SPLIT_SIZES[:-1]:
        acc += s
        pts.append(acc)
    return pts


def _rmsnorm(x, g):
    xf = x.astype(jnp.float32)
    y = xf * lax.rsqrt(jnp.mean(xf * xf, axis=-1, keepdims=True) + EPS)
    return (y * g.astype(jnp.float32)).astype(x.dtype)


def _gla(q, k, v, log_a, s0):
    B, L = q.shape[:2]
    C = min(GLA_CHUNK, L)
    n = -(-L // C)
    pad = n * C - L
    if pad:
        padf = lambda t: jnp.pad(t, ((0, 0), (0, pad), (0, 0), (0, 0)))
        q, k, v, log_a = padf(q), padf(k), padf(v), padf(log_a)
    rs = lambda t: t.reshape(B, n, C, GLA_HEADS, t.shape[-1]).transpose(0, 3, 1, 2, 4)
    q, k, v, log_a = rs(q), rs(k), rs(v), rs(log_a)
    b = jnp.cumsum(log_a, axis=3)
    b_last = b[:, :, :, -1:, :]
    qt = q * jnp.exp(b)
    kt = k * jnp.exp(-b)
    kd = k * jnp.exp(b_last - b)
    causal = jnp.tril(jnp.ones((C, C), dtype=bool))
    A = jnp.where(causal, jnp.einsum('bhnck,bhnsk->bhncs', qt, kt), 0.0)
    o_intra = jnp.einsum('bhncs,bhnsv->bhncv', A, v)
    U = jnp.einsum('bhnck,bhncv->bhnkv', kd, v)
    g = jnp.exp(b_last[:, :, :, 0, :])

    def step(S, inp):
        g_n, U_n = inp
        return g_n[..., None] * S + U_n, S

    S_fin, S_in = lax.scan(step, s0, (jnp.moveaxis(g, 2, 0), jnp.moveaxis(U, 2, 0)))
    S_in = jnp.moveaxis(S_in, 0, 2)
    o = jnp.einsum('bhnck,bhnkv->bhncv', qt, S_in) + o_intra
    o = o.transpose(0, 2, 3, 1, 4).reshape(B, n * C, GLA_HEADS, GLA_DV)[:, :L]
    return o, S_fin


def _sink_alibi_attn(q, k, v, q_pos, k_pos, sinks):
    slopes = jnp.exp2(-8.0 * jnp.arange(1, SWA_HEADS + 1, dtype=jnp.float32) / SWA_HEADS)
    slopes = slopes.reshape(SWA_KV_HEADS, SWA_GROUP)[:, :, None, None]
    s = jnp.einsum('...tkgd,...skd->...kgts', q, k).astype(jnp.float32) * (SWA_HD ** -0.5)
    dist = q_pos[..., :, None] - k_pos[..., None, :]
    valid = (dist >= 0) & (dist < WINDOW) & (k_pos[..., None, :] >= 0)
    dist = dist[..., None, None, :, :].astype(jnp.float32)
    valid = valid[..., None, None, :, :]
    s = jnp.where(valid, s - slopes * dist, -jnp.inf)
    sink = sinks.astype(jnp.float32).reshape(SWA_KV_HEADS, SWA_GROUP)[:, :, None, None]
    m = jnp.maximum(jnp.max(s, axis=-1, keepdims=True), sink)
    p = jnp.exp(s - m)
    p = p / (jnp.sum(p, axis=-1, keepdims=True) + jnp.exp(sink - m))
    return jnp.einsum('...kgts,...skd->...tkgd', p.astype(v.dtype), v)


def _swa_prompt(q, k, v, sinks):
    B, L = q.shape[:2]
    nb = L // WINDOW
    qb = q.reshape(B, nb, WINDOW, SWA_KV_HEADS, SWA_GROUP, SWA_HD)

    def band(t):
        tp = jnp.pad(t, ((0, 0), (WINDOW, 0), (0, 0), (0, 0)))
        tp = tp.reshape(B, nb + 1, WINDOW, SWA_KV_HEADS, SWA_HD)
        return jnp.concatenate([tp[:, :-1], tp[:, 1:]], axis=2)

    pos = jnp.arange(L, dtype=jnp.int32).reshape(nb, WINDOW)
    k_pos = jnp.concatenate([pos - WINDOW, pos], axis=1)
    o = _sink_alibi_attn(qb, band(k), band(v), pos, k_pos, sinks)
    return o.reshape(B, L, SWA_HEADS * SWA_HD), k[:, -WINDOW:], v[:, -WINDOW:]


def _swa_sample(q, k, v, kc, vc, sinks):
    B, T = q.shape[:2]
    kk = jnp.concatenate([kc.astype(k.dtype), k], axis=1)
    vv = jnp.concatenate([vc.astype(v.dtype), v], axis=1)
    k_pos = PAST_LEN - WINDOW + jnp.arange(WINDOW + T, dtype=jnp.int32)
    q_pos = PAST_LEN + jnp.arange(T, dtype=jnp.int32)
    o = _sink_alibi_attn(q, kk, vv, q_pos, k_pos, sinks)
    return o.reshape(B, T, SWA_HEADS * SWA_HD), kk[:, -WINDOW:], vv[:, -WINDOW:]


def _mixer(h, w_in, w_lr, b_lr, gla_norm, conv_w, sinks, w_branch, w_out,
           gla_s0, conv_prev, kc, vc):
    B, L, _ = h.shape
    f32 = jnp.float32
    (gq, gk, gv, gr, glr, cb, cc, ch, sq, sk, sv, gates) = jnp.split(
        h @ w_in, _split_points(), axis=-1)
    log_a = jax.nn.log_sigmoid((glr @ w_lr + b_lr).astype(f32)) / GLA_TAU
    log_a = log_a.reshape(B, L, GLA_HEADS, GLA_DK)
    q = gq.reshape(B, L, GLA_HEADS, GLA_DK).astype(f32) * (GLA_DK ** -0.5)
    k = gk.reshape(B, L, GLA_HEADS, GLA_DK).astype(f32)
    v = gv.reshape(B, L, GLA_HEADS, GLA_DV).astype(f32)
    o, gla_state = _gla(q, k, v, log_a, gla_s0.astype(f32))
    o = _rmsnorm(o, gla_norm).reshape(B, L, GLA_HEADS * GLA_DV).astype(h.dtype)
    out_a = o * jax.nn.silu(gr)
    u = cc * ch
    cat = jnp.concatenate([conv_prev.astype(u.dtype), u], axis=1)
    conv = conv_w[0] * cat[:, 0:L] + conv_w[1] * cat[:, 1:L + 1] + conv_w[2] * cat[:, 2:L + 2]
    out_b = cb * conv
    conv_state = cat[:, -(CONV_K - 1):]
    qs = sq.reshape(B, L, SWA_KV_HEADS, SWA_GROUP, SWA_HD)
    ks = sk.reshape(B, L, SWA_KV_HEADS, SWA_HD)
    vs = sv.reshape(B, L, SWA_KV_HEADS, SWA_HD)
    if kc is None:
        out_c, k_buf, v_buf = _swa_prompt(qs, ks, vs, sinks)
    else:
        out_c, k_buf, v_buf = _swa_sample(qs, ks, vs, kc, vc, sinks)
    br = jnp.stack([out_a, out_b, out_c], axis=2)
    br = jnp.einsum('blnc,ncd->blnd', br, w_branch)
    g = jax.nn.sigmoid(gates.reshape(B, L, N_BRANCH, D_MODEL))
    y = jnp.sum(g * br, axis=2) @ w_out
    return y, gla_state, conv_state, k_buf, v_buf


def _mlp(h, w_up, w_down):
    return jnp.square(jax.nn.relu(h @ w_up)) @ w_down


def setup_inputs(seed: int = 0) -> dict:
    key = jax.random.key(seed)
    ks = jax.random.split(key, 20)
    nrm = lambda k, shape, scale: jax.random.normal(k, shape, jnp.float32) * scale
    return {
        'x_prompt': nrm(ks[0], (BATCH, SEQ, D_MODEL), 1.0),
        'x_sample': nrm(ks[1], (DEC_BATCH, DEC_SEQ, D_MODEL), 1.0),
        'state_gla': nrm(ks[2], (DEPTH, DEC_BATCH, GLA_HEADS, GLA_DK, GLA_DV), 1.0),
        'state_conv': nrm(ks[3], (DEPTH, DEC_BATCH, CONV_K - 1, CONV_WIDTH), 1.0),
        'cache_k': nrm(ks[4], (DEPTH, DEC_BATCH, WINDOW, SWA_KV_HEADS, SWA_HD), 1.0),
        'cache_v': nrm(ks[5], (DEPTH, DEC_BATCH, WINDOW, SWA_KV_HEADS, SWA_HD), 1.0),
        'w_in': nrm(ks[6], (DEPTH, D_MODEL, IN_COLS), D_MODEL ** -0.5),
        'w_lr': nrm(ks[7], (DEPTH, GLA_RANK, GLA_HEADS * GLA_DK), GLA_RANK ** -0.5),
        'b_lr': nrm(ks[8], (DEPTH, GLA_HEADS * GLA_DK), 0.1),
        'gla_norm': 1.0 + nrm(ks[9], (DEPTH, GLA_DV), 0.01),
        'conv_w': nrm(ks[10], (DEPTH, CONV_K, CONV_WIDTH), CONV_K ** -0.5),
        'attn_sinks': nrm(ks[11], (DEPTH, SWA_HEADS), 1.0),
        'w_branch': nrm(ks[12], (DEPTH, N_BRANCH, BRANCH_WIDTH, D_MODEL), BRANCH_WIDTH ** -0.5),
        'w_out': nrm(ks[13], (DEPTH, D_MODEL, D_MODEL), D_MODEL ** -0.5),
        'norm_mix': 1.0 + nrm(ks[14], (DEPTH, D_MODEL), 0.01),
        'norm_mlp': 1.0 + nrm(ks[15], (DEPTH, D_MODEL), 0.01),
        'w_up': nrm(ks[16], (DEPTH, D_MODEL, D_FF), D_MODEL ** -0.5),
        'w_down': nrm(ks[17], (DEPTH, D_FF, D_MODEL), D_FF ** -0.5),
        'norm_final': 1.0 + nrm(ks[18], (D_MODEL,), 0.01),
    }


def reference(x_prompt, x_sample, state_gla, state_conv, cache_k, cache_v,
              w_in, w_lr, b_lr, gla_norm, conv_w, attn_sinks, w_branch, w_out,
              norm_mix, norm_mlp, w_up, w_down, norm_final):
    xp, xs = x_prompt, x_sample
    Bp = xp.shape[0]
    sg_p, sg_s, sc_p, sc_s, kp_l, ks_l, vp_l, vs_l = [], [], [], [], [], [], [], []
    for l in range(DEPTH):
        lw = (w_in[l], w_lr[l], b_lr[l], gla_norm[l], conv_w[l], attn_sinks[l],
              w_branch[l], w_out[l])
        s0 = jnp.zeros((Bp, GLA_HEADS, GLA_DK, GLA_DV), jnp.float32)
        c0 = jnp.zeros((Bp, CONV_K - 1, CONV_WIDTH), xp.dtype)
        yp, gp, cp, kp, vp = _mixer(_rmsnorm(xp, norm_mix[l]), *lw, s0, c0, None, None)
        ys, gs, cs, kss, vss = _mixer(_rmsnorm(xs, norm_mix[l]), *lw, state_gla[l],
                                      state_conv[l], cache_k[l], cache_v[l])
        xp = xp + yp
        xs = xs + ys
        xp = xp + _mlp(_rmsnorm(xp, norm_mlp[l]), w_up[l], w_down[l])
        xs = xs + _mlp(_rmsnorm(xs, norm_mlp[l]), w_up[l], w_down[l])
        sg_p.append(gp); sg_s.append(gs); sc_p.append(cp); sc_s.append(cs)
        kp_l.append(kp); ks_l.append(kss); vp_l.append(vp); vs_l.append(vss)
    y_prompt = _rmsnorm(xp, norm_final)
    y_sample = _rmsnorm(xs, norm_final)
    state_gla_prompt = jnp.stack(sg_p)
    state_gla_sample = jnp.stack(sg_s)
    state_conv_prompt = jnp.stack(sc_p)
    state_conv_sample = jnp.stack(sc_s)
    cache_k_prompt = jnp.stack(kp_l)
    cache_k_sample = jnp.stack(ks_l)
    cache_v_prompt = jnp.stack(vp_l)
    cache_v_sample = jnp.stack(vs_l)
    return (y_prompt, y_sample, state_gla_prompt, state_gla_sample,
            state_conv_prompt, state_conv_sample, cache_k_prompt, cache_k_sample,
            cache_v_prompt, cache_v_sample)
```

```python
import functools

import jax
import jax.numpy as jnp
from jax import lax
from jax.experimental import pallas as pl
from jax.experimental.pallas import tpu as pltpu

F32 = jnp.float32
BF16 = jnp.bfloat16
HIGHEST = lax.Precision.HIGHEST

D_MODEL = 2048
DEPTH = 4
PAST_LEN = 16384
BRANCH_WIDTH = D_MODEL // 2
N_BRANCH = 3
GLA_HEADS = 4
GLA_DV = BRANCH_WIDTH // GLA_HEADS
GLA_DK = GLA_DV // 2
GLA_RANK = 16
GLA_TAU = 16.0
GLA_CHUNK = 64
CONV_WIDTH = BRANCH_WIDTH
CONV_K = 3
SWA_HD = 64
SWA_HEADS = BRANCH_WIDTH // SWA_HD
SWA_KV_HEADS = SWA_HEADS // 4
SWA_GROUP = SWA_HEADS // SWA_KV_HEADS
WINDOW = 128
D_FF = 4 * D_MODEL
EPS = 1e-6

LANES = 128
SUBLANES = 8
VMEM_LIMIT = 56 * 1024 * 1024

GLA_QK = GLA_HEADS * GLA_DK
GLA_V = GLA_HEADS * GLA_DV
SWA_KV = SWA_KV_HEADS * SWA_HD

COL_Q = 0
COL_K = COL_Q + GLA_QK
COL_V = COL_K + GLA_QK
COL_GR = COL_V + GLA_V
COL_CB = COL_GR + GLA_V
COL_CC = COL_CB + CONV_WIDTH
COL_CH = COL_CC + CONV_WIDTH
COL_SQ = COL_CH + CONV_WIDTH
COL_SK = COL_SQ + SWA_HEADS * SWA_HD
COL_SV = COL_SK + SWA_KV
COL_GATES = COL_SV + SWA_KV
PACKED_COLS = COL_GATES + N_BRANCH * D_MODEL
LR_SRC = COL_CB

ALIBI_SLOPES = tuple(2.0 ** (-8.0 * (h + 1) / SWA_HEADS) for h in range(SWA_HEADS))

NT_DIMS = (((1,), (1,)), ((), ()))
TN_DIMS = (((0,), (0,)), ((), ()))


def _params(sem):
    return pltpu.CompilerParams(dimension_semantics=sem, vmem_limit_bytes=VMEM_LIMIT)


def _rmsnorm_rows(x, g):
    ms = jnp.mean(x * x, axis=-1, keepdims=True)
    return x * lax.rsqrt(ms + EPS) * g


def _log_sigmoid(z):
    return jnp.minimum(z, 0.0) - jnp.log1p(jnp.exp(-jnp.abs(z)))


def _inproj_kernel(x_ref, g_ref, w_ref, wlr_ref, y_ref, glr_ref, h_ref):
    @pl.when(pl.program_id(1) == 0)
    def _():
        h = _rmsnorm_rows(x_ref[...], g_ref[...]).astype(BF16)
        h_ref[...] = h
        glr_ref[...] = jnp.dot(h, wlr_ref[...], preferred_element_type=F32)

    y_ref[...] = jnp.dot(h_ref[...], w_ref[...], preferred_element_type=F32)


def _inproj(x, norm_w, w_packed, w_glr, layer, tm, tn):
    m = x.shape[0]
    return pl.pallas_call(
        _inproj_kernel,
        grid=(m // tm, PACKED_COLS // tn),
        in_specs=[
            pl.BlockSpec((tm, D_MODEL), lambda i, j: (i, 0)),
            pl.BlockSpec((None, 1, D_MODEL), lambda i, j: (layer, 0, 0)),
            pl.BlockSpec((None, D_MODEL, tn), lambda i, j: (layer, 0, j)),
            pl.BlockSpec((None, D_MODEL, LANES), lambda i, j: (layer, 0, 0)),
        ],
        out_specs=[
            pl.BlockSpec((tm, tn), lambda i, j: (i, j)),
            pl.BlockSpec((tm, LANES), lambda i, j: (i, 0)),
        ],
        out_shape=[
            jax.ShapeDtypeStruct((m, PACKED_COLS), F32),
            jax.ShapeDtypeStruct((m, LANES), F32),
        ],
        scratch_shapes=[pltpu.VMEM((tm, D_MODEL), BF16)],
        compiler_params=_params(("parallel", "arbitrary")),
        name="inproj",
    )(x, norm_w, w_packed, w_glr)


def _gla_kernel(q_ref, k_ref, v_ref, gr_ref, glr_ref, wlr_ref, blr_ref, gn_ref,
                oa_ref, sfin_ref, st_ref):
    c = pl.program_id(1)

    @pl.when(c == 0)
    def _():
        st_ref[...] = jnp.zeros_like(st_ref)

    z = jnp.dot(glr_ref[...], wlr_ref[...], precision=HIGHEST,
                preferred_element_type=F32) + blr_ref[...]
    log_a = _log_sigmoid(z) / GLA_TAU
    row = lax.broadcasted_iota(jnp.int32, (GLA_CHUNK, GLA_CHUNK), 0)
    col = lax.broadcasted_iota(jnp.int32, (GLA_CHUNK, GLA_CHUNK), 1)
    causal = row >= col
    tri = jnp.where(causal, 1.0, 0.0).astype(F32)
    b = jnp.dot(tri, log_a, precision=HIGHEST, preferred_element_type=F32)
    b_last = b[GLA_CHUNK - 1:GLA_CHUNK, :]
    q = q_ref[...] * (GLA_DK ** -0.5)
    k = k_ref[...]
    qt = (q * jnp.exp(b)).astype(BF16)
    kt = (k * jnp.exp(-b)).astype(BF16)
    kd = (k * jnp.exp(b_last - b)).astype(BF16)
    g_last = jnp.exp(b_last)
    v = v_ref[...].astype(BF16)
    gn = gn_ref[...]
    for h in range(GLA_HEADS):
        ks = slice(h * GLA_DK, (h + 1) * GLA_DK)
        vs = slice(h * GLA_DV, (h + 1) * GLA_DV)
        a = lax.dot_general(qt[:, ks], kt[:, ks], NT_DIMS, preferred_element_type=F32)
        a = jnp.where(causal, a, 0.0).astype(BF16)
        s_t = st_ref[h]
        o = lax.dot_general(qt[:, ks], s_t.astype(BF16), NT_DIMS, preferred_element_type=F32)
        o = o + jnp.dot(a, v[:, vs], preferred_element_type=F32)
        u_t = lax.dot_general(v[:, vs], kd[:, ks], TN_DIMS, preferred_element_type=F32)
        st_ref[h] = g_last[:, ks] * s_t + u_t
        gr = gr_ref[:, vs]
        oa_ref[:, vs] = (_rmsnorm_rows(o, gn) * (gr * jax.nn.sigmoid(gr))).astype(BF16)

    @pl.when(c == pl.num_programs(1) - 1)
    def _():
        for h in range(GLA_HEADS):
            sfin_ref[0, h] = st_ref[h].T


def _gla_prompt(y, glr, w_lr_p, b_lr, gla_norm, layer, batch, seq):
    nc = seq // GLA_CHUNK
    rows = lambda b, c: b * nc + c
    return pl.pallas_call(
        _gla_kernel,
        grid=(batch, nc),
        in_specs=[
            pl.BlockSpec((GLA_CHUNK, GLA_QK), lambda b, c: (rows(b, c), COL_Q // GLA_QK)),
            pl.BlockSpec((GLA_CHUNK, GLA_QK), lambda b, c: (rows(b, c), COL_K // GLA_QK)),
            pl.BlockSpec((GLA_CHUNK, GLA_V), lambda b, c: (rows(b, c), COL_V // GLA_V)),
            pl.BlockSpec((GLA_CHUNK, GLA_V), lambda b, c: (rows(b, c), COL_GR // GLA_V)),
            pl.BlockSpec((GLA_CHUNK, LANES), lambda b, c: (rows(b, c), 0)),
            pl.BlockSpec((None, LANES, GLA_QK), lambda b, c: (layer, 0, 0)),
            pl.BlockSpec((None, 1, GLA_QK), lambda b, c: (layer, 0, 0)),
            pl.BlockSpec((None, 1, GLA_DV), lambda b, c: (layer, 0, 0)),
        ],
        out_specs=[
            pl.BlockSpec((GLA_CHUNK, GLA_V), lambda b, c: (rows(b, c), 0)),
            pl.BlockSpec((1, GLA_HEADS, GLA_DK, GLA_DV), lambda b, c: (b, 0, 0, 0)),
        ],
        out_shape=[
            jax.ShapeDtypeStruct((batch * seq, GLA_V), BF16),
            jax.ShapeDtypeStruct((batch, GLA_HEADS, GLA_DK, GLA_DV), F32),
        ],
        scratch_shapes=[pltpu.VMEM((GLA_HEADS, GLA_DV, GLA_DK), F32)],
        compiler_params=_params(("parallel", "arbitrary")),
        name="gla_prompt",
    )(y, y, y, y, glr, w_lr_p, b_lr, gla_norm)


def _conv_kernel(cb_ref, cc_ref, ch_ref, w_ref, ob_ref, cs_ref, prev_ref):
    @pl.when(pl.program_id(1) == 0)
    def _():
        prev_ref[...] = jnp.zeros_like(prev_ref)

    u = cc_ref[...] * ch_ref[...]
    tl = u.shape[0]
    prev = prev_ref[...]
    p_m1 = prev[SUBLANES - 1:SUBLANES, :]
    p_m2 = prev[SUBLANES - 2:SUBLANES - 1, :]
    row = lax.broadcasted_iota(jnp.int32, u.shape, 0)
    u1 = jnp.where(row == 0, p_m1, pltpu.roll(u, 1, 0))
    u2 = jnp.where(row == 0, p_m2, jnp.where(row == 1, p_m1, pltpu.roll(u, 2, 0)))
    w = w_ref[...]
    conv = w[0:1, :] * u2 + w[1:2, :] * u1 + w[2:3, :] * u
    ob_ref[...] = (cb_ref[...] * conv).astype(BF16)
    prev_ref[...] = u[tl - SUBLANES:tl, :]
    cs_ref[0] = u[tl - (CONV_K - 1):tl, :]


def _conv_prompt(y, conv_w, layer, batch, seq, tl):
    nt = seq // tl
    rows = lambda b, t: b * nt + t
    return pl.pallas_call(
        _conv_kernel,
        grid=(batch, nt),
        in_specs=[
            pl.BlockSpec((tl, CONV_WIDTH), lambda b, t: (rows(b, t), COL_CB // CONV_WIDTH)),
            pl.BlockSpec((tl, CONV_WIDTH), lambda b, t: (rows(b, t), COL_CC // CONV_WIDTH)),
            pl.BlockSpec((tl, CONV_WIDTH), lambda b, t: (rows(b, t), COL_CH // CONV_WIDTH)),
            pl.BlockSpec((None, CONV_K, CONV_WIDTH), lambda b, t: (layer, 0, 0)),
        ],
        out_specs=[
            pl.BlockSpec((tl, CONV_WIDTH), lambda b, t: (rows(b, t), 0)),
            pl.BlockSpec((1, CONV_K - 1, CONV_WIDTH), lambda b, t: (b, 0, 0)),
        ],
        out_shape=[
            jax.ShapeDtypeStruct((batch * seq, CONV_WIDTH), BF16),
            jax.ShapeDtypeStruct((batch, CONV_K - 1, CONV_WIDTH), F32),
        ],
        scratch_shapes=[pltpu.VMEM((SUBLANES, CONV_WIDTH), F32)],
        compiler_params=_params(("parallel", "arbitrary")),
        name="conv_prompt",
    )(y, y, y, conv_w)


def _swa_kernel(sinks_ref, q_ref, kc_ref, kp_ref, vc_ref, vp_ref, o_ref, *, layer):
    blk = pl.program_id(1)
    row = lax.broadcasted_iota(jnp.int32, (WINDOW, WINDOW), 0)
    col = lax.broadcasted_iota(jnp.int32, (WINDOW, WINDOW), 1)
    dist_c = (row - col).astype(F32)
    dist_p = dist_c + float(WINDOW)
    valid_c = row >= col
    valid_p = jnp.logical_and(col > row, blk > 0)
    scale = SWA_HD ** -0.5
    outs = []
    for j in range(SWA_KV_HEADS):
        cs = slice(j * SWA_HD, (j + 1) * SWA_HD)
        kc = kc_ref[:, cs].astype(BF16)
        kp = kp_ref[:, cs].astype(BF16)
        vc = vc_ref[:, cs].astype(BF16)
        vp = vp_ref[:, cs].astype(BF16)
        for g in range(SWA_GROUP):
            h = j * SWA_GROUP + g
            qh = q_ref[:, h * SWA_HD:(h + 1) * SWA_HD].astype(BF16)
            slope = ALIBI_SLOPES[h]
            s_c = lax.dot_general(qh, kc, NT_DIMS, preferred_element_type=F32) * scale
            s_p = lax.dot_general(qh, kp, NT_DIMS, preferred_element_type=F32) * scale
            s_c = jnp.where(valid_c, s_c - slope * dist_c, -jnp.inf)
            s_p = jnp.where(valid_p, s_p - slope * dist_p, -jnp.inf)
            sink = sinks_ref[layer, h]
            m = jnp.maximum(jnp.maximum(jnp.max(s_c, axis=-1, keepdims=True),
                                        jnp.max(s_p, axis=-1, keepdims=True)), sink)
            p_c = jnp.exp(s_c - m)
            p_p = jnp.exp(s_p - m)
            den = (jnp.sum(p_c, axis=-1, keepdims=True) + jnp.sum(p_p, axis=-1, keepdims=True)
                   + jnp.exp(sink - m))
            inv = 1.0 / den
            o = jnp.dot((p_c * inv).astype(BF16), vc, preferred_element_type=F32)
            o = o + jnp.dot((p_p * inv).astype(BF16), vp, preferred_element_type=F32)
            outs.append(o)
    o_ref[...] = jnp.concatenate(outs, axis=1).astype(BF16)


def _swa_prompt(y, sinks, layer, batch, seq):
    nb = seq // WINDOW
    rows = lambda b, i: b * nb + i
    prev_rows = lambda b, i: b * nb + jnp.maximum(i - 1, 0)
    qw = SWA_HEADS * SWA_HD
    return pl.pallas_call(
        functools.partial(_swa_kernel, layer=layer),
        grid=(batch, nb),
        in_specs=[
            pl.BlockSpec(memory_space=pltpu.SMEM),
            pl.BlockSpec((WINDOW, qw), lambda b, i: (rows(b, i), COL_SQ // qw)),
            pl.BlockSpec((WINDOW, SWA_KV), lambda b, i: (rows(b, i), COL_SK // SWA_KV)),
            pl.BlockSpec((WINDOW, SWA_KV), lambda b, i: (prev_rows(b, i), COL_SK // SWA_KV)),
            pl.BlockSpec((WINDOW, SWA_KV), lambda b, i: (rows(b, i), COL_SV // SWA_KV)),
            pl.BlockSpec((WINDOW, SWA_KV), lambda b, i: (prev_rows(b, i), COL_SV // SWA_KV)),
        ],
        out_specs=pl.BlockSpec((WINDOW, qw), lambda b, i: (rows(b, i), 0)),
        out_shape=jax.ShapeDtypeStruct((batch * seq, qw), BF16),
        compiler_params=_params(("parallel", "arbitrary")),
        name="swa_prompt",
    )(sinks, y, y, y, y, y)


def _first_row_only(x8):
    row = lax.broadcasted_iota(jnp.int32, x8.shape, 0)
    return jnp.where(row == 0, x8, 0.0)


def _rows8(x):
    return jnp.broadcast_to(x, (SUBLANES, x.shape[-1]))


def _sample_kernel(sinks_ref, q_ref, k_ref, v_ref, gr_ref, glr_ref, cb_ref, cc_ref, ch_ref,
                   sq_ref, sk_ref, sv_ref, wlr_ref, blr_ref, gn_ref, cw_ref,
                   s_ref, cs_ref, kc_ref, vc_ref,
                   oa_ref, ob_ref, oc_ref, so_ref, cso_ref, kco_ref, vco_ref, *, layer):
    z = jnp.dot(_rows8(glr_ref[0]), wlr_ref[...], precision=HIGHEST,
                preferred_element_type=F32) + blr_ref[...]
    b = _log_sigmoid(z) / GLA_TAU
    q = _rows8(q_ref[0]) * (GLA_DK ** -0.5)
    k = _rows8(k_ref[0])
    v = _rows8(v_ref[0])
    gr = _rows8(gr_ref[0])
    qt = q * jnp.exp(b)
    kt = k * jnp.exp(-b)
    kd = k * jnp.exp(b - b)
    g_last = jnp.exp(b)
    gn = gn_ref[...]
    ones_v = jnp.ones((SUBLANES, GLA_DV), F32)
    oa = []
    for h in range(GLA_HEADS):
        ks = slice(h * GLA_DK, (h + 1) * GLA_DK)
        vs = slice(h * GLA_DV, (h + 1) * GLA_DV)
        s_in = s_ref[0, h]
        a = jnp.sum(qt[:, ks] * kt[:, ks], axis=-1, keepdims=True)
        o = jnp.dot(qt[:, ks], s_in, precision=HIGHEST, preferred_element_type=F32)
        o = o + a * v[:, vs]
        g_mat = lax.dot_general(_first_row_only(g_last[:, ks]), ones_v, TN_DIMS,
                                precision=HIGHEST, preferred_element_type=F32)
        u = lax.dot_general(_first_row_only(kd[:, ks]), v[:, vs], TN_DIMS,
                            precision=HIGHEST, preferred_element_type=F32)
        so_ref[0, h] = g_mat * s_in + u
        grh = gr[:, vs]
        oa.append(_rmsnorm_rows(o, gn) * (grh * jax.nn.sigmoid(grh)))
    oa_ref[0] = jnp.concatenate(oa, axis=1)[0:1, :]

    u_new = cc_ref[0] * ch_ref[0]
    prev = cs_ref[0]
    w = cw_ref[...]
    conv = w[0:1, :] * prev[0:1, :] + w[1:2, :] * prev[1:2, :] + w[2:3, :] * u_new
    ob_ref[0] = cb_ref[0] * conv
    cso_ref[0] = jnp.concatenate([prev[1:2, :], u_new], axis=0)

    rowk = lax.broadcasted_iota(jnp.int32, (WINDOW, SWA_KV), 0)
    k_new = jnp.where(rowk == WINDOW - 1, sk_ref[0], pltpu.roll(kc_ref[0], WINDOW - 1, 0))
    v_new = jnp.where(rowk == WINDOW - 1, sv_ref[0], pltpu.roll(vc_ref[0], WINDOW - 1, 0))
    kco_ref[0] = k_new
    vco_ref[0] = v_new
    key = lax.broadcasted_iota(jnp.int32, (SUBLANES, WINDOW), 1)
    dist = (WINDOW - 1 - key).astype(F32)
    sq = _rows8(sq_ref[0])
    scale = SWA_HD ** -0.5
    oc = []
    for j in range(SWA_KV_HEADS):
        cs = slice(j * SWA_HD, (j + 1) * SWA_HD)
        kj = k_new[:, cs]
        vj = v_new[:, cs]
        for g in range(SWA_GROUP):
            h = j * SWA_GROUP + g
            qh = sq[:, h * SWA_HD:(h + 1) * SWA_HD]
            s = lax.dot_general(qh, kj, NT_DIMS, precision=HIGHEST,
                                preferred_element_type=F32) * scale
            s = s - ALIBI_SLOPES[h] * dist
            sink = sinks_ref[layer, h]
            m = jnp.maximum(jnp.max(s, axis=-1, keepdims=True), sink)
            p = jnp.exp(s - m)
            den = jnp.sum(p, axis=-1, keepdims=True) + jnp.exp(sink - m)
            p = p * (1.0 / den)
            oc.append(jnp.dot(p, vj, precision=HIGHEST, preferred_element_type=F32))
    oc_ref[0] = jnp.concatenate(oc, axis=1)[0:1, :]


def _sample_mixer(y, glr, sinks, w_lr_p, b_lr, gla_norm, conv_w,
                  state_gla, state_conv, cache_k, cache_v, layer):
    nb = y.shape[0]
    y3 = y.reshape(nb, 1, PACKED_COLS)
    glr3 = glr.reshape(nb, 1, LANES)
    qw = SWA_HEADS * SWA_HD

    def seg(width, col):
        return pl.BlockSpec((1, 1, width), lambda b: (b, 0, col // width))

    def per_layer(shape):
        return pl.BlockSpec((None,) + shape, lambda b: (layer,) + (0,) * len(shape))

    def state(shape):
        return pl.BlockSpec((None, 1) + shape, lambda b: (layer, b) + (0,) * len(shape))

    def out(shape):
        return pl.BlockSpec((1,) + shape, lambda b: (b,) + (0,) * len(shape))

    return pl.pallas_call(
        functools.partial(_sample_kernel, layer=layer),
        grid=(nb,),
        in_specs=[
            pl.BlockSpec(memory_space=pltpu.SMEM),
            seg(GLA_QK, COL_Q), seg(GLA_QK, COL_K), seg(GLA_V, COL_V), seg(GLA_V, COL_GR),
            pl.BlockSpec((1, 1, LANES), lambda b: (b, 0, 0)),
            seg(CONV_WIDTH, COL_CB), seg(CONV_WIDTH, COL_CC), seg(CONV_WIDTH, COL_CH),
            seg(qw, COL_SQ), seg(SWA_KV, COL_SK), seg(SWA_KV, COL_SV),
            per_layer((LANES, GLA_QK)), per_layer((1, GLA_QK)), per_layer((1, GLA_DV)),
            per_layer((CONV_K, CONV_WIDTH)),
            state((GLA_HEADS, GLA_DK, GLA_DV)), state((CONV_K - 1, CONV_WIDTH)),
            state((WINDOW, SWA_KV)), state((WINDOW, SWA_KV)),
        ],
        out_specs=[
            out((1, GLA_V)), out((1, CONV_WIDTH)), out((1, qw)),
            out((GLA_HEADS, GLA_DK, GLA_DV)), out((CONV_K - 1, CONV_WIDTH)),
            out((WINDOW, SWA_KV)), out((WINDOW, SWA_KV)),
        ],
        out_shape=[
            jax.ShapeDtypeStruct((nb, 1, GLA_V), F32),
            jax.ShapeDtypeStruct((nb, 1, CONV_WIDTH), F32),
            jax.ShapeDtypeStruct((nb, 1, qw), F32),
            jax.ShapeDtypeStruct((nb, GLA_HEADS, GLA_DK, GLA_DV), F32),
            jax.ShapeDtypeStruct((nb, CONV_K - 1, CONV_WIDTH), F32),
            jax.ShapeDtypeStruct((nb, WINDOW, SWA_KV), F32),
            jax.ShapeDtypeStruct((nb, WINDOW, SWA_KV), F32),
        ],
        compiler_params=_params(("parallel",)),
        name="sample_mixer",
    )(sinks, y3, y3, y3, y3, glr3, y3, y3, y3, y3, y3, y3,
      w_lr_p, b_lr, gla_norm, conv_w, state_gla, state_conv, cache_k, cache_v)


def _merge_kernel(a_ref, b_ref, c_ref, g0_ref, g1_ref, g2_ref, wb_ref, wo_ref, x_ref,
                  o_ref, acc_ref):
    t = pl.program_id(1)

    @pl.when(t == 0)
    def _():
        acc_ref[...] = jnp.zeros_like(acc_ref)

    def branch(src_ref, gate_ref, n):
        br = jnp.dot(src_ref[...].astype(BF16), wb_ref[n], preferred_element_type=F32)
        return jax.nn.sigmoid(gate_ref[...]) * br

    mixed = branch(a_ref, g0_ref, 0) + branch(b_ref, g1_ref, 1) + branch(c_ref, g2_ref, 2)
    acc_ref[...] += jnp.dot(mixed.astype(BF16), wo_ref[...], preferred_element_type=F32)

    @pl.when(t == pl.num_programs(1) - 1)
    def _():
        o_ref[...] = x_ref[...] + acc_ref[...]


def _merge(out_a, out_b, out_c, y, x, w_branch, w_out, layer, tm, tn):
    m = x.shape[0]
    gate_col = lambda n: (COL_GATES + n * D_MODEL) // tn
    src = pl.BlockSpec((tm, BRANCH_WIDTH), lambda i, t: (i, 0))
    gate = lambda n: pl.BlockSpec((tm, tn), lambda i, t: (i, gate_col(n) + t))
    return pl.pallas_call(
        _merge_kernel,
        grid=(m // tm, D_MODEL // tn),
        in_specs=[
            src, src, src, gate(0), gate(1), gate(2),
            pl.BlockSpec((None, N_BRANCH, BRANCH_WIDTH, tn), lambda i, t: (layer, 0, 0, t)),
            pl.BlockSpec((None, tn, D_MODEL), lambda i, t: (layer, t, 0)),
            pl.BlockSpec((tm, D_MODEL), lambda i, t: (i, 0)),
        ],
        out_specs=pl.BlockSpec((tm, D_MODEL), lambda i, t: (i, 0)),
        out_shape=jax.ShapeDtypeStruct((m, D_MODEL), F32),
        scratch_shapes=[pltpu.VMEM((tm, D_MODEL), F32)],
        compiler_params=_params(("parallel", "arbitrary")),
        name="merge_outproj",
    )(out_a, out_b, out_c, y, y, y, w_branch, w_out, x)


def _mlp_kernel(x_ref, g_ref, wu_ref, wd_ref, o_ref, h_ref, acc_ref):
    f = pl.program_id(1)

    @pl.when(f == 0)
    def _():
        h_ref[...] = _rmsnorm_rows(x_ref[...], g_ref[...]).astype(BF16)
        acc_ref[...] = jnp.zeros_like(acc_ref)

    up = jnp.dot(h_ref[...], wu_ref[...], preferred_element_type=F32)
    act = jnp.square(jnp.maximum(up, 0.0)).astype(BF16)
    acc_ref[...] += jnp.dot(act, wd_ref[...], preferred_element_type=F32)

    @pl.when(f == pl.num_programs(1) - 1)
    def _():
        o_ref[...] = x_ref[...] + acc_ref[...]


def _mlp(x, norm_w, w_up, w_down, layer, tm, tf):
    m = x.shape[0]
    return pl.pallas_call(
        _mlp_kernel,
        grid=(m // tm, D_FF // tf),
        in_specs=[
            pl.BlockSpec((tm, D_MODEL), lambda i, f: (i, 0)),
            pl.BlockSpec((None, 1, D_MODEL), lambda i, f: (layer, 0, 0)),
            pl.BlockSpec((None, D_MODEL, tf), lambda i, f: (layer, 0, f)),
            pl.BlockSpec((None, tf, D_MODEL), lambda i, f: (layer, f, 0)),
        ],
        out_specs=pl.BlockSpec((tm, D_MODEL), lambda i, f: (i, 0)),
        out_shape=jax.ShapeDtypeStruct((m, D_MODEL), F32),
        scratch_shapes=[pltpu.VMEM((tm, D_MODEL), BF16), pltpu.VMEM((tm, D_MODEL), F32)],
        compiler_params=_params(("parallel", "arbitrary")),
        name="mlp",
    )(x, norm_w, w_up, w_down)


def _final_norm_kernel(x_ref, g_ref, o_ref):
    o_ref[...] = _rmsnorm_rows(x_ref[...], g_ref[...])


def _final_norm(x, g, tm):
    m = x.shape[0]
    return pl.pallas_call(
        _final_norm_kernel,
        grid=(m // tm,),
        in_specs=[pl.BlockSpec((tm, D_MODEL), lambda i: (i, 0)),
                  pl.BlockSpec((1, D_MODEL), lambda i: (0, 0))],
        out_specs=pl.BlockSpec((tm, D_MODEL), lambda i: (i, 0)),
        out_shape=jax.ShapeDtypeStruct((m, D_MODEL), F32),
        compiler_params=_params(("parallel",)),
        name="final_norm",
    )(x, g)


def kernel(x_prompt, x_sample, state_gla, state_conv, cache_k, cache_v, w_in, w_lr, b_lr,
           gla_norm, conv_w, attn_sinks, w_branch, w_out, norm_mix, norm_mlp, w_up, w_down,
           norm_final):
    batch, seq, _ = x_prompt.shape
    nb = x_sample.shape[0]
    mp = batch * seq

    w_packed = jnp.concatenate(
        [w_in[:, :, :LR_SRC], w_in[:, :, LR_SRC + GLA_RANK:]], axis=2).astype(BF16)
    w_glr = jnp.pad(w_in[:, :, LR_SRC:LR_SRC + GLA_RANK],
                    ((0, 0), (0, 0), (0, LANES - GLA_RANK))).astype(BF16)
    w_lr_p = jnp.pad(w_lr, ((0, 0), (0, LANES - GLA_RANK), (0, 0)))
    w_branch_b = w_branch.astype(BF16)
    w_out_b = w_out.astype(BF16)
    w_up_b = w_up.astype(BF16)
    w_down_b = w_down.astype(BF16)
    b_lr3 = b_lr.reshape(DEPTH, 1, GLA_QK)
    gla_norm3 = gla_norm.reshape(DEPTH, 1, GLA_DV)
    norm_mix3 = norm_mix.reshape(DEPTH, 1, D_MODEL)
    norm_mlp3 = norm_mlp.reshape(DEPTH, 1, D_MODEL)
    cache_k4 = cache_k.reshape(DEPTH, nb, WINDOW, SWA_KV)
    cache_v4 = cache_v.reshape(DEPTH, nb, WINDOW, SWA_KV)

    xp = x_prompt.reshape(mp, D_MODEL)
    xs = x_sample.reshape(nb, D_MODEL)
    outs = [[] for _ in range(8)]
    for l in range(DEPTH):
        yp, glrp = _inproj(xp, norm_mix3, w_packed, w_glr, l, tm=512, tn=768)
        oa, sg_p = _gla_prompt(yp, glrp, w_lr_p, b_lr3, gla_norm3, l, batch, seq)
        ob, sc_p = _conv_prompt(yp, conv_w, l, batch, seq, tl=512)
        oc = _swa_prompt(yp, attn_sinks, l, batch, seq)
        yp3 = yp.reshape(batch, seq, PACKED_COLS)
        kp = yp3[:, seq - WINDOW:, COL_SK:COL_SK + SWA_KV]
        vp = yp3[:, seq - WINDOW:, COL_SV:COL_SV + SWA_KV]
        xp = _merge(oa, ob, oc, yp, xp, w_branch_b, w_out_b, l, tm=512, tn=512)
        xp = _mlp(xp, norm_mlp3, w_up_b, w_down_b, l, tm=512, tf=512)
        ys, glrs = _inproj(xs, norm_mix3, w_packed, w_glr, l, tm=nb, tn=768)
        sa, sb, sc, sg_s, sc_s, ks, vs = _sample_mixer(
            ys, glrs, attn_sinks, w_lr_p, b_lr3, gla_norm3, conv_w,
            state_gla, state_conv, cache_k4, cache_v4, l)
        xs = _merge(sa.reshape(nb, GLA_V), sb.reshape(nb, CONV_WIDTH),
                    sc.reshape(nb, BRANCH_WIDTH), ys, xs, w_branch_b, w_out_b, l, tm=nb, tn=512)
        xs = _mlp(xs, norm_mlp3, w_up_b, w_down_b, l, tm=nb, tf=512)
        for lst, val in zip(outs, (
                sg_p, sg_s, sc_p, sc_s,
                kp.reshape(batch, WINDOW, SWA_KV_HEADS, SWA_HD),
                ks.reshape(nb, WINDOW, SWA_KV_HEADS, SWA_HD),
                vp.reshape(batch, WINDOW, SWA_KV_HEADS, SWA_HD),
                vs.reshape(nb, WINDOW, SWA_KV_HEADS, SWA_HD))):
            lst.append(val)
    y_prompt = _final_norm(xp, norm_final.reshape(1, D_MODEL), tm=512).reshape(batch, seq, D_MODEL)
    y_sample = _final_norm(xs, norm_final.reshape(1, D_MODEL), tm=nb).reshape(nb, 1, D_MODEL)
    return (y_prompt, y_sample) + tuple(jnp.stack(o) for o in outs)
```

```python
import functools

import jax
import jax.numpy as jnp
from jax import lax
from jax.experimental import pallas as pl
from jax.experimental.pallas import tpu as pltpu

F32 = jnp.float32
BF16 = jnp.bfloat16
HIGHEST = lax.Precision.HIGHEST

D_MODEL = 2048
DEPTH = 4
PAST_LEN = 16384
BRANCH_WIDTH = D_MODEL // 2
N_BRANCH = 3
GLA_HEADS = 4
GLA_DV = BRANCH_WIDTH // GLA_HEADS
GLA_DK = GLA_DV // 2
GLA_RANK = 16
GLA_TAU = 16.0
GLA_CHUNK = 64
CONV_WIDTH = BRANCH_WIDTH
CONV_K = 3
SWA_HD = 64
SWA_HEADS = BRANCH_WIDTH // SWA_HD
SWA_KV_HEADS = SWA_HEADS // 4
SWA_GROUP = SWA_HEADS // SWA_KV_HEADS
WINDOW = 128
D_FF = 4 * D_MODEL
EPS = 1e-6

LANES = 128
SUBLANES = 8
VMEM_LIMIT = 56 * 1024 * 1024

GLA_QK = GLA_HEADS * GLA_DK
GLA_V = GLA_HEADS * GLA_DV
SWA_Q = SWA_HEADS * SWA_HD
SWA_KV = SWA_KV_HEADS * SWA_HD

COL_Q = 0
COL_K = COL_Q + GLA_QK
COL_V = COL_K + GLA_QK
COL_GR = COL_V + GLA_V
COL_CB = COL_GR + GLA_V
COL_CC = COL_CB + CONV_WIDTH
COL_CH = COL_CC + CONV_WIDTH
COL_SQ = COL_CH + CONV_WIDTH
COL_SK = COL_SQ + SWA_Q
COL_SV = COL_SK + SWA_KV
COL_GATES = COL_SV + SWA_KV
PACKED_COLS = COL_GATES + N_BRANCH * D_MODEL
LR_SRC = COL_CB

ALIBI_SLOPES = tuple(2.0 ** (-8.0 * (h + 1) / SWA_HEADS) for h in range(SWA_HEADS))
SWA_SCALE = SWA_HD ** -0.5

NT_DIMS = (((1,), (1,)), ((), ()))
TN_DIMS = (((0,), (0,)), ((), ()))


def _params(sem):
    return pltpu.CompilerParams(dimension_semantics=sem, vmem_limit_bytes=VMEM_LIMIT)


def _rmsnorm_rows(x, g):
    ms = jnp.mean(x * x, axis=-1, keepdims=True)
    return x * lax.rsqrt(ms + EPS) * g


def _log_sigmoid(z):
    return jnp.minimum(z, 0.0) - jnp.log1p(jnp.exp(-jnp.abs(z)))


def _silu(x):
    return x * jax.nn.sigmoid(x)


def _lr_weight_spec(layer):
    return pl.BlockSpec((None, D_MODEL, LANES), lambda *a: (layer, 0, LR_SRC // LANES))


def _norm_kernel(x_ref, g_ref, wlr_ref, h_ref, glr_ref):
    h = _rmsnorm_rows(x_ref[...], g_ref[...]).astype(BF16)
    h_ref[...] = h
    glr_ref[...] = jnp.dot(h, wlr_ref[...].astype(BF16), preferred_element_type=F32)


def _norm(x, norm_w, w_in, layer, tm):
    m = x.shape[0]
    return pl.pallas_call(
        _norm_kernel,
        grid=(m // tm,),
        in_specs=[
            pl.BlockSpec((tm, D_MODEL), lambda i: (i, 0)),
            pl.BlockSpec((None, 1, D_MODEL), lambda i: (layer, 0, 0)),
            _lr_weight_spec(layer),
        ],
        out_specs=[
            pl.BlockSpec((tm, D_MODEL), lambda i: (i, 0)),
            pl.BlockSpec((tm, LANES), lambda i: (i, 0)),
        ],
        out_shape=[
            jax.ShapeDtypeStruct((m, D_MODEL), BF16),
            jax.ShapeDtypeStruct((m, LANES), F32),
        ],
        compiler_params=_params(("parallel",)),
        name="norm",
    )(x, norm_w, w_in)


def _inproj_kernel(h_ref, wa_ref, wb_ref, y_ref, wbf_ref, *, n_plain):
    j = pl.program_id(0)
    tn = wbf_ref.shape[1]

    @pl.when(jnp.logical_and(pl.program_id(1) == 0, j < n_plain))
    def _():
        wbf_ref[...] = wa_ref[...].astype(BF16)

    @pl.when(jnp.logical_and(pl.program_id(1) == 0, j >= n_plain))
    def _():
        src = jnp.concatenate([wa_ref[...], wb_ref[...]], axis=1)
        wbf_ref[...] = src[:, GLA_RANK:GLA_RANK + tn].astype(BF16)

    y_ref[...] = jnp.dot(h_ref[...], wbf_ref[...], preferred_element_type=F32)


def _inproj(h, w_in, layer, tm, tn):
    m = h.shape[0]
    return pl.pallas_call(
        functools.partial(_inproj_kernel, n_plain=LR_SRC // tn),
        grid=(PACKED_COLS // tn, m // tm),
        in_specs=[
            pl.BlockSpec((tm, D_MODEL), lambda j, i: (i, 0)),
            pl.BlockSpec((None, D_MODEL, tn), lambda j, i: (layer, 0, j)),
            pl.BlockSpec((None, D_MODEL, LANES), lambda j, i: (layer, 0, (j + 1) * (tn // LANES))),
        ],
        out_specs=pl.BlockSpec((tm, tn), lambda j, i: (i, j)),
        out_shape=jax.ShapeDtypeStruct((m, PACKED_COLS), F32),
        scratch_shapes=[pltpu.VMEM((D_MODEL, tn), BF16)],
        compiler_params=_params(("parallel", "arbitrary")),
        name="inproj",
    )(h, w_in, w_in)


GLA_STEP_CHUNKS = 4


def _gla_kernel(q_ref, k_ref, v_ref, gr_ref, glr_ref, wlr_ref, blr_ref, gn_ref,
                oa_ref, sfin_ref, st_ref):
    c = pl.program_id(1)
    cs = GLA_CHUNK

    @pl.when(c == 0)
    def _():
        st_ref[...] = jnp.zeros_like(st_ref)

    z = jnp.dot(glr_ref[...], wlr_ref[...], precision=HIGHEST,
                preferred_element_type=F32) + blr_ref[...]
    log_a = _log_sigmoid(z) / GLA_TAU
    row = lax.broadcasted_iota(jnp.int32, (cs, cs), 0)
    col = lax.broadcasted_iota(jnp.int32, (cs, cs), 1)
    causal = row >= col
    tri = jnp.where(causal, 1.0, 0.0).astype(F32)
    gn = gn_ref[...]

    qt, kt, kd, vv, g_last = [], [], [], [], []
    for n in range(GLA_STEP_CHUNKS):
        rs = slice(n * cs, (n + 1) * cs)
        b = jnp.dot(tri, log_a[rs, :], precision=HIGHEST, preferred_element_type=F32)
        b_last = b[cs - 1:cs, :]
        q = q_ref[rs, :] * (GLA_DK ** -0.5)
        k = k_ref[rs, :]
        qt.append((q * jnp.exp(b)).astype(BF16))
        kt.append((k * jnp.exp(-b)).astype(BF16))
        kd.append((k * jnp.exp(b_last - b)).astype(BF16))
        g_last.append(jnp.exp(b_last))
        vv.append(v_ref[rs, :].astype(BF16))

    for h in range(GLA_HEADS):
        ks = slice(h * GLA_DK, (h + 1) * GLA_DK)
        vs = slice(h * GLA_DV, (h + 1) * GLA_DV)
        o_intra, u_t = [], []
        for n in range(GLA_STEP_CHUNKS):
            a = lax.dot_general(qt[n][:, ks], kt[n][:, ks], NT_DIMS, preferred_element_type=F32)
            a = jnp.where(causal, a, 0.0).astype(BF16)
            o_intra.append(jnp.dot(a, vv[n][:, vs], preferred_element_type=F32))
            u_t.append(lax.dot_general(vv[n][:, vs], kd[n][:, ks], TN_DIMS,
                                       preferred_element_type=F32))
        s_t = st_ref[h]
        for n in range(GLA_STEP_CHUNKS):
            rs = slice(n * cs, (n + 1) * cs)
            o = lax.dot_general(qt[n][:, ks], s_t.astype(BF16), NT_DIMS,
                                preferred_element_type=F32) + o_intra[n]
            s_t = g_last[n][:, ks] * s_t + u_t[n]
            oa_ref[rs, vs] = (_rmsnorm_rows(o, gn) * _silu(gr_ref[rs, vs])).astype(BF16)
        st_ref[h] = s_t

    @pl.when(c == pl.num_programs(1) - 1)
    def _():
        for h in range(GLA_HEADS):
            sfin_ref[0, h] = st_ref[h].T


def _gla_prompt(y, glr, w_lr_p, b_lr, gla_norm, layer, batch, seq):
    tr = GLA_STEP_CHUNKS * GLA_CHUNK
    ns = seq // tr
    rows = lambda b, c: b * ns + c
    return pl.pallas_call(
        _gla_kernel,
        grid=(batch, ns),
        in_specs=[
            pl.BlockSpec((tr, GLA_QK), lambda b, c: (rows(b, c), COL_Q // GLA_QK)),
            pl.BlockSpec((tr, GLA_QK), lambda b, c: (rows(b, c), COL_K // GLA_QK)),
            pl.BlockSpec((tr, GLA_V), lambda b, c: (rows(b, c), COL_V // GLA_V)),
            pl.BlockSpec((tr, GLA_V), lambda b, c: (rows(b, c), COL_GR // GLA_V)),
            pl.BlockSpec((tr, LANES), lambda b, c: (rows(b, c), 0)),
            pl.BlockSpec((None, LANES, GLA_QK), lambda b, c: (layer, 0, 0)),
            pl.BlockSpec((None, 1, GLA_QK), lambda b, c: (layer, 0, 0)),
            pl.BlockSpec((None, 1, GLA_DV), lambda b, c: (layer, 0, 0)),
        ],
        out_specs=[
            pl.BlockSpec((tr, GLA_V), lambda b, c: (rows(b, c), 0)),
            pl.BlockSpec((1, GLA_HEADS, GLA_DK, GLA_DV), lambda b, c: (b, 0, 0, 0)),
        ],
        out_shape=[
            jax.ShapeDtypeStruct((batch * seq, GLA_V), BF16),
            jax.ShapeDtypeStruct((batch, GLA_HEADS, GLA_DK, GLA_DV), F32),
        ],
        scratch_shapes=[pltpu.VMEM((GLA_HEADS, GLA_DV, GLA_DK), F32)],
        compiler_params=_params(("parallel", "arbitrary")),
        name="gla_prompt",
    )(y, y, y, y, glr, w_lr_p, b_lr, gla_norm)


def _conv_kernel(cb_ref, cc_ref, ch_ref, w_ref, ob_ref, cs_ref, prev_ref):
    @pl.when(pl.program_id(1) == 0)
    def _():
        prev_ref[...] = jnp.zeros_like(prev_ref)

    u = cc_ref[...] * ch_ref[...]
    tl = u.shape[0]
    prev = prev_ref[...]
    p_m1 = prev[SUBLANES - 1:SUBLANES, :]
    p_m2 = prev[SUBLANES - 2:SUBLANES - 1, :]
    row = lax.broadcasted_iota(jnp.int32, u.shape, 0)
    u1 = jnp.where(row == 0, p_m1, pltpu.roll(u, 1, 0))
    u2 = jnp.where(row == 0, p_m2, jnp.where(row == 1, p_m1, pltpu.roll(u, 2, 0)))
    w = w_ref[...]
    conv = w[0:1, :] * u2 + w[1:2, :] * u1 + w[2:3, :] * u
    ob_ref[...] = (cb_ref[...] * conv).astype(BF16)
    prev_ref[...] = u[tl - SUBLANES:tl, :]
    cs_ref[0] = u[tl - (CONV_K - 1):tl, :]


def _conv_prompt(y, conv_w, layer, batch, seq, tl):
    nt = seq // tl
    rows = lambda b, t: b * nt + t
    return pl.pallas_call(
        _conv_kernel,
        grid=(batch, nt),
        in_specs=[
            pl.BlockSpec((tl, CONV_WIDTH), lambda b, t: (rows(b, t), COL_CB // CONV_WIDTH)),
            pl.BlockSpec((tl, CONV_WIDTH), lambda b, t: (rows(b, t), COL_CC // CONV_WIDTH)),
            pl.BlockSpec((tl, CONV_WIDTH), lambda b, t: (rows(b, t), COL_CH // CONV_WIDTH)),
            pl.BlockSpec((None, CONV_K, CONV_WIDTH), lambda b, t: (layer, 0, 0)),
        ],
        out_specs=[
            pl.BlockSpec((tl, CONV_WIDTH), lambda b, t: (rows(b, t), 0)),
            pl.BlockSpec((1, CONV_K - 1, CONV_WIDTH), lambda b, t: (b, 0, 0)),
        ],
        out_shape=[
            jax.ShapeDtypeStruct((batch * seq, CONV_WIDTH), BF16),
            jax.ShapeDtypeStruct((batch, CONV_K - 1, CONV_WIDTH), F32),
        ],
        scratch_shapes=[pltpu.VMEM((SUBLANES, CONV_WIDTH), F32)],
        compiler_params=_params(("parallel", "arbitrary")),
        name="conv_prompt",
    )(y, y, y, conv_w)


def _swa_kernel(sinks_ref, q_ref, kc_ref, kp_ref, vc_ref, vp_ref, o_ref, bias_ref, *, layer):
    blk = pl.program_id(1)

    @pl.when(blk == 0)
    def _():
        row = lax.broadcasted_iota(jnp.int32, (WINDOW, WINDOW), 0)
        col = lax.broadcasted_iota(jnp.int32, (WINDOW, WINDOW), 1)
        dist_c = (row - col).astype(F32)
        dist_p = dist_c + float(WINDOW)
        for h in range(SWA_HEADS):
            bias_ref[h, :, :WINDOW] = jnp.where(col > row, -ALIBI_SLOPES[h] * dist_p, -jnp.inf)
            bias_ref[h, :, WINDOW:] = jnp.where(row >= col, -ALIBI_SLOPES[h] * dist_c, -jnp.inf)

    has_prev = blk > 0
    kc = [kc_ref[:, j * SWA_HD:(j + 1) * SWA_HD].astype(BF16) for j in range(SWA_KV_HEADS)]
    kp = [kp_ref[:, j * SWA_HD:(j + 1) * SWA_HD].astype(BF16) for j in range(SWA_KV_HEADS)]
    vc = [vc_ref[:, j * SWA_HD:(j + 1) * SWA_HD].astype(BF16) for j in range(SWA_KV_HEADS)]
    vp = [vp_ref[:, j * SWA_HD:(j + 1) * SWA_HD].astype(BF16) for j in range(SWA_KV_HEADS)]
    s_c, s_p = [], []
    for h in range(SWA_HEADS):
        j = h // SWA_GROUP
        qh = (q_ref[:, h * SWA_HD:(h + 1) * SWA_HD] * SWA_SCALE).astype(BF16)
        s_c.append(lax.dot_general(qh, kc[j], NT_DIMS, preferred_element_type=F32))
        s_p.append(lax.dot_general(qh, kp[j], NT_DIMS, preferred_element_type=F32))
    p_c, p_p = [], []
    for h in range(SWA_HEADS):
        sc = s_c[h] + bias_ref[h, :, WINDOW:]
        sp = jnp.where(has_prev, s_p[h] + bias_ref[h, :, :WINDOW], -jnp.inf)
        sink = sinks_ref[layer, h]
        m = jnp.maximum(jnp.max(jnp.maximum(sc, sp), axis=-1, keepdims=True), sink)
        ec = jnp.exp(sc - m)
        ep = jnp.exp(sp - m)
        den = jnp.sum(ec + ep, axis=-1, keepdims=True) + jnp.exp(sink - m)
        inv = 1.0 / den
        p_c.append((ec * inv).astype(BF16))
        p_p.append((ep * inv).astype(BF16))
    outs = []
    for h in range(SWA_HEADS):
        j = h // SWA_GROUP
        outs.append(jnp.dot(p_c[h], vc[j], preferred_element_type=F32)
                    + jnp.dot(p_p[h], vp[j], preferred_element_type=F32))
    o_ref[...] = jnp.concatenate(outs, axis=1).astype(BF16)


def _swa_prompt(y, sinks, layer, batch, seq):
    nb = seq // WINDOW
    rows = lambda b, i: b * nb + i
    prev_rows = lambda b, i: b * nb + jnp.maximum(i - 1, 0)
    return pl.pallas_call(
        functools.partial(_swa_kernel, layer=layer),
        grid=(batch, nb),
        in_specs=[
            pl.BlockSpec(memory_space=pltpu.SMEM),
            pl.BlockSpec((WINDOW, SWA_Q), lambda b, i: (rows(b, i), COL_SQ // SWA_Q)),
            pl.BlockSpec((WINDOW, SWA_KV), lambda b, i: (rows(b, i), COL_SK // SWA_KV)),
            pl.BlockSpec((WINDOW, SWA_KV), lambda b, i: (prev_rows(b, i), COL_SK // SWA_KV)),
            pl.BlockSpec((WINDOW, SWA_KV), lambda b, i: (rows(b, i), COL_SV // SWA_KV)),
            pl.BlockSpec((WINDOW, SWA_KV), lambda b, i: (prev_rows(b, i), COL_SV // SWA_KV)),
        ],
        out_specs=pl.BlockSpec((WINDOW, SWA_Q), lambda b, i: (rows(b, i), 0)),
        out_shape=jax.ShapeDtypeStruct((batch * seq, SWA_Q), BF16),
        scratch_shapes=[pltpu.VMEM((SWA_HEADS, WINDOW, 2 * WINDOW), F32)],
        compiler_params=_params(("parallel", "arbitrary")),
        name="swa_prompt",
    )(sinks, y, y, y, y, y)


SAMPLE_STEP = SUBLANES


def _sample_kernel(sinks_ref, q_ref, k_ref, v_ref, gr_ref, glr_ref, cb_ref, cc_ref, ch_ref,
                   sq_ref, sk_ref, sv_ref, wlr_ref, blr_ref, gn_ref, cw_ref,
                   s_ref, cs_ref, kc_ref, vc_ref,
                   oa_ref, ob_ref, oc_ref, so_ref, cso_ref, kco_ref, vco_ref,
                   o_scr, *, layer):
    nb = SAMPLE_STEP
    z = jnp.dot(glr_ref[...], wlr_ref[...], precision=HIGHEST,
                preferred_element_type=F32) + blr_ref[...]
    b = _log_sigmoid(z) / GLA_TAU
    q = q_ref[...] * (GLA_DK ** -0.5)
    k = k_ref[...]
    v = v_ref[...]
    qt = q * jnp.exp(b)
    kt = k * jnp.exp(-b)
    kd = k * jnp.exp(b - b)
    g_last = jnp.exp(b)
    qt_t = qt.T
    kd_t = kd.T
    gl_t = g_last.T
    for h in range(GLA_HEADS):
        ks = slice(h * GLA_DK, (h + 1) * GLA_DK)
        vs = slice(h * GLA_DV, (h + 1) * GLA_DV)
        a = jnp.sum(qt[:, ks] * kt[:, ks], axis=-1, keepdims=True)
        o_intra = a * v[:, vs]
        for i in range(nb):
            s_in = s_ref[i, h]
            v_row = v[i:i + 1, vs]
            o = jnp.sum(qt_t[ks, i:i + 1] * s_in, axis=0, keepdims=True)
            o_scr[i:i + 1, vs] = o + o_intra[i:i + 1, :]
            so_ref[i, h] = gl_t[ks, i:i + 1] * s_in + kd_t[ks, i:i + 1] * v_row
    gn = gn_ref[...]
    for h in range(GLA_HEADS):
        vs = slice(h * GLA_DV, (h + 1) * GLA_DV)
        oa_ref[:, vs] = _rmsnorm_rows(o_scr[:, vs], gn) * _silu(gr_ref[:, vs])

    u_new = cc_ref[...] * ch_ref[...]
    w = cw_ref[...]
    p0 = cs_ref[:, :CONV_WIDTH]
    p1 = cs_ref[:, CONV_WIDTH:]
    conv = w[0:1, :] * p0 + w[1:2, :] * p1 + w[2:3, :] * u_new
    ob_ref[...] = cb_ref[...] * conv
    cso_ref[:, :CONV_WIDTH] = p1
    cso_ref[:, CONV_WIDTH:] = u_new

    rowk = lax.broadcasted_iota(jnp.int32, (WINDOW, SWA_KV), 0)
    key = lax.broadcasted_iota(jnp.int32, (SUBLANES, WINDOW), 1)
    grp = lax.broadcasted_iota(jnp.int32, (SUBLANES, WINDOW), 0)
    dist = (WINDOW - 1 - key).astype(F32)
    bias, sink = [], []
    for j in range(SWA_KV_HEADS):
        bj = jnp.zeros((SUBLANES, WINDOW), F32)
        sj = jnp.zeros((SUBLANES, 1), F32)
        for g in range(SWA_GROUP):
            bj = jnp.where(grp == g, -ALIBI_SLOPES[j * SWA_GROUP + g] * dist, bj)
            sj = jnp.where(grp[:, 0:1] == g, sinks_ref[layer, j * SWA_GROUP + g], sj)
        bias.append(bj)
        sink.append(sj)
    k_new, v_new = [], []
    for i in range(nb):
        kn = jnp.where(rowk == WINDOW - 1, sk_ref[i:i + 1, :], pltpu.roll(kc_ref[i], WINDOW - 1, 0))
        vn = jnp.where(rowk == WINDOW - 1, sv_ref[i:i + 1, :], pltpu.roll(vc_ref[i], WINDOW - 1, 0))
        kco_ref[i] = kn
        vco_ref[i] = vn
        k_new.append(kn.astype(BF16))
        v_new.append(vn.astype(BF16))
    scores = []
    for i in range(nb):
        for j in range(SWA_KV_HEADS):
            qj = (sq_ref[i, j] * SWA_SCALE).astype(BF16)
            kj = k_new[i][:, j * SWA_HD:(j + 1) * SWA_HD]
            scores.append(lax.dot_general(qj, kj, NT_DIMS, preferred_element_type=F32))
    probs = []
    for i in range(nb):
        for j in range(SWA_KV_HEADS):
            s = scores[i * SWA_KV_HEADS + j] + bias[j]
            m = jnp.maximum(jnp.max(s, axis=-1, keepdims=True), sink[j])
            e = jnp.exp(s - m)
            den = jnp.sum(e, axis=-1, keepdims=True) + jnp.exp(sink[j] - m)
            probs.append((e * (1.0 / den)).astype(BF16))
    for i in range(nb):
        for j in range(SWA_KV_HEADS):
            vj = v_new[i][:, j * SWA_HD:(j + 1) * SWA_HD]
            oc_ref[i, j] = jnp.dot(probs[i * SWA_KV_HEADS + j], vj, preferred_element_type=F32)


def _sample_mixer(y, glr, sinks, w_lr_p, b_lr, gla_norm, conv_w,
                  state_gla, state_conv, cache_k, cache_v, layer):
    nb = y.shape[0]
    st = SAMPLE_STEP
    sq = y[:, COL_SQ:COL_SQ + SWA_Q].reshape(nb, SWA_KV_HEADS, SWA_GROUP, SWA_HD)
    sq = jnp.pad(sq, ((0, 0), (0, 0), (0, SUBLANES - SWA_GROUP), (0, 0)))

    def seg(width, col):
        return pl.BlockSpec((st, width), lambda b: (b, col // width))

    def per_layer(shape):
        return pl.BlockSpec((None,) + shape, lambda b: (layer,) + (0,) * len(shape))

    def state(shape):
        return pl.BlockSpec((None, st) + shape, lambda b: (layer, b) + (0,) * len(shape))

    def out(shape):
        return pl.BlockSpec((st,) + shape, lambda b: (b,) + (0,) * len(shape))

    res = pl.pallas_call(
        functools.partial(_sample_kernel, layer=layer),
        grid=(nb // st,),
        in_specs=[
            pl.BlockSpec(memory_space=pltpu.SMEM),
            seg(GLA_QK, COL_Q), seg(GLA_QK, COL_K), seg(GLA_V, COL_V), seg(GLA_V, COL_GR),
            pl.BlockSpec((st, LANES), lambda b: (b, 0)),
            seg(CONV_WIDTH, COL_CB), seg(CONV_WIDTH, COL_CC), seg(CONV_WIDTH, COL_CH),
            out((SWA_KV_HEADS, SUBLANES, SWA_HD)), seg(SWA_KV, COL_SK), seg(SWA_KV, COL_SV),
            per_layer((LANES, GLA_QK)), per_layer((1, GLA_QK)), per_layer((1, GLA_DV)),
            per_layer((CONV_K, CONV_WIDTH)),
            state((GLA_HEADS, GLA_DK, GLA_DV)), state(((CONV_K - 1) * CONV_WIDTH,)),
            state((WINDOW, SWA_KV)), state((WINDOW, SWA_KV)),
        ],
        out_specs=[
            out((GLA_V,)), out((CONV_WIDTH,)), out((SWA_KV_HEADS, SUBLANES, SWA_HD)),
            out((GLA_HEADS, GLA_DK, GLA_DV)), out(((CONV_K - 1) * CONV_WIDTH,)),
            out((WINDOW, SWA_KV)), out((WINDOW, SWA_KV)),
        ],
        out_shape=[
            jax.ShapeDtypeStruct((nb, GLA_V), F32),
            jax.ShapeDtypeStruct((nb, CONV_WIDTH), F32),
            jax.ShapeDtypeStruct((nb, SWA_KV_HEADS, SUBLANES, SWA_HD), F32),
            jax.ShapeDtypeStruct((nb, GLA_HEADS, GLA_DK, GLA_DV), F32),
            jax.ShapeDtypeStruct((nb, (CONV_K - 1) * CONV_WIDTH), F32),
            jax.ShapeDtypeStruct((nb, WINDOW, SWA_KV), F32),
            jax.ShapeDtypeStruct((nb, WINDOW, SWA_KV), F32),
        ],
        scratch_shapes=[pltpu.VMEM((st, GLA_V), F32)],
        compiler_params=_params(("parallel",)),
        name="sample_mixer",
    )(sinks, y, y, y, y, glr, y, y, y, sq, y, y,
      w_lr_p, b_lr, gla_norm, conv_w, state_gla, state_conv, cache_k, cache_v)
    oa, ob, oc, sg, sc, kc, vc = res
    oc = oc[:, :, :SWA_GROUP, :].reshape(nb, SWA_Q)
    sc = sc.reshape(nb, CONV_K - 1, CONV_WIDTH)
    return oa, ob, oc, sg, sc, kc, vc


def _merge_kernel(a_ref, b_ref, c_ref, g0_ref, g1_ref, g2_ref, wb_ref, wo_ref, x_ref, o_ref):
    @pl.when(pl.program_id(1) == 0)
    def _():
        o_ref[...] = x_ref[...]

    def branch(src_ref, gate_ref, n):
        br = jnp.dot(src_ref[...].astype(BF16), wb_ref[n], preferred_element_type=F32)
        return jax.nn.sigmoid(gate_ref[...]) * br

    mixed = branch(a_ref, g0_ref, 0) + branch(b_ref, g1_ref, 1) + branch(c_ref, g2_ref, 2)
    o_ref[...] += jnp.dot(mixed.astype(BF16), wo_ref[...], preferred_element_type=F32)


def _merge(out_a, out_b, out_c, y, x, w_branch, w_out, layer, tm, tn):
    m = x.shape[0]
    gate_col = lambda n: (COL_GATES + n * D_MODEL) // tn
    src = pl.BlockSpec((tm, BRANCH_WIDTH), lambda i, t: (i, 0), pipeline_mode=pl.Buffered(1))
    gate = lambda n: pl.BlockSpec((tm, tn), lambda i, t: (i, gate_col(n) + t))
    return pl.pallas_call(
        _merge_kernel,
        grid=(m // tm, D_MODEL // tn),
        in_specs=[
            src, src, src, gate(0), gate(1), gate(2),
            pl.BlockSpec((None, N_BRANCH, BRANCH_WIDTH, tn), lambda i, t: (layer, 0, 0, t)),
            pl.BlockSpec((None, tn, D_MODEL), lambda i, t: (layer, t, 0)),
            pl.BlockSpec((tm, D_MODEL), lambda i, t: (i, 0), pipeline_mode=pl.Buffered(1)),
        ],
        out_specs=pl.BlockSpec((tm, D_MODEL), lambda i, t: (i, 0)),
        out_shape=jax.ShapeDtypeStruct((m, D_MODEL), F32),
        compiler_params=_params(("parallel", "arbitrary")),
        name="merge_outproj",
    )(out_a, out_b, out_c, y, y, y, w_branch, w_out, x)


def _mlp_kernel(x_ref, g_ref, wu_ref, wd_ref, gnext_ref, wlr_ref, *rest, last):
    if last:
        o_ref, h_ref = rest
    else:
        o_ref, hn_ref, glr_ref, h_ref = rest
    f = pl.program_id(1)

    @pl.when(f == 0)
    def _():
        x = x_ref[...]
        h_ref[...] = _rmsnorm_rows(x, g_ref[...]).astype(BF16)
        o_ref[...] = x

    up = jnp.dot(h_ref[...], wu_ref[...], preferred_element_type=F32)
    act = jnp.square(jnp.maximum(up, 0.0)).astype(BF16)
    o_ref[...] += jnp.dot(act, wd_ref[...], preferred_element_type=F32)

    @pl.when(f == pl.num_programs(1) - 1)
    def _():
        xn = _rmsnorm_rows(o_ref[...], gnext_ref[...])
        if last:
            o_ref[...] = xn
        else:
            hn = xn.astype(BF16)
            hn_ref[...] = hn
            glr_ref[...] = jnp.dot(hn, wlr_ref[...].astype(BF16), preferred_element_type=F32)


def _mlp(x, norm_w, w_up, w_down, next_norm_w, w_in, layer, tm, tf):
    m = x.shape[0]
    last = layer == DEPTH - 1
    row_tile = pl.BlockSpec((tm, D_MODEL), lambda i, f: (i, 0))
    if last:
        next_norm_spec = pl.BlockSpec((1, D_MODEL), lambda i, f: (0, 0))
        lr_layer = layer
        out_specs = row_tile
        out_shape = jax.ShapeDtypeStruct((m, D_MODEL), F32)
    else:
        next_norm_spec = pl.BlockSpec((None, 1, D_MODEL), lambda i, f: (layer + 1, 0, 0))
        lr_layer = layer + 1
        out_specs = [row_tile, row_tile, pl.BlockSpec((tm, LANES), lambda i, f: (i, 0))]
        out_shape = [jax.ShapeDtypeStruct((m, D_MODEL), F32),
                     jax.ShapeDtypeStruct((m, D_MODEL), BF16),
                     jax.ShapeDtypeStruct((m, LANES), F32)]
    return pl.pallas_call(
        functools.partial(_mlp_kernel, last=last),
        grid=(m // tm, D_FF // tf),
        in_specs=[
            pl.BlockSpec((tm, D_MODEL), lambda i, f: (i, 0), pipeline_mode=pl.Buffered(1)),
            pl.BlockSpec((None, 1, D_MODEL), lambda i, f: (layer, 0, 0)),
            pl.BlockSpec((None, D_MODEL, tf), lambda i, f: (layer, 0, f)),
            pl.BlockSpec((None, tf, D_MODEL), lambda i, f: (layer, f, 0)),
            next_norm_spec,
            _lr_weight_spec(lr_layer),
        ],
        out_specs=out_specs,
        out_shape=out_shape,
        scratch_shapes=[pltpu.VMEM((tm, D_MODEL), BF16)],
        compiler_params=_params(("parallel", "arbitrary")),
        name="mlp",
    )(x, norm_w, w_up, w_down, next_norm_w, w_in)


def kernel(x_prompt, x_sample, state_gla, state_conv, cache_k, cache_v, w_in, w_lr, b_lr,
           gla_norm, conv_w, attn_sinks, w_branch, w_out, norm_mix, norm_mlp, w_up, w_down,
           norm_final):
    batch, seq, _ = x_prompt.shape
    nb = x_sample.shape[0]
    mp = batch * seq

    w_lr_p = jnp.pad(w_lr, ((0, 0), (0, LANES - GLA_RANK), (0, 0)))
    w_branch_b = w_branch.astype(BF16)
    w_out_b = w_out.astype(BF16)
    w_up_b = w_up.astype(BF16)
    w_down_b = w_down.astype(BF16)
    b_lr3 = b_lr.reshape(DEPTH, 1, GLA_QK)
    gla_norm3 = gla_norm.reshape(DEPTH, 1, GLA_DV)
    norm_mix3 = norm_mix.reshape(DEPTH, 1, D_MODEL)
    norm_mlp3 = norm_mlp.reshape(DEPTH, 1, D_MODEL)
    norm_final2 = norm_final.reshape(1, D_MODEL)
    state_conv2 = state_conv.reshape(DEPTH, nb, (CONV_K - 1) * CONV_WIDTH)
    cache_k4 = cache_k.reshape(DEPTH, nb, WINDOW, SWA_KV)
    cache_v4 = cache_v.reshape(DEPTH, nb, WINDOW, SWA_KV)

    xp = x_prompt.reshape(mp, D_MODEL)
    xs = x_sample.reshape(nb, D_MODEL)
    hp, glrp = _norm(xp, norm_mix3, w_in, 0, tm=min(1024, mp))
    hs, glrs = _norm(xs, norm_mix3, w_in, 0, tm=nb)
    outs = [[] for _ in range(8)]
    for l in range(DEPTH):
        next_norm = norm_final2 if l == DEPTH - 1 else norm_mix3
        yp = _inproj(hp, w_in, l, tm=min(1024, mp), tn=768)
        oa, sg_p = _gla_prompt(yp, glrp, w_lr_p, b_lr3, gla_norm3, l, batch, seq)
        ob, sc_p = _conv_prompt(yp, conv_w, l, batch, seq, tl=512)
        oc = _swa_prompt(yp, attn_sinks, l, batch, seq)
        yp3 = yp.reshape(batch, seq, PACKED_COLS)
        kp = yp3[:, seq - WINDOW:, COL_SK:COL_SK + SWA_KV]
        vp = yp3[:, seq - WINDOW:, COL_SV:COL_SV + SWA_KV]
        xp = _merge(oa, ob, oc, yp, xp, w_branch_b, w_out_b, l, tm=min(1024, mp), tn=512)
        res_p = _mlp(xp, norm_mlp3, w_up_b, w_down_b, next_norm, w_in, l, tm=min(1024, mp), tf=512)
        ys = _inproj(hs, w_in, l, tm=nb, tn=768)
        sa, sb, sc, sg_s, sc_s, ks, vs = _sample_mixer(
            ys, glrs, attn_sinks, w_lr_p, b_lr3, gla_norm3, conv_w,
            state_gla, state_conv2, cache_k4, cache_v4, l)
        xs = _merge(sa, sb, sc, ys, xs, w_branch_b, w_out_b, l, tm=nb, tn=512)
        res_s = _mlp(xs, norm_mlp3, w_up_b, w_down_b, next_norm, w_in, l, tm=nb, tf=512)
        if l == DEPTH - 1:
            xp, xs = res_p, res_s
        else:
            (xp, hp, glrp), (xs, hs, glrs) = res_p, res_s
        for lst, val in zip(outs, (
                sg_p, sg_s, sc_p, sc_s,
                kp.reshape(batch, WINDOW, SWA_KV_HEADS, SWA_HD),
                ks.reshape(nb, WINDOW, SWA_KV_HEADS, SWA_HD),
                vp.reshape(batch, WINDOW, SWA_KV_HEADS, SWA_HD),
                vs.reshape(nb, WINDOW, SWA_KV_HEADS, SWA_HD))):
            lst.append(val)
    y_prompt = xp.reshape(batch, seq, D_MODEL)
    y_sample = xs.reshape(nb, 1, D_MODEL)
    return (y_prompt, y_sample) + tuple(jnp.stack(o) for o in outs)
```

```python
import functools

import jax
import jax.numpy as jnp
from jax import lax
from jax.experimental import pallas as pl
from jax.experimental.pallas import tpu as pltpu

F32 = jnp.float32
BF16 = jnp.bfloat16
HIGHEST = lax.Precision.HIGHEST

D_MODEL = 2048
DEPTH = 4
PAST_LEN = 16384
BRANCH_WIDTH = D_MODEL // 2
N_BRANCH = 3
GLA_HEADS = 4
GLA_DV = BRANCH_WIDTH // GLA_HEADS
GLA_DK = GLA_DV // 2
GLA_RANK = 16
GLA_TAU = 16.0
GLA_CHUNK = 64
CONV_WIDTH = BRANCH_WIDTH
CONV_K = 3
SWA_HD = 64
SWA_HEADS = BRANCH_WIDTH // SWA_HD
SWA_KV_HEADS = SWA_HEADS // 4
SWA_GROUP = SWA_HEADS // SWA_KV_HEADS
WINDOW = 128
D_FF = 4 * D_MODEL
EPS = 1e-6

LANES = 128
SUBLANES = 8
VMEM_LIMIT = 56 * 1024 * 1024

GLA_QK = GLA_HEADS * GLA_DK
GLA_V = GLA_HEADS * GLA_DV
SWA_Q = SWA_HEADS * SWA_HD
SWA_KV = SWA_KV_HEADS * SWA_HD

COL_Q = 0
COL_K = COL_Q + GLA_QK
COL_V = COL_K + GLA_QK
COL_GR = COL_V + GLA_V
COL_CB = COL_GR + GLA_V
COL_CC = COL_CB + CONV_WIDTH
COL_CH = COL_CC + CONV_WIDTH
COL_SQ = COL_CH + CONV_WIDTH
COL_SK = COL_SQ + SWA_Q
COL_SV = COL_SK + SWA_KV
COL_GATES = COL_SV + SWA_KV
PACKED_COLS = COL_GATES + N_BRANCH * D_MODEL
LR_SRC = COL_CB

ALIBI_SLOPES = tuple(2.0 ** (-8.0 * (h + 1) / SWA_HEADS) for h in range(SWA_HEADS))
SWA_SCALE = SWA_HD ** -0.5

NT_DIMS = (((1,), (1,)), ((), ()))
TN_DIMS = (((0,), (0,)), ((), ()))


def _params(sem):
    return pltpu.CompilerParams(dimension_semantics=sem, vmem_limit_bytes=VMEM_LIMIT)


def _rmsnorm_rows(x, g):
    ms = jnp.mean(x * x, axis=-1, keepdims=True)
    return x * lax.rsqrt(ms + EPS) * g


def _log_sigmoid(z):
    return jnp.minimum(z, 0.0) - jnp.log1p(jnp.exp(-jnp.abs(z)))


def _silu(x):
    return x * jax.nn.sigmoid(x)


def _lr_weight_spec(layer):
    return pl.BlockSpec((None, LANES, D_MODEL), lambda *a: (layer, LR_SRC // LANES, 0))


def _lr_project(h, wlr_ref):
    return lax.dot_general(h, wlr_ref[...].astype(BF16), NT_DIMS, preferred_element_type=F32)


def _norm_kernel(x_ref, g_ref, wlr_ref, h_ref, glr_ref):
    h = _rmsnorm_rows(x_ref[...], g_ref[...]).astype(BF16)
    h_ref[...] = h
    glr_ref[...] = _lr_project(h, wlr_ref)


def _norm(x, norm_w, w_in_t, layer, tm):
    m = x.shape[0]
    return pl.pallas_call(
        _norm_kernel,
        grid=(m // tm,),
        in_specs=[
            pl.BlockSpec((tm, D_MODEL), lambda i: (i, 0)),
            pl.BlockSpec((None, 1, D_MODEL), lambda i: (layer, 0, 0)),
            _lr_weight_spec(layer),
        ],
        out_specs=[
            pl.BlockSpec((tm, D_MODEL), lambda i: (i, 0)),
            pl.BlockSpec((tm, LANES), lambda i: (i, 0)),
        ],
        out_shape=[
            jax.ShapeDtypeStruct((m, D_MODEL), BF16),
            jax.ShapeDtypeStruct((m, LANES), F32),
        ],
        compiler_params=_params(("parallel",)),
        name="norm",
    )(x, norm_w, w_in_t)


def _inproj_kernel(hp_ref, hs_ref, w_ref, yp_ref, gp_ref, ys_ref, gs_ref, wbf_ref, *,
                   n_main, n_ptiles):
    j = pl.program_id(0)
    i = pl.program_id(1)

    @pl.when(i == 0)
    def _():
        wbf_ref[...] = w_ref[...].astype(BF16)

    def project(h_ref):
        return lax.dot_general(h_ref[...], wbf_ref[...], NT_DIMS, preferred_element_type=F32)

    is_main = j < n_main
    is_prompt = i < n_ptiles

    @pl.when(jnp.logical_and(is_main, is_prompt))
    def _():
        yp_ref[...] = project(hp_ref)

    @pl.when(jnp.logical_and(jnp.logical_not(is_main), is_prompt))
    def _():
        gp_ref[...] = project(hp_ref).astype(BF16)

    @pl.when(jnp.logical_and(is_main, jnp.logical_not(is_prompt)))
    def _():
        ys_ref[...] = project(hs_ref)

    @pl.when(jnp.logical_and(jnp.logical_not(is_main), jnp.logical_not(is_prompt)))
    def _():
        gs_ref[...] = project(hs_ref).astype(BF16)


def _inproj(hp, hs, w_in_t, layer, tm, tn):
    mp, ms = hp.shape[0], hs.shape[0]
    n_ptiles = mp // tm
    n_main = COL_GATES // tn
    n_gate = (N_BRANCH * D_MODEL) // tn
    n_plain = LR_SRC // tn
    last_p = n_ptiles - 1
    prow = lambda i: jnp.minimum(i, last_p)
    w_row = lambda j: pl.multiple_of(j * tn + jnp.where(j >= n_plain, GLA_RANK, 0), GLA_RANK)
    return pl.pallas_call(
        functools.partial(_inproj_kernel, n_main=n_main, n_ptiles=n_ptiles),
        grid=(n_main + n_gate, n_ptiles + 1),
        in_specs=[
            pl.BlockSpec((tm, D_MODEL), lambda j, i: (prow(i), 0)),
            pl.BlockSpec((ms, D_MODEL), lambda j, i: (0, 0)),
            pl.BlockSpec((None, pl.Element(tn), pl.Element(D_MODEL)),
                         lambda j, i: (layer, w_row(j), 0)),
        ],
        out_specs=[
            pl.BlockSpec((tm, tn), lambda j, i: (jnp.where(j < n_main, prow(i), last_p),
                                                 jnp.minimum(j, n_main - 1))),
            pl.BlockSpec((tm, tn), lambda j, i: (jnp.where(j < n_main, 0, prow(i)),
                                                 jnp.maximum(j - n_main, 0))),
            pl.BlockSpec((ms, tn), lambda j, i: (0, jnp.minimum(j, n_main - 1))),
            pl.BlockSpec((ms, tn), lambda j, i: (0, jnp.maximum(j - n_main, 0))),
        ],
        out_shape=[
            jax.ShapeDtypeStruct((mp, COL_GATES), F32),
            jax.ShapeDtypeStruct((mp, N_BRANCH * D_MODEL), BF16),
            jax.ShapeDtypeStruct((ms, COL_GATES), F32),
            jax.ShapeDtypeStruct((ms, N_BRANCH * D_MODEL), BF16),
        ],
        scratch_shapes=[pltpu.VMEM((tn, D_MODEL), BF16)],
        compiler_params=_params(("arbitrary", "arbitrary")),
        name="inproj",
    )(hp, hs, w_in_t)


GLA_STEP_CHUNKS = 4


def _gla_kernel(q_ref, k_ref, v_ref, gr_ref, glr_ref, wlr_ref, blr_ref, gn_ref,
                oa_ref, sfin_ref, st_ref):
    c = pl.program_id(1)
    cs = GLA_CHUNK

    @pl.when(c == 0)
    def _():
        st_ref[...] = jnp.zeros_like(st_ref)

    z = jnp.dot(glr_ref[...], wlr_ref[...], precision=HIGHEST,
                preferred_element_type=F32) + blr_ref[...]
    log_a = _log_sigmoid(z) / GLA_TAU
    row = lax.broadcasted_iota(jnp.int32, (cs, cs), 0)
    col = lax.broadcasted_iota(jnp.int32, (cs, cs), 1)
    causal = row >= col
    tri = jnp.where(causal, 1.0, 0.0).astype(F32)
    gn = gn_ref[...]

    qt, kt, kd, vv, g_last = [], [], [], [], []
    for n in range(GLA_STEP_CHUNKS):
        rs = slice(n * cs, (n + 1) * cs)
        b = jnp.dot(tri, log_a[rs, :], precision=HIGHEST, preferred_element_type=F32)
        b_last = b[cs - 1:cs, :]
        q = q_ref[rs, :] * (GLA_DK ** -0.5)
        k = k_ref[rs, :]
        qt.append((q * jnp.exp(b)).astype(BF16))
        kt.append((k * jnp.exp(-b)).astype(BF16))
        kd.append((k * jnp.exp(b_last - b)).astype(BF16))
        g_last.append(jnp.exp(b_last))
        vv.append(v_ref[rs, :].astype(BF16))

    for h in range(GLA_HEADS):
        ks = slice(h * GLA_DK, (h + 1) * GLA_DK)
        vs = slice(h * GLA_DV, (h + 1) * GLA_DV)
        o_intra, u_t = [], []
        for n in range(GLA_STEP_CHUNKS):
            a = lax.dot_general(qt[n][:, ks], kt[n][:, ks], NT_DIMS, preferred_element_type=F32)
            a = jnp.where(causal, a, 0.0).astype(BF16)
            o_intra.append(jnp.dot(a, vv[n][:, vs], preferred_element_type=F32))
            u_t.append(lax.dot_general(vv[n][:, vs], kd[n][:, ks], TN_DIMS,
                                       preferred_element_type=F32))
        s_t = st_ref[h]
        for n in range(GLA_STEP_CHUNKS):
            rs = slice(n * cs, (n + 1) * cs)
            o = lax.dot_general(qt[n][:, ks], s_t.astype(BF16), NT_DIMS,
                                preferred_element_type=F32) + o_intra[n]
            s_t = g_last[n][:, ks] * s_t + u_t[n]
            oa_ref[rs, vs] = (_rmsnorm_rows(o, gn) * _silu(gr_ref[rs, vs])).astype(BF16)
        st_ref[h] = s_t

    @pl.when(c == pl.num_programs(1) - 1)
    def _():
        for h in range(GLA_HEADS):
            sfin_ref[0, h] = st_ref[h].T


def _gla_prompt(y, glr, w_lr_p, b_lr, gla_norm, layer, batch, seq):
    tr = GLA_STEP_CHUNKS * GLA_CHUNK
    ns = seq // tr
    rows = lambda b, c: b * ns + c
    return pl.pallas_call(
        _gla_kernel,
        grid=(batch, ns),
        in_specs=[
            pl.BlockSpec((tr, GLA_QK), lambda b, c: (rows(b, c), COL_Q // GLA_QK)),
            pl.BlockSpec((tr, GLA_QK), lambda b, c: (rows(b, c), COL_K // GLA_QK)),
            pl.BlockSpec((tr, GLA_V), lambda b, c: (rows(b, c), COL_V // GLA_V)),
            pl.BlockSpec((tr, GLA_V), lambda b, c: (rows(b, c), COL_GR // GLA_V)),
            pl.BlockSpec((tr, LANES), lambda b, c: (rows(b, c), 0)),
            pl.BlockSpec((None, LANES, GLA_QK), lambda b, c: (layer, 0, 0)),
            pl.BlockSpec((None, 1, GLA_QK), lambda b, c: (layer, 0, 0)),
            pl.BlockSpec((None, 1, GLA_DV), lambda b, c: (layer, 0, 0)),
        ],
        out_specs=[
            pl.BlockSpec((tr, GLA_V), lambda b, c: (rows(b, c), 0)),
            pl.BlockSpec((1, GLA_HEADS, GLA_DK, GLA_DV), lambda b, c: (b, 0, 0, 0)),
        ],
        out_shape=[
            jax.ShapeDtypeStruct((batch * seq, GLA_V), BF16),
            jax.ShapeDtypeStruct((batch, GLA_HEADS, GLA_DK, GLA_DV), F32),
        ],
        scratch_shapes=[pltpu.VMEM((GLA_HEADS, GLA_DV, GLA_DK), F32)],
        compiler_params=_params(("parallel", "arbitrary")),
        name="gla_prompt",
    )(y, y, y, y, glr, w_lr_p, b_lr, gla_norm)


def _conv_kernel(cb_ref, cc_ref, ch_ref, w_ref, ob_ref, cs_ref, prev_ref):
    @pl.when(pl.program_id(1) == 0)
    def _():
        prev_ref[...] = jnp.zeros_like(prev_ref)

    u = cc_ref[...] * ch_ref[...]
    tl = u.shape[0]
    prev = prev_ref[...]
    p_m1 = prev[SUBLANES - 1:SUBLANES, :]
    p_m2 = prev[SUBLANES - 2:SUBLANES - 1, :]
    row = lax.broadcasted_iota(jnp.int32, u.shape, 0)
    u1 = jnp.where(row == 0, p_m1, pltpu.roll(u, 1, 0))
    u2 = jnp.where(row == 0, p_m2, jnp.where(row == 1, p_m1, pltpu.roll(u, 2, 0)))
    w = w_ref[...]
    conv = w[0:1, :] * u2 + w[1:2, :] * u1 + w[2:3, :] * u
    ob_ref[...] = (cb_ref[...] * conv).astype(BF16)
    prev_ref[...] = u[tl - SUBLANES:tl, :]
    cs_ref[0] = u[tl - (CONV_K - 1):tl, :]


def _conv_prompt(y, conv_w, layer, batch, seq, tl):
    nt = seq // tl
    rows = lambda b, t: b * nt + t
    return pl.pallas_call(
        _conv_kernel,
        grid=(batch, nt),
        in_specs=[
            pl.BlockSpec((tl, CONV_WIDTH), lambda b, t: (rows(b, t), COL_CB // CONV_WIDTH)),
            pl.BlockSpec((tl, CONV_WIDTH), lambda b, t: (rows(b, t), COL_CC // CONV_WIDTH)),
            pl.BlockSpec((tl, CONV_WIDTH), lambda b, t: (rows(b, t), COL_CH // CONV_WIDTH)),
            pl.BlockSpec((None, CONV_K, CONV_WIDTH), lambda b, t: (layer, 0, 0)),
        ],
        out_specs=[
            pl.BlockSpec((tl, CONV_WIDTH), lambda b, t: (rows(b, t), 0)),
            pl.BlockSpec((1, CONV_K - 1, CONV_WIDTH), lambda b, t: (b, 0, 0)),
        ],
        out_shape=[
            jax.ShapeDtypeStruct((batch * seq, CONV_WIDTH), BF16),
            jax.ShapeDtypeStruct((batch, CONV_K - 1, CONV_WIDTH), F32),
        ],
        scratch_shapes=[pltpu.VMEM((SUBLANES, CONV_WIDTH), F32)],
        compiler_params=_params(("parallel", "arbitrary")),
        name="conv_prompt",
    )(y, y, y, conv_w)


def _swa_kernel(sinks_ref, q_ref, kc_ref, kp_ref, vc_ref, vp_ref, o_ref, bias_ref, *, layer):
    blk = pl.program_id(1)

    @pl.when(blk == 0)
    def _():
        row = lax.broadcasted_iota(jnp.int32, (WINDOW, WINDOW), 0)
        col = lax.broadcasted_iota(jnp.int32, (WINDOW, WINDOW), 1)
        dist_c = (row - col).astype(F32)
        dist_p = dist_c + float(WINDOW)
        for h in range(SWA_HEADS):
            bias_ref[h, :, :WINDOW] = jnp.where(col > row, -ALIBI_SLOPES[h] * dist_p, -jnp.inf)
            bias_ref[h, :, WINDOW:] = jnp.where(row >= col, -ALIBI_SLOPES[h] * dist_c, -jnp.inf)

    has_prev = blk > 0
    kc = [kc_ref[:, j * SWA_HD:(j + 1) * SWA_HD].astype(BF16) for j in range(SWA_KV_HEADS)]
    kp = [kp_ref[:, j * SWA_HD:(j + 1) * SWA_HD].astype(BF16) for j in range(SWA_KV_HEADS)]
    vc = [vc_ref[:, j * SWA_HD:(j + 1) * SWA_HD].astype(BF16) for j in range(SWA_KV_HEADS)]
    vp = [vp_ref[:, j * SWA_HD:(j + 1) * SWA_HD].astype(BF16) for j in range(SWA_KV_HEADS)]
    s_c, s_p = [], []
    for h in range(SWA_HEADS):
        j = h // SWA_GROUP
        qh = (q_ref[:, h * SWA_HD:(h + 1) * SWA_HD] * SWA_SCALE).astype(BF16)
        s_c.append(lax.dot_general(qh, kc[j], NT_DIMS, preferred_element_type=F32))
        s_p.append(lax.dot_general(qh, kp[j], NT_DIMS, preferred_element_type=F32))
    p_c, p_p = [], []
    for h in range(SWA_HEADS):
        sc = s_c[h] + bias_ref[h, :, WINDOW:]
        sp = jnp.where(has_prev, s_p[h] + bias_ref[h, :, :WINDOW], -jnp.inf)
        sink = sinks_ref[layer, h]
        m = jnp.maximum(jnp.max(jnp.maximum(sc, sp), axis=-1, keepdims=True), sink)
        ec = jnp.exp(sc - m)
        ep = jnp.exp(sp - m)
        den = jnp.sum(ec + ep, axis=-1, keepdims=True) + jnp.exp(sink - m)
        inv = 1.0 / den
        p_c.append((ec * inv).astype(BF16))
        p_p.append((ep * inv).astype(BF16))
    outs = []
    for h in range(SWA_HEADS):
        j = h // SWA_GROUP
        outs.append(jnp.dot(p_c[h], vc[j], preferred_element_type=F32)
                    + jnp.dot(p_p[h], vp[j], preferred_element_type=F32))
    o_ref[...] = jnp.concatenate(outs, axis=1).astype(BF16)


def _swa_prompt(y, sinks, layer, batch, seq):
    nb = seq // WINDOW
    rows = lambda b, i: b * nb + i
    prev_rows = lambda b, i: b * nb + jnp.maximum(i - 1, 0)
    return pl.pallas_call(
        functools.partial(_swa_kernel, layer=layer),
        grid=(batch, nb),
        in_specs=[
            pl.BlockSpec(memory_space=pltpu.SMEM),
            pl.BlockSpec((WINDOW, SWA_Q), lambda b, i: (rows(b, i), COL_SQ // SWA_Q)),
            pl.BlockSpec((WINDOW, SWA_KV), lambda b, i: (rows(b, i), COL_SK // SWA_KV)),
            pl.BlockSpec((WINDOW, SWA_KV), lambda b, i: (prev_rows(b, i), COL_SK // SWA_KV)),
            pl.BlockSpec((WINDOW, SWA_KV), lambda b, i: (rows(b, i), COL_SV // SWA_KV)),
            pl.BlockSpec((WINDOW, SWA_KV), lambda b, i: (prev_rows(b, i), COL_SV // SWA_KV)),
        ],
        out_specs=pl.BlockSpec((WINDOW, SWA_Q), lambda b, i: (rows(b, i), 0)),
        out_shape=jax.ShapeDtypeStruct((batch * seq, SWA_Q), BF16),
        scratch_shapes=[pltpu.VMEM((SWA_HEADS, WINDOW, 2 * WINDOW), F32)],
        compiler_params=_params(("parallel", "arbitrary")),
        name="swa_prompt",
    )(sinks, y, y, y, y, y)


SAMPLE_STEP = SUBLANES


def _sample_kernel(sinks_ref, q_ref, k_ref, v_ref, gr_ref, glr_ref, cb_ref, cc_ref, ch_ref,
                   sq_ref, sk_ref, sv_ref, wlr_ref, blr_ref, gn_ref, cw_ref,
                   s_ref, cs_ref, kc_ref, vc_ref,
                   oa_ref, ob_ref, oc_ref, so_ref, cso_ref, kco_ref, vco_ref,
                   o_scr, *, layer):
    nb = SAMPLE_STEP
    z = jnp.dot(glr_ref[...], wlr_ref[...], precision=HIGHEST,
                preferred_element_type=F32) + blr_ref[...]
    b = _log_sigmoid(z) / GLA_TAU
    q = q_ref[...] * (GLA_DK ** -0.5)
    k = k_ref[...]
    v = v_ref[...]
    qt = q * jnp.exp(b)
    kt = k * jnp.exp(-b)
    kd = k * jnp.exp(b - b)
    g_last = jnp.exp(b)
    qt_t = qt.T
    kd_t = kd.T
    gl_t = g_last.T
    for h in range(GLA_HEADS):
        ks = slice(h * GLA_DK, (h + 1) * GLA_DK)
        vs = slice(h * GLA_DV, (h + 1) * GLA_DV)
        a = jnp.sum(qt[:, ks] * kt[:, ks], axis=-1, keepdims=True)
        o_intra = a * v[:, vs]
        for i in range(nb):
            s_in = s_ref[i, h]
            v_row = v[i:i + 1, vs]
            o = jnp.sum(qt_t[ks, i:i + 1] * s_in, axis=0, keepdims=True)
            o_scr[i:i + 1, vs] = o + o_intra[i:i + 1, :]
            so_ref[i, h] = gl_t[ks, i:i + 1] * s_in + kd_t[ks, i:i + 1] * v_row
    gn = gn_ref[...]
    for h in range(GLA_HEADS):
        vs = slice(h * GLA_DV, (h + 1) * GLA_DV)
        oa_ref[:, vs] = _rmsnorm_rows(o_scr[:, vs], gn) * _silu(gr_ref[:, vs])

    u_new = cc_ref[...] * ch_ref[...]
    w = cw_ref[...]
    p0 = cs_ref[:, :CONV_WIDTH]
    p1 = cs_ref[:, CONV_WIDTH:]
    conv = w[0:1, :] * p0 + w[1:2, :] * p1 + w[2:3, :] * u_new
    ob_ref[...] = cb_ref[...] * conv
    cso_ref[:, :CONV_WIDTH] = p1
    cso_ref[:, CONV_WIDTH:] = u_new

    rowk = lax.broadcasted_iota(jnp.int32, (WINDOW, SWA_KV), 0)
    key = lax.broadcasted_iota(jnp.int32, (SUBLANES, WINDOW), 1)
    grp = lax.broadcasted_iota(jnp.int32, (SUBLANES, WINDOW), 0)
    dist = (WINDOW - 1 - key).astype(F32)
    bias, sink = [], []
    for j in range(SWA_KV_HEADS):
        bj = jnp.zeros((SUBLANES, WINDOW), F32)
        sj = jnp.zeros((SUBLANES, 1), F32)
        for g in range(SWA_GROUP):
            bj = jnp.where(grp == g, -ALIBI_SLOPES[j * SWA_GROUP + g] * dist, bj)
            sj = jnp.where(grp[:, 0:1] == g, sinks_ref[layer, j * SWA_GROUP + g], sj)
        bias.append(bj)
        sink.append(sj)
    k_new, v_new = [], []
    for i in range(nb):
        kn = jnp.where(rowk == WINDOW - 1, sk_ref[i:i + 1, :], pltpu.roll(kc_ref[i], WINDOW - 1, 0))
        vn = jnp.where(rowk == WINDOW - 1, sv_ref[i:i + 1, :], pltpu.roll(vc_ref[i], WINDOW - 1, 0))
        kco_ref[i] = kn
        vco_ref[i] = vn
        k_new.append(kn.astype(BF16))
        v_new.append(vn.astype(BF16))
    scores = []
    for i in range(nb):
        for j in range(SWA_KV_HEADS):
            qj = (sq_ref[i, j] * SWA_SCALE).astype(BF16)
            kj = k_new[i][:, j * SWA_HD:(j + 1) * SWA_HD]
            scores.append(lax.dot_general(qj, kj, NT_DIMS, preferred_element_type=F32))
    probs = []
    for i in range(nb):
        for j in range(SWA_KV_HEADS):
            s = scores[i * SWA_KV_HEADS + j] + bias[j]
            m = jnp.maximum(jnp.max(s, axis=-1, keepdims=True), sink[j])
            e = jnp.exp(s - m)
            den = jnp.sum(e, axis=-1, keepdims=True) + jnp.exp(sink[j] - m)
            probs.append((e * (1.0 / den)).astype(BF16))
    for i in range(nb):
        for j in range(SWA_KV_HEADS):
            vj = v_new[i][:, j * SWA_HD:(j + 1) * SWA_HD]
            oc_ref[i, j] = jnp.dot(probs[i * SWA_KV_HEADS + j], vj, preferred_element_type=F32)


def _sample_mixer(y, glr, sinks, w_lr_p, b_lr, gla_norm, conv_w,
                  state_gla, state_conv, cache_k, cache_v, layer):
    nb = y.shape[0]
    st = SAMPLE_STEP
    sq = y[:, COL_SQ:COL_SQ + SWA_Q].reshape(nb, SWA_KV_HEADS, SWA_GROUP, SWA_HD)
    sq = jnp.pad(sq, ((0, 0), (0, 0), (0, SUBLANES - SWA_GROUP), (0, 0)))

    def seg(width, col):
        return pl.BlockSpec((st, width), lambda b: (b, col // width))

    def per_layer(shape):
        return pl.BlockSpec((None,) + shape, lambda b: (layer,) + (0,) * len(shape))

    def state(shape):
        return pl.BlockSpec((None, st) + shape, lambda b: (layer, b) + (0,) * len(shape))

    def out(shape):
        return pl.BlockSpec((st,) + shape, lambda b: (b,) + (0,) * len(shape))

    res = pl.pallas_call(
        functools.partial(_sample_kernel, layer=layer),
        grid=(nb // st,),
        in_specs=[
            pl.BlockSpec(memory_space=pltpu.SMEM),
            seg(GLA_QK, COL_Q), seg(GLA_QK, COL_K), seg(GLA_V, COL_V), seg(GLA_V, COL_GR),
            pl.BlockSpec((st, LANES), lambda b: (b, 0)),
            seg(CONV_WIDTH, COL_CB), seg(CONV_WIDTH, COL_CC), seg(CONV_WIDTH, COL_CH),
            out((SWA_KV_HEADS, SUBLANES, SWA_HD)), seg(SWA_KV, COL_SK), seg(SWA_KV, COL_SV),
            per_layer((LANES, GLA_QK)), per_layer((1, GLA_QK)), per_layer((1, GLA_DV)),
            per_layer((CONV_K, CONV_WIDTH)),
            state((GLA_HEADS, GLA_DK, GLA_DV)), state(((CONV_K - 1) * CONV_WIDTH,)),
            state((WINDOW, SWA_KV)), state((WINDOW, SWA_KV)),
        ],
        out_specs=[
            out((GLA_V,)), out((CONV_WIDTH,)), out((SWA_KV_HEADS, SUBLANES, SWA_HD)),
            out((GLA_HEADS, GLA_DK, GLA_DV)), out(((CONV_K - 1) * CONV_WIDTH,)),
            out((WINDOW, SWA_KV)), out((WINDOW, SWA_KV)),
        ],
        out_shape=[
            jax.ShapeDtypeStruct((nb, GLA_V), F32),
            jax.ShapeDtypeStruct((nb, CONV_WIDTH), F32),
            jax.ShapeDtypeStruct((nb, SWA_KV_HEADS, SUBLANES, SWA_HD), F32),
            jax.ShapeDtypeStruct((nb, GLA_HEADS, GLA_DK, GLA_DV), F32),
            jax.ShapeDtypeStruct((nb, (CONV_K - 1) * CONV_WIDTH), F32),
            jax.ShapeDtypeStruct((nb, WINDOW, SWA_KV), F32),
            jax.ShapeDtypeStruct((nb, WINDOW, SWA_KV), F32),
        ],
        scratch_shapes=[pltpu.VMEM((st, GLA_V), F32)],
        compiler_params=_params(("parallel",)),
        name="sample_mixer",
    )(sinks, y, y, y, y, glr, y, y, y, sq, y, y,
      w_lr_p, b_lr, gla_norm, conv_w, state_gla, state_conv, cache_k, cache_v)
    oa, ob, oc, sg, sc, kc, vc = res
    oc = oc[:, :, :SWA_GROUP, :].reshape(nb, SWA_Q)
    sc = sc.reshape(nb, CONV_K - 1, CONV_WIDTH)
    return oa, ob, oc, sg, sc, kc, vc


def _merge_kernel(*refs, n_ptiles):
    (pa, pb, pc, pg0, pg1, pg2, px, sa, sb, sc, sg0, sg1, sg2, sx, wb_ref, wo_ref,
     po_ref, so_ref) = refs

    def tile(a_ref, b_ref, c_ref, g0_ref, g1_ref, g2_ref, x_ref, o_ref):
        @pl.when(pl.program_id(1) == 0)
        def _():
            o_ref[...] = x_ref[...]

        def branch(src_ref, gate_ref, n):
            br = jnp.dot(src_ref[...].astype(BF16), wb_ref[n], preferred_element_type=F32)
            return jax.nn.sigmoid(gate_ref[...].astype(F32)) * br

        mixed = branch(a_ref, g0_ref, 0) + branch(b_ref, g1_ref, 1) + branch(c_ref, g2_ref, 2)
        o_ref[...] += jnp.dot(mixed.astype(BF16), wo_ref[...], preferred_element_type=F32)

    @pl.when(pl.program_id(0) < n_ptiles)
    def _():
        tile(pa, pb, pc, pg0, pg1, pg2, px, po_ref)

    @pl.when(pl.program_id(0) == n_ptiles)
    def _():
        tile(sa, sb, sc, sg0, sg1, sg2, sx, so_ref)


def _merge(prompt, sample, w_branch, w_out, layer, tm, tn):
    mp, ms = prompt[4].shape[0], sample[4].shape[0]
    n_ptiles = mp // tm
    prow = lambda i: jnp.minimum(i, n_ptiles - 1)
    gate_col = lambda n: (n * D_MODEL) // tn
    once = pl.Buffered(1)
    p_src = pl.BlockSpec((tm, BRANCH_WIDTH), lambda i, t: (prow(i), 0), pipeline_mode=once)
    p_gate = lambda n: pl.BlockSpec((tm, tn), lambda i, t: (prow(i), gate_col(n) + t))
    s_src = pl.BlockSpec((ms, BRANCH_WIDTH), lambda i, t: (0, 0))
    s_gate = lambda n: pl.BlockSpec((ms, tn), lambda i, t: (0, gate_col(n) + t))
    return pl.pallas_call(
        functools.partial(_merge_kernel, n_ptiles=n_ptiles),
        grid=(n_ptiles + 1, D_MODEL // tn),
        in_specs=[
            p_src, p_src, p_src, p_gate(0), p_gate(1), p_gate(2),
            pl.BlockSpec((tm, D_MODEL), lambda i, t: (prow(i), 0), pipeline_mode=once),
            s_src, s_src, s_src, s_gate(0), s_gate(1), s_gate(2),
            pl.BlockSpec((ms, D_MODEL), lambda i, t: (0, 0)),
            pl.BlockSpec((None, N_BRANCH, BRANCH_WIDTH, tn), lambda i, t: (layer, 0, 0, t)),
            pl.BlockSpec((None, tn, D_MODEL), lambda i, t: (layer, t, 0)),
        ],
        out_specs=[
            pl.BlockSpec((tm, D_MODEL), lambda i, t: (prow(i), 0)),
            pl.BlockSpec((ms, D_MODEL), lambda i, t: (0, 0)),
        ],
        out_shape=[
            jax.ShapeDtypeStruct((mp, D_MODEL), F32),
            jax.ShapeDtypeStruct((ms, D_MODEL), F32),
        ],
        compiler_params=_params(("arbitrary", "arbitrary")),
        name="merge_outproj",
    )(prompt[0], prompt[1], prompt[2], prompt[3], prompt[3], prompt[3], prompt[4],
      sample[0], sample[1], sample[2], sample[3], sample[3], sample[3], sample[4],
      w_branch, w_out)


def _mlp_kernel(xp_ref, xs_ref, g_ref, wu_ref, wd_ref, gnext_ref, wlr_ref, *rest,
                last, n_ptiles):
    if last:
        po, so, hp_scr, hs_scr = rest
        p_out, s_out = (po,), (so,)
    else:
        po, php, pglr, so, shp, sglr, hp_scr, hs_scr = rest
        p_out, s_out = (po, php, pglr), (so, shp, sglr)
    f = pl.program_id(1)

    def tile(x_ref, h_ref, o_ref, hn_ref=None, glr_ref=None):
        @pl.when(f == 0)
        def _():
            x = x_ref[...]
            h_ref[...] = _rmsnorm_rows(x, g_ref[...]).astype(BF16)
            o_ref[...] = x

        up = jnp.dot(h_ref[...], wu_ref[...], preferred_element_type=F32)
        act = jnp.square(jnp.maximum(up, 0.0)).astype(BF16)
        o_ref[...] += jnp.dot(act, wd_ref[...], preferred_element_type=F32)

        @pl.when(f == pl.num_programs(1) - 1)
        def _():
            xn = _rmsnorm_rows(o_ref[...], gnext_ref[...])
            if last:
                o_ref[...] = xn
            else:
                hn = xn.astype(BF16)
                hn_ref[...] = hn
                glr_ref[...] = _lr_project(hn, wlr_ref)

    @pl.when(pl.program_id(0) < n_ptiles)
    def _():
        tile(xp_ref, hp_scr, *p_out)

    @pl.when(pl.program_id(0) == n_ptiles)
    def _():
        tile(xs_ref, hs_scr, *s_out)


def _mlp(xp, xs, norm_w, w_up, w_down, next_norm_w, w_in_t, layer, tm, tf):
    mp, ms = xp.shape[0], xs.shape[0]
    n_ptiles = mp // tm
    last = layer == DEPTH - 1
    prow = lambda i: jnp.minimum(i, n_ptiles - 1)

    def group_out(m, rows, row_map):
        specs = [pl.BlockSpec((rows, D_MODEL), lambda i, f: (row_map(i), 0))]
        shapes = [jax.ShapeDtypeStruct((m, D_MODEL), F32)]
        if not last:
            specs += [pl.BlockSpec((rows, D_MODEL), lambda i, f: (row_map(i), 0)),
                      pl.BlockSpec((rows, LANES), lambda i, f: (row_map(i), 0))]
            shapes += [jax.ShapeDtypeStruct((m, D_MODEL), BF16),
                       jax.ShapeDtypeStruct((m, LANES), F32)]
        return specs, shapes

    p_specs, p_shapes = group_out(mp, tm, prow)
    s_specs, s_shapes = group_out(ms, ms, lambda i: 0)
    if last:
        next_norm_spec = pl.BlockSpec((1, D_MODEL), lambda i, f: (0, 0))
        lr_layer = layer
    else:
        next_norm_spec = pl.BlockSpec((None, 1, D_MODEL), lambda i, f: (layer + 1, 0, 0))
        lr_layer = layer + 1
    return pl.pallas_call(
        functools.partial(_mlp_kernel, last=last, n_ptiles=n_ptiles),
        grid=(n_ptiles + 1, D_FF // tf),
        in_specs=[
            pl.BlockSpec((tm, D_MODEL), lambda i, f: (prow(i), 0), pipeline_mode=pl.Buffered(1)),
            pl.BlockSpec((ms, D_MODEL), lambda i, f: (0, 0)),
            pl.BlockSpec((None, 1, D_MODEL), lambda i, f: (layer, 0, 0)),
            pl.BlockSpec((None, D_MODEL, tf), lambda i, f: (layer, 0, f)),
            pl.BlockSpec((None, tf, D_MODEL), lambda i, f: (layer, f, 0)),
            next_norm_spec,
            _lr_weight_spec(lr_layer),
        ],
        out_specs=p_specs + s_specs,
        out_shape=p_shapes + s_shapes,
        scratch_shapes=[pltpu.VMEM((tm, D_MODEL), BF16), pltpu.VMEM((ms, D_MODEL), BF16)],
        compiler_params=_params(("arbitrary", "arbitrary")),
        name="mlp",
    )(xp, xs, norm_w, w_up, w_down, next_norm_w, w_in_t)


def kernel(x_prompt, x_sample, state_gla, state_conv, cache_k, cache_v, w_in, w_lr, b_lr,
           gla_norm, conv_w, attn_sinks, w_branch, w_out, norm_mix, norm_mlp, w_up, w_down,
           norm_final):
    batch, seq, _ = x_prompt.shape
    nb = x_sample.shape[0]
    mp = batch * seq

    w_lr_p = jnp.pad(w_lr, ((0, 0), (0, LANES - GLA_RANK), (0, 0)))
    w_branch_b = w_branch.astype(BF16)
    w_out_b = w_out.astype(BF16)
    w_up_b = w_up.astype(BF16)
    w_down_b = w_down.astype(BF16)
    b_lr3 = b_lr.reshape(DEPTH, 1, GLA_QK)
    gla_norm3 = gla_norm.reshape(DEPTH, 1, GLA_DV)
    norm_mix3 = norm_mix.reshape(DEPTH, 1, D_MODEL)
    norm_mlp3 = norm_mlp.reshape(DEPTH, 1, D_MODEL)
    norm_final2 = norm_final.reshape(1, D_MODEL)
    state_conv2 = state_conv.reshape(DEPTH, nb, (CONV_K - 1) * CONV_WIDTH)
    cache_k4 = cache_k.reshape(DEPTH, nb, WINDOW, SWA_KV)
    cache_v4 = cache_v.reshape(DEPTH, nb, WINDOW, SWA_KV)

    w_in_t = jnp.swapaxes(w_in, 1, 2)
    tm = min(1024, mp)

    xp = x_prompt.reshape(mp, D_MODEL)
    xs = x_sample.reshape(nb, D_MODEL)
    hp, glrp = _norm(xp, norm_mix3, w_in_t, 0, tm=tm)
    hs, glrs = _norm(xs, norm_mix3, w_in_t, 0, tm=nb)
    outs = [[] for _ in range(8)]
    for l in range(DEPTH):
        next_norm = norm_final2 if l == DEPTH - 1 else norm_mix3
        yp, gp, ys, gs = _inproj(hp, hs, w_in_t, l, tm=tm, tn=768)
        oa, sg_p = _gla_prompt(yp, glrp, w_lr_p, b_lr3, gla_norm3, l, batch, seq)
        ob, sc_p = _conv_prompt(yp, conv_w, l, batch, seq, tl=512)
        oc = _swa_prompt(yp, attn_sinks, l, batch, seq)
        yp3 = yp.reshape(batch, seq, COL_GATES)
        kp = yp3[:, seq - WINDOW:, COL_SK:COL_SK + SWA_KV]
        vp = yp3[:, seq - WINDOW:, COL_SV:COL_SV + SWA_KV]
        sa, sb, sc, sg_s, sc_s, ks, vs = _sample_mixer(
            ys, glrs, attn_sinks, w_lr_p, b_lr3, gla_norm3, conv_w,
            state_gla, state_conv2, cache_k4, cache_v4, l)
        xp, xs = _merge((oa, ob, oc, gp, xp), (sa, sb, sc, gs, xs), w_branch_b, w_out_b, l,
                        tm=tm, tn=512)
        res = _mlp(xp, xs, norm_mlp3, w_up_b, w_down_b, next_norm, w_in_t, l, tm=tm, tf=512)
        if l == DEPTH - 1:
            xp, xs = res
        else:
            xp, hp, glrp, xs, hs, glrs = res
        for lst, val in zip(outs, (
                sg_p, sg_s, sc_p, sc_s,
                kp.reshape(batch, WINDOW, SWA_KV_HEADS, SWA_HD),
                ks.reshape(nb, WINDOW, SWA_KV_HEADS, SWA_HD),
                vp.reshape(batch, WINDOW, SWA_KV_HEADS, SWA_HD),
                vs.reshape(nb, WINDOW, SWA_KV_HEADS, SWA_HD))):
            lst.append(val)
    y_prompt = xp.reshape(batch, seq, D_MODEL)
    y_sample = xs.reshape(nb, 1, D_MODEL)
    return (y_prompt, y_sample) + tuple(jnp.stack(o) for o in outs)
```

```python
import functools

import jax
import jax.numpy as jnp
from jax import lax
from jax.experimental import pallas as pl
from jax.experimental.pallas import tpu as pltpu

F32 = jnp.float32
BF16 = jnp.bfloat16
HIGHEST = lax.Precision.HIGHEST

D_MODEL = 2048
DEPTH = 4
PAST_LEN = 16384
BRANCH_WIDTH = D_MODEL // 2
N_BRANCH = 3
GLA_HEADS = 4
GLA_DV = BRANCH_WIDTH // GLA_HEADS
GLA_DK = GLA_DV // 2
GLA_RANK = 16
GLA_TAU = 16.0
GLA_CHUNK = 64
CONV_WIDTH = BRANCH_WIDTH
CONV_K = 3
SWA_HD = 64
SWA_HEADS = BRANCH_WIDTH // SWA_HD
SWA_KV_HEADS = SWA_HEADS // 4
SWA_GROUP = SWA_HEADS // SWA_KV_HEADS
WINDOW = 128
D_FF = 4 * D_MODEL
EPS = 1e-6

LANES = 128
SUBLANES = 8
VMEM_LIMIT = 56 * 1024 * 1024

GLA_QK = GLA_HEADS * GLA_DK
GLA_V = GLA_HEADS * GLA_DV
SWA_Q = SWA_HEADS * SWA_HD
SWA_KV = SWA_KV_HEADS * SWA_HD

COL_Q = 0
COL_K = COL_Q + GLA_QK
COL_V = COL_K + GLA_QK
COL_GR = COL_V + GLA_V
COL_CB = COL_GR + GLA_V
COL_CC = COL_CB + CONV_WIDTH
COL_CH = COL_CC + CONV_WIDTH
COL_SQ = COL_CH + CONV_WIDTH
COL_SK = COL_SQ + SWA_Q
COL_SV = COL_SK + SWA_KV
COL_GATES = COL_SV + SWA_KV
PACKED_COLS = COL_GATES + N_BRANCH * D_MODEL
LR_SRC = COL_CB

ALIBI_SLOPES = tuple(2.0 ** (-8.0 * (h + 1) / SWA_HEADS) for h in range(SWA_HEADS))
SWA_SCALE = SWA_HD ** -0.5

NT_DIMS = (((1,), (1,)), ((), ()))
TN_DIMS = (((0,), (0,)), ((), ()))


def _params(sem):
    return pltpu.CompilerParams(dimension_semantics=sem, vmem_limit_bytes=VMEM_LIMIT)


def _rmsnorm_rows(x, g):
    ms = jnp.mean(x * x, axis=-1, keepdims=True)
    return x * lax.rsqrt(ms + EPS) * g


def _log_sigmoid(z):
    return jnp.minimum(z, 0.0) - jnp.log1p(jnp.exp(-jnp.abs(z)))


def _silu(x):
    return x * jax.nn.sigmoid(x)


def _lr_weight_spec(layer):
    return pl.BlockSpec((None, LANES, D_MODEL), lambda *a: (layer, LR_SRC // LANES, 0))


def _lr_project(h, wlr_ref):
    return lax.dot_general(h, wlr_ref[...].astype(BF16), NT_DIMS, preferred_element_type=F32)


def _norm_kernel(x_ref, g_ref, wlr_ref, h_ref, glr_ref):
    h = _rmsnorm_rows(x_ref[...], g_ref[...]).astype(BF16)
    h_ref[...] = h
    glr_ref[...] = _lr_project(h, wlr_ref)


def _norm(x, norm_w, w_in_t, layer, tm):
    m = x.shape[0]
    return pl.pallas_call(
        _norm_kernel,
        grid=(m // tm,),
        in_specs=[
            pl.BlockSpec((tm, D_MODEL), lambda i: (i, 0)),
            pl.BlockSpec((None, 1, D_MODEL), lambda i: (layer, 0, 0)),
            _lr_weight_spec(layer),
        ],
        out_specs=[
            pl.BlockSpec((tm, D_MODEL), lambda i: (i, 0)),
            pl.BlockSpec((tm, LANES), lambda i: (i, 0)),
        ],
        out_shape=[
            jax.ShapeDtypeStruct((m, D_MODEL), BF16),
            jax.ShapeDtypeStruct((m, LANES), F32),
        ],
        compiler_params=_params(("parallel",)),
        name="norm",
    )(x, norm_w, w_in_t)


def _inproj_kernel(hp_ref, hs_ref, w_ref, yp_ref, gp_ref, ys_ref, gs_ref, wbf_ref, *,
                   n_main, n_ptiles):
    j = pl.program_id(0)
    i = pl.program_id(1)

    @pl.when(i == 0)
    def _():
        wbf_ref[...] = w_ref[...].astype(BF16)

    def project(h_ref):
        return lax.dot_general(h_ref[...], wbf_ref[...], NT_DIMS, preferred_element_type=F32)

    is_main = j < n_main
    is_prompt = i < n_ptiles

    @pl.when(jnp.logical_and(is_main, is_prompt))
    def _():
        yp_ref[...] = project(hp_ref)

    @pl.when(jnp.logical_and(jnp.logical_not(is_main), is_prompt))
    def _():
        gp_ref[...] = project(hp_ref).astype(BF16)

    @pl.when(jnp.logical_and(is_main, jnp.logical_not(is_prompt)))
    def _():
        ys_ref[...] = project(hs_ref)

    @pl.when(jnp.logical_and(jnp.logical_not(is_main), jnp.logical_not(is_prompt)))
    def _():
        gs_ref[...] = project(hs_ref).astype(BF16)


def _inproj(hp, hs, w_in_t, layer, tm, tn):
    mp, ms = hp.shape[0], hs.shape[0]
    n_ptiles = mp // tm
    n_main = COL_GATES // tn
    n_gate = (N_BRANCH * D_MODEL) // tn
    n_plain = LR_SRC // tn
    last_p = n_ptiles - 1
    prow = lambda i: jnp.minimum(i, last_p)
    w_row = lambda j: pl.multiple_of(j * tn + jnp.where(j >= n_plain, GLA_RANK, 0), GLA_RANK)
    return pl.pallas_call(
        functools.partial(_inproj_kernel, n_main=n_main, n_ptiles=n_ptiles),
        grid=(n_main + n_gate, n_ptiles + 1),
        in_specs=[
            pl.BlockSpec((tm, D_MODEL), lambda j, i: (prow(i), 0)),
            pl.BlockSpec((ms, D_MODEL), lambda j, i: (0, 0)),
            pl.BlockSpec((None, pl.Element(tn), pl.Element(D_MODEL)),
                         lambda j, i: (layer, w_row(j), 0)),
        ],
        out_specs=[
            pl.BlockSpec((tm, tn), lambda j, i: (jnp.where(j < n_main, prow(i), last_p),
                                                 jnp.minimum(j, n_main - 1))),
            pl.BlockSpec((tm, tn), lambda j, i: (jnp.where(j < n_main, 0, prow(i)),
                                                 jnp.maximum(j - n_main, 0))),
            pl.BlockSpec((ms, tn), lambda j, i: (0, jnp.minimum(j, n_main - 1))),
            pl.BlockSpec((ms, tn), lambda j, i: (0, jnp.maximum(j - n_main, 0))),
        ],
        out_shape=[
            jax.ShapeDtypeStruct((mp, COL_GATES), F32),
            jax.ShapeDtypeStruct((mp, N_BRANCH * D_MODEL), BF16),
            jax.ShapeDtypeStruct((ms, COL_GATES), F32),
            jax.ShapeDtypeStruct((ms, N_BRANCH * D_MODEL), BF16),
        ],
        scratch_shapes=[pltpu.VMEM((tn, D_MODEL), BF16)],
        compiler_params=_params(("arbitrary", "arbitrary")),
        name="inproj",
    )(hp, hs, w_in_t)


GLA_STEP_CHUNKS = 4


def _gla_kernel(q_ref, k_ref, v_ref, gr_ref, glr_ref, wlr_ref, blr_ref, gn_ref,
                oa_ref, sfin_ref, st_ref):
    c = pl.program_id(1)
    cs = GLA_CHUNK

    @pl.when(c == 0)
    def _():
        st_ref[...] = jnp.zeros_like(st_ref)

    z = jnp.dot(glr_ref[...], wlr_ref[...], precision=HIGHEST,
                preferred_element_type=F32) + blr_ref[...]
    log_a = _log_sigmoid(z) / GLA_TAU
    tr = GLA_STEP_CHUNKS * cs
    row = lax.broadcasted_iota(jnp.int32, (tr, tr), 0)
    col = lax.broadcasted_iota(jnp.int32, (tr, tr), 1)
    log2_cs = cs.bit_length() - 1
    same_chunk = jnp.right_shift(row, log2_cs) == jnp.right_shift(col, log2_cs)
    causal = jnp.logical_and(same_chunk, row >= col)
    tri = jnp.where(causal, 1.0, 0.0).astype(F32)
    chunk_ones = jnp.where(same_chunk, 1.0, 0.0).astype(F32)
    gn = gn_ref[...]

    b = jnp.dot(tri, log_a, precision=HIGHEST, preferred_element_type=F32)
    b_last = jnp.dot(chunk_ones, log_a, precision=HIGHEST, preferred_element_type=F32)
    q = q_ref[...] * (GLA_DK ** -0.5)
    k = k_ref[...]
    qt = (q * jnp.exp(b)).astype(BF16)
    kt = (k * jnp.exp(-b)).astype(BF16)
    kd = (k * jnp.exp(b_last - b)).astype(BF16)
    g_last = jnp.exp(b_last)
    v = v_ref[...].astype(BF16)

    for h in range(GLA_HEADS):
        ks = slice(h * GLA_DK, (h + 1) * GLA_DK)
        vs = slice(h * GLA_DV, (h + 1) * GLA_DV)
        a = lax.dot_general(qt[:, ks], kt[:, ks], NT_DIMS, preferred_element_type=F32)
        a = jnp.where(causal, a, 0.0).astype(BF16)
        o_intra = jnp.dot(a, v[:, vs], preferred_element_type=F32)
        u_t = [lax.dot_general(v[n * cs:(n + 1) * cs, vs], kd[n * cs:(n + 1) * cs, ks], TN_DIMS,
                               preferred_element_type=F32) for n in range(GLA_STEP_CHUNKS)]
        s_t = st_ref[h]
        for n in range(GLA_STEP_CHUNKS):
            rs = slice(n * cs, (n + 1) * cs)
            o = lax.dot_general(qt[rs, ks], s_t.astype(BF16), NT_DIMS,
                                preferred_element_type=F32) + o_intra[rs, :]
            s_t = g_last[n * cs:n * cs + 1, ks] * s_t + u_t[n]
            oa_ref[rs, vs] = (_rmsnorm_rows(o, gn) * _silu(gr_ref[rs, vs])).astype(BF16)
        st_ref[h] = s_t

    @pl.when(c == pl.num_programs(1) - 1)
    def _():
        for h in range(GLA_HEADS):
            sfin_ref[0, h] = st_ref[h].T


def _gla_prompt(y, glr, w_lr_p, b_lr, gla_norm, layer, batch, seq):
    tr = GLA_STEP_CHUNKS * GLA_CHUNK
    ns = seq // tr
    rows = lambda b, c: b * ns + c
    return pl.pallas_call(
        _gla_kernel,
        grid=(batch, ns),
        in_specs=[
            pl.BlockSpec((tr, GLA_QK), lambda b, c: (rows(b, c), COL_Q // GLA_QK)),
            pl.BlockSpec((tr, GLA_QK), lambda b, c: (rows(b, c), COL_K // GLA_QK)),
            pl.BlockSpec((tr, GLA_V), lambda b, c: (rows(b, c), COL_V // GLA_V)),
            pl.BlockSpec((tr, GLA_V), lambda b, c: (rows(b, c), COL_GR // GLA_V)),
            pl.BlockSpec((tr, LANES), lambda b, c: (rows(b, c), 0)),
            pl.BlockSpec((None, LANES, GLA_QK), lambda b, c: (layer, 0, 0)),
            pl.BlockSpec((None, 1, GLA_QK), lambda b, c: (layer, 0, 0)),
            pl.BlockSpec((None, 1, GLA_DV), lambda b, c: (layer, 0, 0)),
        ],
        out_specs=[
            pl.BlockSpec((tr, GLA_V), lambda b, c: (rows(b, c), 0)),
            pl.BlockSpec((1, GLA_HEADS, GLA_DK, GLA_DV), lambda b, c: (b, 0, 0, 0)),
        ],
        out_shape=[
            jax.ShapeDtypeStruct((batch * seq, GLA_V), BF16),
            jax.ShapeDtypeStruct((batch, GLA_HEADS, GLA_DK, GLA_DV), F32),
        ],
        scratch_shapes=[pltpu.VMEM((GLA_HEADS, GLA_DV, GLA_DK), F32)],
        compiler_params=_params(("parallel", "arbitrary")),
        name="gla_prompt",
    )(y, y, y, y, glr, w_lr_p, b_lr, gla_norm)


def _conv_kernel(cb_ref, cc_ref, ch_ref, w_ref, ob_ref, cs_ref, prev_ref):
    @pl.when(pl.program_id(1) == 0)
    def _():
        prev_ref[...] = jnp.zeros_like(prev_ref)

    u = cc_ref[...] * ch_ref[...]
    tl = u.shape[0]
    prev = prev_ref[...]
    p_m1 = prev[SUBLANES - 1:SUBLANES, :]
    p_m2 = prev[SUBLANES - 2:SUBLANES - 1, :]
    row = lax.broadcasted_iota(jnp.int32, u.shape, 0)
    u1 = jnp.where(row == 0, p_m1, pltpu.roll(u, 1, 0))
    u2 = jnp.where(row == 0, p_m2, jnp.where(row == 1, p_m1, pltpu.roll(u, 2, 0)))
    w = w_ref[...]
    conv = w[0:1, :] * u2 + w[1:2, :] * u1 + w[2:3, :] * u
    ob_ref[...] = (cb_ref[...] * conv).astype(BF16)
    prev_ref[...] = u[tl - SUBLANES:tl, :]
    cs_ref[0] = u[tl - (CONV_K - 1):tl, :]


def _conv_prompt(y, conv_w, layer, batch, seq, tl):
    nt = seq // tl
    rows = lambda b, t: b * nt + t
    return pl.pallas_call(
        _conv_kernel,
        grid=(batch, nt),
        in_specs=[
            pl.BlockSpec((tl, CONV_WIDTH), lambda b, t: (rows(b, t), COL_CB // CONV_WIDTH)),
            pl.BlockSpec((tl, CONV_WIDTH), lambda b, t: (rows(b, t), COL_CC // CONV_WIDTH)),
            pl.BlockSpec((tl, CONV_WIDTH), lambda b, t: (rows(b, t), COL_CH // CONV_WIDTH)),
            pl.BlockSpec((None, CONV_K, CONV_WIDTH), lambda b, t: (layer, 0, 0)),
        ],
        out_specs=[
            pl.BlockSpec((tl, CONV_WIDTH), lambda b, t: (rows(b, t), 0)),
            pl.BlockSpec((1, CONV_K - 1, CONV_WIDTH), lambda b, t: (b, 0, 0)),
        ],
        out_shape=[
            jax.ShapeDtypeStruct((batch * seq, CONV_WIDTH), BF16),
            jax.ShapeDtypeStruct((batch, CONV_K - 1, CONV_WIDTH), F32),
        ],
        scratch_shapes=[pltpu.VMEM((SUBLANES, CONV_WIDTH), F32)],
        compiler_params=_params(("parallel", "arbitrary")),
        name="conv_prompt",
    )(y, y, y, conv_w)


def _swa_kernel(sinks_ref, q_ref, kc_ref, kp_ref, vc_ref, vp_ref, o_ref, bias_ref, *, layer):
    blk = pl.program_id(1)

    key = lax.broadcasted_iota(jnp.int32, (2 * WINDOW, WINDOW), 0)

    @pl.when(blk == 0)
    def _():
        qry = lax.broadcasted_iota(jnp.int32, (2 * WINDOW, WINDOW), 1)
        dist = qry - key + WINDOW
        valid = jnp.logical_and(dist >= 0, dist < WINDOW)
        dist_f = dist.astype(F32)
        for h in range(SWA_HEADS):
            bias_ref[h] = jnp.where(valid, -ALIBI_SLOPES[h] * dist_f, -jnp.inf)

    key_exists = jnp.logical_or(blk > 0, key >= WINDOW)
    q = (q_ref[...] * SWA_SCALE).astype(BF16)
    k2 = jnp.concatenate([kp_ref[...], kc_ref[...]], axis=0).astype(BF16)
    v2_t = jnp.concatenate([vp_ref[...], vc_ref[...]], axis=0).T.astype(BF16)
    scores = []
    for h in range(SWA_HEADS):
        j = h // SWA_GROUP
        scores.append(lax.dot_general(k2[:, j * SWA_HD:(j + 1) * SWA_HD],
                                      q[:, h * SWA_HD:(h + 1) * SWA_HD], NT_DIMS,
                                      preferred_element_type=F32))
    probs = []
    for h in range(SWA_HEADS):
        s = jnp.where(key_exists, scores[h] + bias_ref[h], -jnp.inf)
        sink = sinks_ref[layer, h]
        m = jnp.maximum(jnp.max(s, axis=0, keepdims=True), sink)
        e = jnp.exp(s - m)
        den = jnp.sum(e, axis=0, keepdims=True) + jnp.exp(sink - m)
        probs.append((e * (1.0 / den)).astype(BF16))
    outs_t = []
    for h in range(SWA_HEADS):
        j = h // SWA_GROUP
        outs_t.append(jnp.dot(v2_t[j * SWA_HD:(j + 1) * SWA_HD, :], probs[h],
                              preferred_element_type=F32))
    o_ref[...] = jnp.concatenate(outs_t, axis=0).T.astype(BF16)


def _swa_prompt(y, sinks, layer, batch, seq):
    nb = seq // WINDOW
    rows = lambda b, i: b * nb + i
    prev_rows = lambda b, i: b * nb + jnp.maximum(i - 1, 0)
    return pl.pallas_call(
        functools.partial(_swa_kernel, layer=layer),
        grid=(batch, nb),
        in_specs=[
            pl.BlockSpec(memory_space=pltpu.SMEM),
            pl.BlockSpec((WINDOW, SWA_Q), lambda b, i: (rows(b, i), COL_SQ // SWA_Q)),
            pl.BlockSpec((WINDOW, SWA_KV), lambda b, i: (rows(b, i), COL_SK // SWA_KV)),
            pl.BlockSpec((WINDOW, SWA_KV), lambda b, i: (prev_rows(b, i), COL_SK // SWA_KV)),
            pl.BlockSpec((WINDOW, SWA_KV), lambda b, i: (rows(b, i), COL_SV // SWA_KV)),
            pl.BlockSpec((WINDOW, SWA_KV), lambda b, i: (prev_rows(b, i), COL_SV // SWA_KV)),
        ],
        out_specs=pl.BlockSpec((WINDOW, SWA_Q), lambda b, i: (rows(b, i), 0)),
        out_shape=jax.ShapeDtypeStruct((batch * seq, SWA_Q), BF16),
        scratch_shapes=[pltpu.VMEM((SWA_HEADS, 2 * WINDOW, WINDOW), F32)],
        compiler_params=_params(("parallel", "arbitrary")),
        name="swa_prompt",
    )(sinks, y, y, y, y, y)


SAMPLE_STEP = SUBLANES


def _sample_kernel(sinks_ref, q_ref, k_ref, v_ref, gr_ref, glr_ref, cb_ref, cc_ref, ch_ref,
                   sq_ref, sk_ref, sv_ref, wlr_ref, blr_ref, gn_ref, cw_ref,
                   s_ref, cs_ref, kc_ref, vc_ref,
                   oa_ref, ob_ref, oc_ref, so_ref, cso_ref, kco_ref, vco_ref,
                   o_scr, *, layer):
    nb = SAMPLE_STEP
    z = jnp.dot(glr_ref[...], wlr_ref[...], precision=HIGHEST,
                preferred_element_type=F32) + blr_ref[...]
    b = _log_sigmoid(z) / GLA_TAU
    q = q_ref[...] * (GLA_DK ** -0.5)
    k = k_ref[...]
    v = v_ref[...]
    qt = q * jnp.exp(b)
    kt = k * jnp.exp(-b)
    kd = k * jnp.exp(b - b)
    g_last = jnp.exp(b)
    qt_t = qt.T
    kd_t = kd.T
    gl_t = g_last.T
    for h in range(GLA_HEADS):
        ks = slice(h * GLA_DK, (h + 1) * GLA_DK)
        vs = slice(h * GLA_DV, (h + 1) * GLA_DV)
        a = jnp.sum(qt[:, ks] * kt[:, ks], axis=-1, keepdims=True)
        o_intra = a * v[:, vs]
        for i in range(nb):
            s_in = s_ref[i, h]
            v_row = v[i:i + 1, vs]
            o = jnp.sum(qt_t[ks, i:i + 1] * s_in, axis=0, keepdims=True)
            o_scr[i:i + 1, vs] = o + o_intra[i:i + 1, :]
            so_ref[i, h] = gl_t[ks, i:i + 1] * s_in + kd_t[ks, i:i + 1] * v_row
    gn = gn_ref[...]
    for h in range(GLA_HEADS):
        vs = slice(h * GLA_DV, (h + 1) * GLA_DV)
        oa_ref[:, vs] = _rmsnorm_rows(o_scr[:, vs], gn) * _silu(gr_ref[:, vs])

    u_new = cc_ref[...] * ch_ref[...]
    w = cw_ref[...]
    p0 = cs_ref[:, :CONV_WIDTH]
    p1 = cs_ref[:, CONV_WIDTH:]
    conv = w[0:1, :] * p0 + w[1:2, :] * p1 + w[2:3, :] * u_new
    ob_ref[...] = cb_ref[...] * conv
    cso_ref[:, :CONV_WIDTH] = p1
    cso_ref[:, CONV_WIDTH:] = u_new

    rowk = lax.broadcasted_iota(jnp.int32, (WINDOW, SWA_KV), 0)
    key = lax.broadcasted_iota(jnp.int32, (SUBLANES, WINDOW), 1)
    grp = lax.broadcasted_iota(jnp.int32, (SUBLANES, WINDOW), 0)
    dist = (WINDOW - 1 - key).astype(F32)
    bias, sink = [], []
    for j in range(SWA_KV_HEADS):
        bj = jnp.zeros((SUBLANES, WINDOW), F32)
        sj = jnp.zeros((SUBLANES, 1), F32)
        for g in range(SWA_GROUP):
            bj = jnp.where(grp == g, -ALIBI_SLOPES[j * SWA_GROUP + g] * dist, bj)
            sj = jnp.where(grp[:, 0:1] == g, sinks_ref[layer, j * SWA_GROUP + g], sj)
        bias.append(bj)
        sink.append(sj)
    k_new, v_new = [], []
    for i in range(nb):
        kn = jnp.where(rowk == WINDOW - 1, sk_ref[i:i + 1, :], pltpu.roll(kc_ref[i], WINDOW - 1, 0))
        vn = jnp.where(rowk == WINDOW - 1, sv_ref[i:i + 1, :], pltpu.roll(vc_ref[i], WINDOW - 1, 0))
        kco_ref[i] = kn
        vco_ref[i] = vn
        k_new.append(kn.astype(BF16))
        v_new.append(vn.astype(BF16))
    scores = []
    for i in range(nb):
        for j in range(SWA_KV_HEADS):
            qj = (sq_ref[i, j] * SWA_SCALE).astype(BF16)
            kj = k_new[i][:, j * SWA_HD:(j + 1) * SWA_HD]
            scores.append(lax.dot_general(qj, kj, NT_DIMS, preferred_element_type=F32))
    probs = []
    for i in range(nb):
        for j in range(SWA_KV_HEADS):
            s = scores[i * SWA_KV_HEADS + j] + bias[j]
            m = jnp.maximum(jnp.max(s, axis=-1, keepdims=True), sink[j])
            e = jnp.exp(s - m)
            den = jnp.sum(e, axis=-1, keepdims=True) + jnp.exp(sink[j] - m)
            probs.append((e * (1.0 / den)).astype(BF16))
    for i in range(nb):
        for j in range(SWA_KV_HEADS):
            vj = v_new[i][:, j * SWA_HD:(j + 1) * SWA_HD]
            oc_ref[i, j] = jnp.dot(probs[i * SWA_KV_HEADS + j], vj, preferred_element_type=F32)


def _sample_mixer(y, glr, sinks, w_lr_p, b_lr, gla_norm, conv_w,
                  state_gla, state_conv, cache_k, cache_v, layer):
    nb = y.shape[0]
    st = SAMPLE_STEP
    sq = y[:, COL_SQ:COL_SQ + SWA_Q].reshape(nb, SWA_KV_HEADS, SWA_GROUP, SWA_HD)
    sq = jnp.pad(sq, ((0, 0), (0, 0), (0, SUBLANES - SWA_GROUP), (0, 0)))

    def seg(width, col):
        return pl.BlockSpec((st, width), lambda b: (b, col // width))

    def per_layer(shape):
        return pl.BlockSpec((None,) + shape, lambda b: (layer,) + (0,) * len(shape))

    def state(shape):
        return pl.BlockSpec((None, st) + shape, lambda b: (layer, b) + (0,) * len(shape))

    def out(shape):
        return pl.BlockSpec((st,) + shape, lambda b: (b,) + (0,) * len(shape))

    res = pl.pallas_call(
        functools.partial(_sample_kernel, layer=layer),
        grid=(nb // st,),
        in_specs=[
            pl.BlockSpec(memory_space=pltpu.SMEM),
            seg(GLA_QK, COL_Q), seg(GLA_QK, COL_K), seg(GLA_V, COL_V), seg(GLA_V, COL_GR),
            pl.BlockSpec((st, LANES), lambda b: (b, 0)),
            seg(CONV_WIDTH, COL_CB), seg(CONV_WIDTH, COL_CC), seg(CONV_WIDTH, COL_CH),
            out((SWA_KV_HEADS, SUBLANES, SWA_HD)), seg(SWA_KV, COL_SK), seg(SWA_KV, COL_SV),
            per_layer((LANES, GLA_QK)), per_layer((1, GLA_QK)), per_layer((1, GLA_DV)),
            per_layer((CONV_K, CONV_WIDTH)),
            state((GLA_HEADS, GLA_DK, GLA_DV)), state(((CONV_K - 1) * CONV_WIDTH,)),
            state((WINDOW, SWA_KV)), state((WINDOW, SWA_KV)),
        ],
        out_specs=[
            out((GLA_V,)), out((CONV_WIDTH,)), out((SWA_KV_HEADS, SUBLANES, SWA_HD)),
            out((GLA_HEADS, GLA_DK, GLA_DV)), out(((CONV_K - 1) * CONV_WIDTH,)),
            out((WINDOW, SWA_KV)), out((WINDOW, SWA_KV)),
        ],
        out_shape=[
            jax.ShapeDtypeStruct((nb, GLA_V), F32),
            jax.ShapeDtypeStruct((nb, CONV_WIDTH), F32),
            jax.ShapeDtypeStruct((nb, SWA_KV_HEADS, SUBLANES, SWA_HD), F32),
            jax.ShapeDtypeStruct((nb, GLA_HEADS, GLA_DK, GLA_DV), F32),
            jax.ShapeDtypeStruct((nb, (CONV_K - 1) * CONV_WIDTH), F32),
            jax.ShapeDtypeStruct((nb, WINDOW, SWA_KV), F32),
            jax.ShapeDtypeStruct((nb, WINDOW, SWA_KV), F32),
        ],
        scratch_shapes=[pltpu.VMEM((st, GLA_V), F32)],
        compiler_params=_params(("parallel",)),
        name="sample_mixer",
    )(sinks, y, y, y, y, glr, y, y, y, sq, y, y,
      w_lr_p, b_lr, gla_norm, conv_w, state_gla, state_conv, cache_k, cache_v)
    oa, ob, oc, sg, sc, kc, vc = res
    oc = oc[:, :, :SWA_GROUP, :].reshape(nb, SWA_Q)
    sc = sc.reshape(nb, CONV_K - 1, CONV_WIDTH)
    return oa, ob, oc, sg, sc, kc, vc


def _merge_kernel(*refs, n_ptiles):
    (pa, pb, pc, pg0, pg1, pg2, px, sa, sb, sc, sg0, sg1, sg2, sx, wb_ref, wo_ref,
     po_ref, so_ref) = refs

    def tile(a_ref, b_ref, c_ref, g0_ref, g1_ref, g2_ref, x_ref, o_ref):
        @pl.when(pl.program_id(1) == 0)
        def _():
            o_ref[...] = x_ref[...]

        def branch(src_ref, gate_ref, n):
            br = jnp.dot(src_ref[...].astype(BF16), wb_ref[n], preferred_element_type=F32)
            return jax.nn.sigmoid(gate_ref[...].astype(F32)) * br

        mixed = branch(a_ref, g0_ref, 0) + branch(b_ref, g1_ref, 1) + branch(c_ref, g2_ref, 2)
        o_ref[...] += jnp.dot(mixed.astype(BF16), wo_ref[...], preferred_element_type=F32)

    @pl.when(pl.program_id(0) < n_ptiles)
    def _():
        tile(pa, pb, pc, pg0, pg1, pg2, px, po_ref)

    @pl.when(pl.program_id(0) == n_ptiles)
    def _():
        tile(sa, sb, sc, sg0, sg1, sg2, sx, so_ref)


def _merge(prompt, sample, w_branch, w_out, layer, tm, tn):
    mp, ms = prompt[4].shape[0], sample[4].shape[0]
    n_ptiles = mp // tm
    prow = lambda i: jnp.minimum(i, n_ptiles - 1)
    gate_col = lambda n: (n * D_MODEL) // tn
    once = pl.Buffered(1)
    p_src = pl.BlockSpec((tm, BRANCH_WIDTH), lambda i, t: (prow(i), 0), pipeline_mode=once)
    p_gate = lambda n: pl.BlockSpec((tm, tn), lambda i, t: (prow(i), gate_col(n) + t))
    s_src = pl.BlockSpec((ms, BRANCH_WIDTH), lambda i, t: (0, 0))
    s_gate = lambda n: pl.BlockSpec((ms, tn), lambda i, t: (0, gate_col(n) + t))
    return pl.pallas_call(
        functools.partial(_merge_kernel, n_ptiles=n_ptiles),
        grid=(n_ptiles + 1, D_MODEL // tn),
        in_specs=[
            p_src, p_src, p_src, p_gate(0), p_gate(1), p_gate(2),
            pl.BlockSpec((tm, D_MODEL), lambda i, t: (prow(i), 0), pipeline_mode=once),
            s_src, s_src, s_src, s_gate(0), s_gate(1), s_gate(2),
            pl.BlockSpec((ms, D_MODEL), lambda i, t: (0, 0)),
            pl.BlockSpec((None, N_BRANCH, BRANCH_WIDTH, tn), lambda i, t: (layer, 0, 0, t)),
            pl.BlockSpec((None, tn, D_MODEL), lambda i, t: (layer, t, 0)),
        ],
        out_specs=[
            pl.BlockSpec((tm, D_MODEL), lambda i, t: (prow(i), 0)),
            pl.BlockSpec((ms, D_MODEL), lambda i, t: (0, 0)),
        ],
        out_shape=[
            jax.ShapeDtypeStruct((mp, D_MODEL), F32),
            jax.ShapeDtypeStruct((ms, D_MODEL), F32),
        ],
        compiler_params=_params(("arbitrary", "arbitrary")),
        name="merge_outproj",
    )(prompt[0], prompt[1], prompt[2], prompt[3], prompt[3], prompt[3], prompt[4],
      sample[0], sample[1], sample[2], sample[3], sample[3], sample[3], sample[4],
      w_branch, w_out)


def _mlp_kernel(xp_ref, xs_ref, g_ref, wu_ref, wd_ref, gnext_ref, wlr_ref, *rest,
                last, n_ptiles):
    if last:
        po, so, hp_scr, hs_scr = rest
        p_out, s_out = (po,), (so,)
    else:
        po, php, pglr, so, shp, sglr, hp_scr, hs_scr = rest
        p_out, s_out = (po, php, pglr), (so, shp, sglr)
    f = pl.program_id(1)

    def tile(x_ref, h_ref, o_ref, hn_ref=None, glr_ref=None):
        @pl.when(f == 0)
        def _():
            x = x_ref[...]
            h_ref[...] = _rmsnorm_rows(x, g_ref[...]).astype(BF16)
            o_ref[...] = x

        up = jnp.dot(h_ref[...], wu_ref[...], preferred_element_type=F32)
        act = jnp.square(jnp.maximum(up, 0.0)).astype(BF16)
        o_ref[...] += jnp.dot(act, wd_ref[...], preferred_element_type=F32)

        @pl.when(f == pl.num_programs(1) - 1)
        def _():
            xn = _rmsnorm_rows(o_ref[...], gnext_ref[...])
            if last:
                o_ref[...] = xn
            else:
                hn = xn.astype(BF16)
                hn_ref[...] = hn
                glr_ref[...] = _lr_project(hn, wlr_ref)

    @pl.when(pl.program_id(0) < n_ptiles)
    def _():
        tile(xp_ref, hp_scr, *p_out)

    @pl.when(pl.program_id(0) == n_ptiles)
    def _():
        tile(xs_ref, hs_scr, *s_out)


def _mlp(xp, xs, norm_w, w_up, w_down, next_norm_w, w_in_t, layer, tm, tf):
    mp, ms = xp.shape[0], xs.shape[0]
    n_ptiles = mp // tm
    last = layer == DEPTH - 1
    prow = lambda i: jnp.minimum(i, n_ptiles - 1)

    def group_out(m, rows, row_map):
        specs = [pl.BlockSpec((rows, D_MODEL), lambda i, f: (row_map(i), 0))]
        shapes = [jax.ShapeDtypeStruct((m, D_MODEL), F32)]
        if not last:
            specs += [pl.BlockSpec((rows, D_MODEL), lambda i, f: (row_map(i), 0)),
                      pl.BlockSpec((rows, LANES), lambda i, f: (row_map(i), 0))]
            shapes += [jax.ShapeDtypeStruct((m, D_MODEL), BF16),
                       jax.ShapeDtypeStruct((m, LANES), F32)]
        return specs, shapes

    p_specs, p_shapes = group_out(mp, tm, prow)
    s_specs, s_shapes = group_out(ms, ms, lambda i: 0)
    if last:
        next_norm_spec = pl.BlockSpec((1, D_MODEL), lambda i, f: (0, 0))
        lr_layer = layer
    else:
        next_norm_spec = pl.BlockSpec((None, 1, D_MODEL), lambda i, f: (layer + 1, 0, 0))
        lr_layer = layer + 1
    return pl.pallas_call(
        functools.partial(_mlp_kernel, last=last, n_ptiles=n_ptiles),
        grid=(n_ptiles + 1, D_FF // tf),
        in_specs=[
            pl.BlockSpec((tm, D_MODEL), lambda i, f: (prow(i), 0), pipeline_mode=pl.Buffered(1)),
            pl.BlockSpec((ms, D_MODEL), lambda i, f: (0, 0)),
            pl.BlockSpec((None, 1, D_MODEL), lambda i, f: (layer, 0, 0)),
            pl.BlockSpec((None, D_MODEL, tf), lambda i, f: (layer, 0, f)),
            pl.BlockSpec((None, tf, D_MODEL), lambda i, f: (layer, f, 0)),
            next_norm_spec,
            _lr_weight_spec(lr_layer),
        ],
        out_specs=p_specs + s_specs,
        out_shape=p_shapes + s_shapes,
        scratch_shapes=[pltpu.VMEM((tm, D_MODEL), BF16), pltpu.VMEM((ms, D_MODEL), BF16)],
        compiler_params=_params(("arbitrary", "arbitrary")),
        name="mlp",
    )(xp, xs, norm_w, w_up, w_down, next_norm_w, w_in_t)


def kernel(x_prompt, x_sample, state_gla, state_conv, cache_k, cache_v, w_in, w_lr, b_lr,
           gla_norm, conv_w, attn_sinks, w_branch, w_out, norm_mix, norm_mlp, w_up, w_down,
           norm_final):
    batch, seq, _ = x_prompt.shape
    nb = x_sample.shape[0]
    mp = batch * seq

    w_lr_p = jnp.pad(w_lr, ((0, 0), (0, LANES - GLA_RANK), (0, 0)))
    w_branch_b = w_branch.astype(BF16)
    w_out_b = w_out.astype(BF16)
    w_up_b = w_up.astype(BF16)
    w_down_b = w_down.astype(BF16)
    b_lr3 = b_lr.reshape(DEPTH, 1, GLA_QK)
    gla_norm3 = gla_norm.reshape(DEPTH, 1, GLA_DV)
    norm_mix3 = norm_mix.reshape(DEPTH, 1, D_MODEL)
    norm_mlp3 = norm_mlp.reshape(DEPTH, 1, D_MODEL)
    norm_final2 = norm_final.reshape(1, D_MODEL)
    state_conv2 = state_conv.reshape(DEPTH, nb, (CONV_K - 1) * CONV_WIDTH)
    cache_k4 = cache_k.reshape(DEPTH, nb, WINDOW, SWA_KV)
    cache_v4 = cache_v.reshape(DEPTH, nb, WINDOW, SWA_KV)

    w_in_t = jnp.swapaxes(w_in, 1, 2)
    tm = min(1024, mp)

    xp = x_prompt.reshape(mp, D_MODEL)
    xs = x_sample.reshape(nb, D_MODEL)
    hp, glrp = _norm(xp, norm_mix3, w_in_t, 0, tm=tm)
    hs, glrs = _norm(xs, norm_mix3, w_in_t, 0, tm=nb)
    outs = [[] for _ in range(8)]
    for l in range(DEPTH):
        next_norm = norm_final2 if l == DEPTH - 1 else norm_mix3
        yp, gp, ys, gs = _inproj(hp, hs, w_in_t, l, tm=min(512, mp), tn=1536)
        oa, sg_p = _gla_prompt(yp, glrp, w_lr_p, b_lr3, gla_norm3, l, batch, seq)
        ob, sc_p = _conv_prompt(yp, conv_w, l, batch, seq, tl=512)
        oc = _swa_prompt(yp, attn_sinks, l, batch, seq)
        yp3 = yp.reshape(batch, seq, COL_GATES)
        kp = yp3[:, seq - WINDOW:, COL_SK:COL_SK + SWA_KV]
        vp = yp3[:, seq - WINDOW:, COL_SV:COL_SV + SWA_KV]
        sa, sb, sc, sg_s, sc_s, ks, vs = _sample_mixer(
            ys, glrs, attn_sinks, w_lr_p, b_lr3, gla_norm3, conv_w,
            state_gla, state_conv2, cache_k4, cache_v4, l)
        xp, xs = _merge((oa, ob, oc, gp, xp), (sa, sb, sc, gs, xs), w_branch_b, w_out_b, l,
                        tm=tm, tn=512)
        res = _mlp(xp, xs, norm_mlp3, w_up_b, w_down_b, next_norm, w_in_t, l, tm=tm, tf=512)
        if l == DEPTH - 1:
            xp, xs = res
        else:
            xp, hp, glrp, xs, hs, glrs = res
        for lst, val in zip(outs, (
                sg_p, sg_s, sc_p, sc_s,
                kp.reshape(batch, WINDOW, SWA_KV_HEADS, SWA_HD),
                ks.reshape(nb, WINDOW, SWA_KV_HEADS, SWA_HD),
                vp.reshape(batch, WINDOW, SWA_KV_HEADS, SWA_HD),
                vs.reshape(nb, WINDOW, SWA_KV_HEADS, SWA_HD))):
            lst.append(val)
    y_prompt = xp.reshape(batch, seq, D_MODEL)
    y_sample = xs.reshape(nb, 1, D_MODEL)
    return (y_prompt, y_sample) + tuple(jnp.stack(o) for o in outs)
```

```python
import functools

import jax
import jax.numpy as jnp
from jax import lax
from jax.experimental import pallas as pl
from jax.experimental.pallas import tpu as pltpu

F32 = jnp.float32
BF16 = jnp.bfloat16

D_MODEL = 2048
DEPTH = 4
PAST_LEN = 16384
BRANCH_WIDTH = D_MODEL // 2
N_BRANCH = 3
GLA_HEADS = 4
GLA_DV = BRANCH_WIDTH // GLA_HEADS
GLA_DK = GLA_DV // 2
GLA_RANK = 16
GLA_TAU = 16.0
GLA_CHUNK = 64
CONV_WIDTH = BRANCH_WIDTH
CONV_K = 3
SWA_HD = 64
SWA_HEADS = BRANCH_WIDTH // SWA_HD
SWA_KV_HEADS = SWA_HEADS // 4
SWA_GROUP = SWA_HEADS // SWA_KV_HEADS
WINDOW = 128
D_FF = 4 * D_MODEL
EPS = 1e-6

LANES = 128
SUBLANES = 8
VMEM_LIMIT = 56 * 1024 * 1024

GLA_QK = GLA_HEADS * GLA_DK
GLA_V = GLA_HEADS * GLA_DV
SWA_Q = SWA_HEADS * SWA_HD
SWA_KV = SWA_KV_HEADS * SWA_HD

COL_Q = 0
COL_K = COL_Q + GLA_QK
COL_V = COL_K + GLA_QK
COL_GR = COL_V + GLA_V
COL_CB = COL_GR + GLA_V
COL_CC = COL_CB + CONV_WIDTH
COL_CH = COL_CC + CONV_WIDTH
COL_SQ = COL_CH + CONV_WIDTH
COL_SK = COL_SQ + SWA_Q
COL_SV = COL_SK + SWA_KV
COL_GATES = COL_SV + SWA_KV
PACKED_COLS = COL_GATES + N_BRANCH * D_MODEL
LR_SRC = COL_CB

ALIBI_SLOPES = tuple(2.0 ** (-8.0 * (h + 1) / SWA_HEADS) for h in range(SWA_HEADS))
SWA_SCALE = SWA_HD ** -0.5

NT_DIMS = (((1,), (1,)), ((), ()))
TN_DIMS = (((0,), (0,)), ((), ()))


def _params(sem):
    return pltpu.CompilerParams(dimension_semantics=sem, vmem_limit_bytes=VMEM_LIMIT)


def _rmsnorm_rows(x, g):
    ms = jnp.mean(x * x, axis=-1, keepdims=True)
    return x * lax.rsqrt(ms + EPS) * g


def _log_sigmoid(z):
    return jnp.minimum(z, 0.0) - jnp.log(1.0 + jnp.exp(-jnp.abs(z)))


def _forget_gate_log(glr_ref, wlr_ref, blr_ref):
    z = jnp.dot(glr_ref[...].astype(BF16), wlr_ref[...].astype(BF16),
                preferred_element_type=F32) + blr_ref[...]
    return _log_sigmoid(z) / GLA_TAU


def _split_bf16x3(x):
    hi = x.astype(BF16)
    r = x - hi.astype(F32)
    mid = r.astype(BF16)
    lo = (r - mid.astype(F32)).astype(BF16)
    return hi, mid, lo


def _silu(x):
    return x * jax.nn.sigmoid(x)


def _lr_weight_spec(layer):
    return pl.BlockSpec((None, LANES, D_MODEL), lambda *a: (layer, LR_SRC // LANES, 0))


def _lr_project(h, wlr_ref):
    return lax.dot_general(h, wlr_ref[...].astype(BF16), NT_DIMS, preferred_element_type=F32)


def _norm_kernel(x_ref, g_ref, wlr_ref, h_ref, glr_ref):
    h = _rmsnorm_rows(x_ref[...], g_ref[...]).astype(BF16)
    h_ref[...] = h
    glr_ref[...] = _lr_project(h, wlr_ref)


def _norm(x, norm_w, w_in_t, layer, tm):
    m = x.shape[0]
    return pl.pallas_call(
        _norm_kernel,
        grid=(m // tm,),
        in_specs=[
            pl.BlockSpec((tm, D_MODEL), lambda i: (i, 0)),
            pl.BlockSpec((None, 1, D_MODEL), lambda i: (layer, 0, 0)),
            _lr_weight_spec(layer),
        ],
        out_specs=[
            pl.BlockSpec((tm, D_MODEL), lambda i: (i, 0)),
            pl.BlockSpec((tm, LANES), lambda i: (i, 0)),
        ],
        out_shape=[
            jax.ShapeDtypeStruct((m, D_MODEL), BF16),
            jax.ShapeDtypeStruct((m, LANES), F32),
        ],
        compiler_params=_params(("parallel",)),
        name="norm",
    )(x, norm_w, w_in_t)


def _inproj_kernel(hp_ref, hs_ref, w_ref, yp_ref, gp_ref, ys_ref, gs_ref, wbf_ref, *,
                   n_main, n_ptiles):
    j = pl.program_id(0)
    i = pl.program_id(1)

    @pl.when(i == 0)
    def _():
        wbf_ref[...] = w_ref[...].astype(BF16)

    def project(h_ref):
        return lax.dot_general(h_ref[...], wbf_ref[...], NT_DIMS, preferred_element_type=F32)

    is_main = j < n_main
    is_prompt = i < n_ptiles

    @pl.when(jnp.logical_and(is_main, is_prompt))
    def _():
        yp_ref[...] = project(hp_ref)

    @pl.when(jnp.logical_and(jnp.logical_not(is_main), is_prompt))
    def _():
        gp_ref[...] = project(hp_ref).astype(BF16)

    @pl.when(jnp.logical_and(is_main, jnp.logical_not(is_prompt)))
    def _():
        ys_ref[...] = project(hs_ref)

    @pl.when(jnp.logical_and(jnp.logical_not(is_main), jnp.logical_not(is_prompt)))
    def _():
        gs_ref[...] = project(hs_ref).astype(BF16)


def _inproj(hp, hs, w_in_t, layer, tm, tn):
    mp, ms = hp.shape[0], hs.shape[0]
    n_ptiles = mp // tm
    n_main = COL_GATES // tn
    n_gate = (N_BRANCH * D_MODEL) // tn
    n_plain = LR_SRC // tn
    last_p = n_ptiles - 1
    prow = lambda i: jnp.minimum(i, last_p)
    w_row = lambda j: pl.multiple_of(j * tn + jnp.where(j >= n_plain, GLA_RANK, 0), GLA_RANK)
    return pl.pallas_call(
        functools.partial(_inproj_kernel, n_main=n_main, n_ptiles=n_ptiles),
        grid=(n_main + n_gate, n_ptiles + 1),
        in_specs=[
            pl.BlockSpec((tm, D_MODEL), lambda j, i: (prow(i), 0)),
            pl.BlockSpec((ms, D_MODEL), lambda j, i: (0, 0)),
            pl.BlockSpec((None, pl.Element(tn), pl.Element(D_MODEL)),
                         lambda j, i: (layer, w_row(j), 0)),
        ],
        out_specs=[
            pl.BlockSpec((tm, tn), lambda j, i: (jnp.where(j < n_main, prow(i), last_p),
                                                 jnp.minimum(j, n_main - 1))),
            pl.BlockSpec((tm, tn), lambda j, i: (jnp.where(j < n_main, 0, prow(i)),
                                                 jnp.maximum(j - n_main, 0))),
            pl.BlockSpec((ms, tn), lambda j, i: (0, jnp.minimum(j, n_main - 1))),
            pl.BlockSpec((ms, tn), lambda j, i: (0, jnp.maximum(j - n_main, 0))),
        ],
        out_shape=[
            jax.ShapeDtypeStruct((mp, COL_GATES), F32),
            jax.ShapeDtypeStruct((mp, N_BRANCH * D_MODEL), BF16),
            jax.ShapeDtypeStruct((ms, COL_GATES), F32),
            jax.ShapeDtypeStruct((ms, N_BRANCH * D_MODEL), BF16),
        ],
        scratch_shapes=[pltpu.VMEM((tn, D_MODEL), BF16)],
        compiler_params=_params(("arbitrary", "arbitrary")),
        name="inproj",
    )(hp, hs, w_in_t)


GLA_STEP_CHUNKS = 4


def _gla_kernel(q_ref, k_ref, v_ref, gr_ref, glr_ref, wlr_ref, blr_ref, gn_ref,
                oa_ref, sfin_ref, st_ref):
    c = pl.program_id(1)
    cs = GLA_CHUNK

    @pl.when(c == 0)
    def _():
        st_ref[...] = jnp.zeros_like(st_ref)

    log_a = _forget_gate_log(glr_ref, wlr_ref, blr_ref)
    tr = GLA_STEP_CHUNKS * cs
    row = lax.broadcasted_iota(jnp.int32, (tr, tr), 0)
    col = lax.broadcasted_iota(jnp.int32, (tr, tr), 1)
    log2_cs = cs.bit_length() - 1
    same_chunk = jnp.right_shift(row, log2_cs) == jnp.right_shift(col, log2_cs)
    causal = jnp.logical_and(same_chunk, row >= col)
    tri = jnp.where(causal, 1.0, 0.0).astype(BF16)
    gn = gn_ref[...]

    b = sum(jnp.dot(tri, part, preferred_element_type=F32) for part in _split_bf16x3(log_a))
    b_last = jnp.concatenate(
        [jnp.broadcast_to(b[(n + 1) * cs - 1:(n + 1) * cs, :], (cs, b.shape[1]))
         for n in range(GLA_STEP_CHUNKS)], axis=0)
    q = q_ref[...] * (GLA_DK ** -0.5)
    k = k_ref[...]
    qt = (q * jnp.exp(b)).astype(BF16)
    kt = (k * jnp.exp(-b)).astype(BF16)
    kd = (k * jnp.exp(b_last - b)).astype(BF16)
    g_last = jnp.exp(b_last)
    v = v_ref[...].astype(BF16)

    for h in range(GLA_HEADS):
        ks = slice(h * GLA_DK, (h + 1) * GLA_DK)
        vs = slice(h * GLA_DV, (h + 1) * GLA_DV)
        a = lax.dot_general(qt[:, ks], kt[:, ks], NT_DIMS, preferred_element_type=F32)
        a = jnp.where(causal, a, 0.0).astype(BF16)
        o_intra = jnp.dot(a, v[:, vs], preferred_element_type=F32)
        u_t = [lax.dot_general(v[n * cs:(n + 1) * cs, vs], kd[n * cs:(n + 1) * cs, ks], TN_DIMS,
                               preferred_element_type=F32) for n in range(GLA_STEP_CHUNKS)]
        s_t = st_ref[h]
        for n in range(GLA_STEP_CHUNKS):
            rs = slice(n * cs, (n + 1) * cs)
            o = lax.dot_general(qt[rs, ks], s_t.astype(BF16), NT_DIMS,
                                preferred_element_type=F32) + o_intra[rs, :]
            s_t = g_last[n * cs:n * cs + 1, ks] * s_t + u_t[n]
            oa_ref[rs, vs] = (_rmsnorm_rows(o, gn) * _silu(gr_ref[rs, vs])).astype(BF16)
        st_ref[h] = s_t

    @pl.when(c == pl.num_programs(1) - 1)
    def _():
        for h in range(GLA_HEADS):
            sfin_ref[0, h] = st_ref[h].T


def _gla_prompt(y, glr, w_lr_p, b_lr, gla_norm, layer, batch, seq):
    tr = GLA_STEP_CHUNKS * GLA_CHUNK
    ns = seq // tr
    rows = lambda b, c: b * ns + c
    return pl.pallas_call(
        _gla_kernel,
        grid=(batch, ns),
        in_specs=[
            pl.BlockSpec((tr, GLA_QK), lambda b, c: (rows(b, c), COL_Q // GLA_QK)),
            pl.BlockSpec((tr, GLA_QK), lambda b, c: (rows(b, c), COL_K // GLA_QK)),
            pl.BlockSpec((tr, GLA_V), lambda b, c: (rows(b, c), COL_V // GLA_V)),
            pl.BlockSpec((tr, GLA_V), lambda b, c: (rows(b, c), COL_GR // GLA_V)),
            pl.BlockSpec((tr, LANES), lambda b, c: (rows(b, c), 0)),
            pl.BlockSpec((None, LANES, GLA_QK), lambda b, c: (layer, 0, 0)),
            pl.BlockSpec((None, 1, GLA_QK), lambda b, c: (layer, 0, 0)),
            pl.BlockSpec((None, 1, GLA_DV), lambda b, c: (layer, 0, 0)),
        ],
        out_specs=[
            pl.BlockSpec((tr, GLA_V), lambda b, c: (rows(b, c), 0)),
            pl.BlockSpec((1, GLA_HEADS, GLA_DK, GLA_DV), lambda b, c: (b, 0, 0, 0)),
        ],
        out_shape=[
            jax.ShapeDtypeStruct((batch * seq, GLA_V), BF16),
            jax.ShapeDtypeStruct((batch, GLA_HEADS, GLA_DK, GLA_DV), F32),
        ],
        scratch_shapes=[pltpu.VMEM((GLA_HEADS, GLA_DV, GLA_DK), F32)],
        compiler_params=_params(("parallel", "arbitrary")),
        name="gla_prompt",
    )(y, y, y, y, glr, w_lr_p, b_lr, gla_norm)


def _conv_kernel(cb_ref, cc_ref, ch_ref, w_ref, ob_ref, cs_ref, prev_ref):
    @pl.when(pl.program_id(1) == 0)
    def _():
        prev_ref[...] = jnp.zeros_like(prev_ref)

    u = cc_ref[...] * ch_ref[...]
    tl = u.shape[0]
    prev = prev_ref[...]
    p_m1 = prev[SUBLANES - 1:SUBLANES, :]
    p_m2 = prev[SUBLANES - 2:SUBLANES - 1, :]
    row = lax.broadcasted_iota(jnp.int32, u.shape, 0)
    u1 = jnp.where(row == 0, p_m1, pltpu.roll(u, 1, 0))
    u2 = jnp.where(row == 0, p_m2, jnp.where(row == 1, p_m1, pltpu.roll(u, 2, 0)))
    w = w_ref[...]
    conv = w[0:1, :] * u2 + w[1:2, :] * u1 + w[2:3, :] * u
    ob_ref[...] = (cb_ref[...] * conv).astype(BF16)
    prev_ref[...] = u[tl - SUBLANES:tl, :]
    cs_ref[0] = u[tl - (CONV_K - 1):tl, :]


def _conv_prompt(y, conv_w, layer, batch, seq, tl):
    nt = seq // tl
    rows = lambda b, t: b * nt + t
    return pl.pallas_call(
        _conv_kernel,
        grid=(batch, nt),
        in_specs=[
            pl.BlockSpec((tl, CONV_WIDTH), lambda b, t: (rows(b, t), COL_CB // CONV_WIDTH)),
            pl.BlockSpec((tl, CONV_WIDTH), lambda b, t: (rows(b, t), COL_CC // CONV_WIDTH)),
            pl.BlockSpec((tl, CONV_WIDTH), lambda b, t: (rows(b, t), COL_CH // CONV_WIDTH)),
            pl.BlockSpec((None, CONV_K, CONV_WIDTH), lambda b, t: (layer, 0, 0)),
        ],
        out_specs=[
            pl.BlockSpec((tl, CONV_WIDTH), lambda b, t: (rows(b, t), 0)),
            pl.BlockSpec((1, CONV_K - 1, CONV_WIDTH), lambda b, t: (b, 0, 0)),
        ],
        out_shape=[
            jax.ShapeDtypeStruct((batch * seq, CONV_WIDTH), BF16),
            jax.ShapeDtypeStruct((batch, CONV_K - 1, CONV_WIDTH), F32),
        ],
        scratch_shapes=[pltpu.VMEM((SUBLANES, CONV_WIDTH), F32)],
        compiler_params=_params(("parallel", "arbitrary")),
        name="conv_prompt",
    )(y, y, y, conv_w)


def _swa_kernel(sinks_ref, q_ref, kc_ref, kp_ref, vc_ref, vp_ref, o_ref, bias_ref, *, layer):
    blk = pl.program_id(1)

    key = lax.broadcasted_iota(jnp.int32, (2 * WINDOW, WINDOW), 0)

    @pl.when(blk == 0)
    def _():
        qry = lax.broadcasted_iota(jnp.int32, (2 * WINDOW, WINDOW), 1)
        dist = qry - key + WINDOW
        valid = jnp.logical_and(dist >= 0, dist < WINDOW)
        dist_f = dist.astype(F32)
        for h in range(SWA_HEADS):
            bias_ref[h] = jnp.where(valid, -ALIBI_SLOPES[h] * dist_f, -jnp.inf)

    key_exists = jnp.logical_or(blk > 0, key >= WINDOW)
    q = (q_ref[...] * SWA_SCALE).astype(BF16)
    k2 = jnp.concatenate([kp_ref[...], kc_ref[...]], axis=0).astype(BF16)
    v2_t = jnp.concatenate([vp_ref[...], vc_ref[...]], axis=0).T.astype(BF16)
    scores = []
    for h in range(SWA_HEADS):
        j = h // SWA_GROUP
        scores.append(lax.dot_general(k2[:, j * SWA_HD:(j + 1) * SWA_HD],
                                      q[:, h * SWA_HD:(h + 1) * SWA_HD], NT_DIMS,
                                      preferred_element_type=F32))
    probs = []
    for h in range(SWA_HEADS):
        s = jnp.where(key_exists, scores[h] + bias_ref[h], -jnp.inf)
        sink = sinks_ref[layer, h]
        m = jnp.maximum(jnp.max(s, axis=0, keepdims=True), sink)
        e = jnp.exp(s - m)
        den = jnp.sum(e, axis=0, keepdims=True) + jnp.exp(sink - m)
        probs.append((e * (1.0 / den)).astype(BF16))
    outs_t = []
    for h in range(SWA_HEADS):
        j = h // SWA_GROUP
        outs_t.append(jnp.dot(v2_t[j * SWA_HD:(j + 1) * SWA_HD, :], probs[h],
                              preferred_element_type=F32))
    o_ref[...] = jnp.concatenate(outs_t, axis=0).T.astype(BF16)


def _swa_prompt(y, sinks, layer, batch, seq):
    nb = seq // WINDOW
    rows = lambda b, i: b * nb + i
    prev_rows = lambda b, i: b * nb + jnp.maximum(i - 1, 0)
    return pl.pallas_call(
        functools.partial(_swa_kernel, layer=layer),
        grid=(batch, nb),
        in_specs=[
            pl.BlockSpec(memory_space=pltpu.SMEM),
            pl.BlockSpec((WINDOW, SWA_Q), lambda b, i: (rows(b, i), COL_SQ // SWA_Q)),
            pl.BlockSpec((WINDOW, SWA_KV), lambda b, i: (rows(b, i), COL_SK // SWA_KV)),
            pl.BlockSpec((WINDOW, SWA_KV), lambda b, i: (prev_rows(b, i), COL_SK // SWA_KV)),
            pl.BlockSpec((WINDOW, SWA_KV), lambda b, i: (rows(b, i), COL_SV // SWA_KV)),
            pl.BlockSpec((WINDOW, SWA_KV), lambda b, i: (prev_rows(b, i), COL_SV // SWA_KV)),
        ],
        out_specs=pl.BlockSpec((WINDOW, SWA_Q), lambda b, i: (rows(b, i), 0)),
        out_shape=jax.ShapeDtypeStruct((batch * seq, SWA_Q), BF16),
        scratch_shapes=[pltpu.VMEM((SWA_HEADS, 2 * WINDOW, WINDOW), F32)],
        compiler_params=_params(("parallel", "arbitrary")),
        name="swa_prompt",
    )(sinks, y, y, y, y, y)


SAMPLE_STEP = SUBLANES


def _sample_kernel(sinks_ref, q_ref, k_ref, v_ref, gr_ref, glr_ref, cb_ref, cc_ref, ch_ref,
                   sq_ref, sk_ref, sv_ref, wlr_ref, blr_ref, gn_ref, cw_ref,
                   s_ref, cs_ref, kc_ref, vc_ref, *rest, layer, n_carried):
    oa_ref, ob_ref, oc_ref, so_ref, cso_ref, kco_ref, vco_ref, o_scr = rest[n_carried:]
    nb = SAMPLE_STEP
    b = _forget_gate_log(glr_ref, wlr_ref, blr_ref)
    q = q_ref[...] * (GLA_DK ** -0.5)
    k = k_ref[...]
    v = v_ref[...]
    qt = q * jnp.exp(b)
    kt = k * jnp.exp(-b)
    kd = k * jnp.exp(b - b)
    g_last = jnp.exp(b)
    qt_t = qt.T
    kd_t = kd.T
    gl_t = g_last.T
    for h in range(GLA_HEADS):
        ks = slice(h * GLA_DK, (h + 1) * GLA_DK)
        vs = slice(h * GLA_DV, (h + 1) * GLA_DV)
        a = jnp.sum(qt[:, ks] * kt[:, ks], axis=-1, keepdims=True)
        o_intra = a * v[:, vs]
        for i in range(nb):
            s_in = s_ref[i, h]
            v_row = v[i:i + 1, vs]
            o = jnp.sum(qt_t[ks, i:i + 1] * s_in, axis=0, keepdims=True)
            o_scr[i:i + 1, vs] = o + o_intra[i:i + 1, :]
            so_ref[i, h] = gl_t[ks, i:i + 1] * s_in + kd_t[ks, i:i + 1] * v_row
    gn = gn_ref[...]
    for h in range(GLA_HEADS):
        vs = slice(h * GLA_DV, (h + 1) * GLA_DV)
        oa_ref[:, vs] = _rmsnorm_rows(o_scr[:, vs], gn) * _silu(gr_ref[:, vs])

    u_new = cc_ref[...] * ch_ref[...]
    w = cw_ref[...]
    p0 = cs_ref[:, :CONV_WIDTH]
    p1 = cs_ref[:, CONV_WIDTH:]
    conv = w[0:1, :] * p0 + w[1:2, :] * p1 + w[2:3, :] * u_new
    ob_ref[...] = cb_ref[...] * conv
    cso_ref[:, :CONV_WIDTH] = p1
    cso_ref[:, CONV_WIDTH:] = u_new

    rowk = lax.broadcasted_iota(jnp.int32, (WINDOW, SWA_KV), 0)
    key = lax.broadcasted_iota(jnp.int32, (SUBLANES, WINDOW), 1)
    grp = lax.broadcasted_iota(jnp.int32, (SUBLANES, WINDOW), 0)
    dist = (WINDOW - 1 - key).astype(F32)
    bias, sink = [], []
    for j in range(SWA_KV_HEADS):
        bj = jnp.zeros((SUBLANES, WINDOW), F32)
        sj = jnp.zeros((SUBLANES, 1), F32)
        for g in range(SWA_GROUP):
            bj = jnp.where(grp == g, -ALIBI_SLOPES[j * SWA_GROUP + g] * dist, bj)
            sj = jnp.where(grp[:, 0:1] == g, sinks_ref[layer, j * SWA_GROUP + g], sj)
        bias.append(bj)
        sink.append(sj)
    k_new, v_new = [], []
    for i in range(nb):
        kn = jnp.where(rowk == WINDOW - 1, sk_ref[i:i + 1, :], pltpu.roll(kc_ref[i], WINDOW - 1, 0))
        vn = jnp.where(rowk == WINDOW - 1, sv_ref[i:i + 1, :], pltpu.roll(vc_ref[i], WINDOW - 1, 0))
        kco_ref[i] = kn
        vco_ref[i] = vn
        k_new.append(kn.astype(BF16))
        v_new.append(vn.astype(BF16))
    scores = []
    for i in range(nb):
        for j in range(SWA_KV_HEADS):
            qj = (sq_ref[i, j] * SWA_SCALE).astype(BF16)
            kj = k_new[i][:, j * SWA_HD:(j + 1) * SWA_HD]
            scores.append(lax.dot_general(qj, kj, NT_DIMS, preferred_element_type=F32))
    probs = []
    for i in range(nb):
        for j in range(SWA_KV_HEADS):
            s = scores[i * SWA_KV_HEADS + j] + bias[j]
            m = jnp.maximum(jnp.max(s, axis=-1, keepdims=True), sink[j])
            e = jnp.exp(s - m)
            den = jnp.sum(e, axis=-1, keepdims=True) + jnp.exp(sink[j] - m)
            probs.append((e * (1.0 / den)).astype(BF16))
    for i in range(nb):
        for j in range(SWA_KV_HEADS):
            vj = v_new[i][:, j * SWA_HD:(j + 1) * SWA_HD]
            oc_ref[i, j] = jnp.dot(probs[i * SWA_KV_HEADS + j], vj, preferred_element_type=F32)


def _sample_mixer(y, glr, sinks, w_lr_p, b_lr, gla_norm, conv_w,
                  state_gla, state_conv, cache_k, cache_v, carried, layer):
    nb = y.shape[0]
    st = SAMPLE_STEP
    sq = y[:, COL_SQ:COL_SQ + SWA_Q].reshape(nb, SWA_KV_HEADS, SWA_GROUP, SWA_HD)
    sq = jnp.pad(sq, ((0, 0), (0, 0), (0, SUBLANES - SWA_GROUP), (0, 0)))

    def seg(width, col):
        return pl.BlockSpec((st, width), lambda b: (b, col // width))

    def per_layer(shape):
        return pl.BlockSpec((None,) + shape, lambda b: (layer,) + (0,) * len(shape))

    def state(shape):
        return pl.BlockSpec((None, st) + shape, lambda b: (layer, b) + (0,) * len(shape))

    def out(shape):
        return pl.BlockSpec((st,) + shape, lambda b: (b,) + (0,) * len(shape))

    state_shapes = ((GLA_HEADS, GLA_DK, GLA_DV), ((CONV_K - 1) * CONV_WIDTH,),
                    (WINDOW, SWA_KV), (WINDOW, SWA_KV))
    n_fixed_inputs = 20
    res = pl.pallas_call(
        functools.partial(_sample_kernel, layer=layer, n_carried=len(carried)),
        grid=(nb // st,),
        in_specs=[
            pl.BlockSpec(memory_space=pltpu.SMEM),
            seg(GLA_QK, COL_Q), seg(GLA_QK, COL_K), seg(GLA_V, COL_V), seg(GLA_V, COL_GR),
            pl.BlockSpec((st, LANES), lambda b: (b, 0)),
            seg(CONV_WIDTH, COL_CB), seg(CONV_WIDTH, COL_CC), seg(CONV_WIDTH, COL_CH),
            out((SWA_KV_HEADS, SUBLANES, SWA_HD)), seg(SWA_KV, COL_SK), seg(SWA_KV, COL_SV),
            per_layer((LANES, GLA_QK)), per_layer((1, GLA_QK)), per_layer((1, GLA_DV)),
            per_layer((CONV_K, CONV_WIDTH)),
        ] + [state(s) for s in state_shapes] + [pl.BlockSpec(memory_space=pl.ANY)] * len(carried),
        out_specs=[
            out((GLA_V,)), out((CONV_WIDTH,)), out((SWA_KV_HEADS, SUBLANES, SWA_HD)),
        ] + [state(s) for s in state_shapes],
        out_shape=[
            jax.ShapeDtypeStruct((nb, GLA_V), F32),
            jax.ShapeDtypeStruct((nb, CONV_WIDTH), F32),
            jax.ShapeDtypeStruct((nb, SWA_KV_HEADS, SUBLANES, SWA_HD), F32),
        ] + [jax.ShapeDtypeStruct((DEPTH, nb) + s, F32) for s in state_shapes],
        input_output_aliases={n_fixed_inputs + n: 3 + n for n in range(len(carried))},
        scratch_shapes=[pltpu.VMEM((st, GLA_V), F32)],
        compiler_params=_params(("parallel",)),
        name="sample_mixer",
    )(sinks, y, y, y, y, glr, y, y, y, sq, y, y,
      w_lr_p, b_lr, gla_norm, conv_w, state_gla, state_conv, cache_k, cache_v, *carried)
    oa, ob, oc = res[:3]
    oc = oc[:, :, :SWA_GROUP, :].reshape(nb, SWA_Q)
    return oa, ob, oc, tuple(res[3:])


def _merge_kernel(*refs, n_ptiles):
    (pa, pb, pc, pg0, pg1, pg2, px, sa, sb, sc, sg0, sg1, sg2, sx, wb_ref, wo_ref,
     po_ref, so_ref) = refs

    def tile(a_ref, b_ref, c_ref, g0_ref, g1_ref, g2_ref, x_ref, o_ref):
        @pl.when(pl.program_id(1) == 0)
        def _():
            o_ref[...] = x_ref[...]

        def branch(src_ref, gate_ref, n):
            br = jnp.dot(src_ref[...].astype(BF16), wb_ref[n], preferred_element_type=F32)
            return jax.nn.sigmoid(gate_ref[...].astype(F32)) * br

        mixed = branch(a_ref, g0_ref, 0) + branch(b_ref, g1_ref, 1) + branch(c_ref, g2_ref, 2)
        o_ref[...] += jnp.dot(mixed.astype(BF16), wo_ref[...], preferred_element_type=F32)

    @pl.when(pl.program_id(0) < n_ptiles)
    def _():
        tile(pa, pb, pc, pg0, pg1, pg2, px, po_ref)

    @pl.when(pl.program_id(0) == n_ptiles)
    def _():
        tile(sa, sb, sc, sg0, sg1, sg2, sx, so_ref)


def _merge(prompt, sample, w_branch, w_out, layer, tm, tn):
    mp, ms = prompt[4].shape[0], sample[4].shape[0]
    n_ptiles = mp // tm
    prow = lambda i: jnp.minimum(i, n_ptiles - 1)
    gate_col = lambda n: (n * D_MODEL) // tn
    once = pl.Buffered(1)
    p_src = pl.BlockSpec((tm, BRANCH_WIDTH), lambda i, t: (prow(i), 0), pipeline_mode=once)
    p_gate = lambda n: pl.BlockSpec((tm, tn), lambda i, t: (prow(i), gate_col(n) + t))
    s_src = pl.BlockSpec((ms, BRANCH_WIDTH), lambda i, t: (0, 0))
    s_gate = lambda n: pl.BlockSpec((ms, tn), lambda i, t: (0, gate_col(n) + t))
    return pl.pallas_call(
        functools.partial(_merge_kernel, n_ptiles=n_ptiles),
        grid=(n_ptiles + 1, D_MODEL // tn),
        in_specs=[
            p_src, p_src, p_src, p_gate(0), p_gate(1), p_gate(2),
            pl.BlockSpec((tm, D_MODEL), lambda i, t: (prow(i), 0), pipeline_mode=once),
            s_src, s_src, s_src, s_gate(0), s_gate(1), s_gate(2),
            pl.BlockSpec((ms, D_MODEL), lambda i, t: (0, 0)),
            pl.BlockSpec((None, N_BRANCH, BRANCH_WIDTH, tn), lambda i, t: (layer, 0, 0, t)),
            pl.BlockSpec((None, tn, D_MODEL), lambda i, t: (layer, t, 0)),
        ],
        out_specs=[
            pl.BlockSpec((tm, D_MODEL), lambda i, t: (prow(i), 0)),
            pl.BlockSpec((ms, D_MODEL), lambda i, t: (0, 0)),
        ],
        out_shape=[
            jax.ShapeDtypeStruct((mp, D_MODEL), F32),
            jax.ShapeDtypeStruct((ms, D_MODEL), F32),
        ],
        compiler_params=_params(("arbitrary", "arbitrary")),
        name="merge_outproj",
    )(prompt[0], prompt[1], prompt[2], prompt[3], prompt[3], prompt[3], prompt[4],
      sample[0], sample[1], sample[2], sample[3], sample[3], sample[3], sample[4],
      w_branch, w_out)


def _mlp_kernel(xp_ref, xs_ref, g_ref, wu_ref, wd_ref, gnext_ref, wlr_ref, *rest,
                last, n_ptiles):
    if last:
        po, so, hp_scr, hs_scr = rest
        p_out, s_out = (po,), (so,)
    else:
        po, php, pglr, so, shp, sglr, hp_scr, hs_scr = rest
        p_out, s_out = (po, php, pglr), (so, shp, sglr)
    f = pl.program_id(1)

    def tile(x_ref, h_ref, o_ref, hn_ref=None, glr_ref=None):
        @pl.when(f == 0)
        def _():
            x = x_ref[...]
            h_ref[...] = _rmsnorm_rows(x, g_ref[...]).astype(BF16)
            o_ref[...] = x

        up = jnp.dot(h_ref[...], wu_ref[...], preferred_element_type=F32)
        act = jnp.square(jnp.maximum(up, 0.0)).astype(BF16)
        o_ref[...] += jnp.dot(act, wd_ref[...], preferred_element_type=F32)

        @pl.when(f == pl.num_programs(1) - 1)
        def _():
            xn = _rmsnorm_rows(o_ref[...], gnext_ref[...])
            if last:
                o_ref[...] = xn
            else:
                hn = xn.astype(BF16)
                hn_ref[...] = hn
                glr_ref[...] = _lr_project(hn, wlr_ref)

    @pl.when(pl.program_id(0) < n_ptiles)
    def _():
        tile(xp_ref, hp_scr, *p_out)

    @pl.when(pl.program_id(0) == n_ptiles)
    def _():
        tile(xs_ref, hs_scr, *s_out)


def _mlp(xp, xs, norm_w, w_up, w_down, next_norm_w, w_in_t, layer, tm, tf):
    mp, ms = xp.shape[0], xs.shape[0]
    n_ptiles = mp // tm
    last = layer == DEPTH - 1
    prow = lambda i: jnp.minimum(i, n_ptiles - 1)

    def group_out(m, rows, row_map):
        specs = [pl.BlockSpec((rows, D_MODEL), lambda i, f: (row_map(i), 0))]
        shapes = [jax.ShapeDtypeStruct((m, D_MODEL), F32)]
        if not last:
            specs += [pl.BlockSpec((rows, D_MODEL), lambda i, f: (row_map(i), 0)),
                      pl.BlockSpec((rows, LANES), lambda i, f: (row_map(i), 0))]
            shapes += [jax.ShapeDtypeStruct((m, D_MODEL), BF16),
                       jax.ShapeDtypeStruct((m, LANES), F32)]
        return specs, shapes

    p_specs, p_shapes = group_out(mp, tm, prow)
    s_specs, s_shapes = group_out(ms, ms, lambda i: 0)
    if last:
        next_norm_spec = pl.BlockSpec((1, D_MODEL), lambda i, f: (0, 0))
        lr_layer = layer
    else:
        next_norm_spec = pl.BlockSpec((None, 1, D_MODEL), lambda i, f: (layer + 1, 0, 0))
        lr_layer = layer + 1
    return pl.pallas_call(
        functools.partial(_mlp_kernel, last=last, n_ptiles=n_ptiles),
        grid=(n_ptiles + 1, D_FF // tf),
        in_specs=[
            pl.BlockSpec((tm, D_MODEL), lambda i, f: (prow(i), 0), pipeline_mode=pl.Buffered(1)),
            pl.BlockSpec((ms, D_MODEL), lambda i, f: (0, 0)),
            pl.BlockSpec((None, 1, D_MODEL), lambda i, f: (layer, 0, 0)),
            pl.BlockSpec((None, D_MODEL, tf), lambda i, f: (layer, 0, f)),
            pl.BlockSpec((None, tf, D_MODEL), lambda i, f: (layer, f, 0)),
            next_norm_spec,
            _lr_weight_spec(lr_layer),
        ],
        out_specs=p_specs + s_specs,
        out_shape=p_shapes + s_shapes,
        scratch_shapes=[pltpu.VMEM((tm, D_MODEL), BF16), pltpu.VMEM((ms, D_MODEL), BF16)],
        compiler_params=_params(("arbitrary", "arbitrary")),
        name="mlp",
    )(xp, xs, norm_w, w_up, w_down, next_norm_w, w_in_t)


def kernel(x_prompt, x_sample, state_gla, state_conv, cache_k, cache_v, w_in, w_lr, b_lr,
           gla_norm, conv_w, attn_sinks, w_branch, w_out, norm_mix, norm_mlp, w_up, w_down,
           norm_final):
    batch, seq, _ = x_prompt.shape
    nb = x_sample.shape[0]
    mp = batch * seq

    w_lr_p = jnp.pad(w_lr, ((0, 0), (0, LANES - GLA_RANK), (0, 0)))
    w_branch_b = w_branch.astype(BF16)
    w_out_b = w_out.astype(BF16)
    w_up_b = w_up.astype(BF16)
    w_down_b = w_down.astype(BF16)
    b_lr3 = b_lr.reshape(DEPTH, 1, GLA_QK)
    gla_norm3 = gla_norm.reshape(DEPTH, 1, GLA_DV)
    norm_mix3 = norm_mix.reshape(DEPTH, 1, D_MODEL)
    norm_mlp3 = norm_mlp.reshape(DEPTH, 1, D_MODEL)
    norm_final2 = norm_final.reshape(1, D_MODEL)
    state_conv2 = state_conv.reshape(DEPTH, nb, (CONV_K - 1) * CONV_WIDTH)
    cache_k4 = cache_k.reshape(DEPTH, nb, WINDOW, SWA_KV)
    cache_v4 = cache_v.reshape(DEPTH, nb, WINDOW, SWA_KV)

    w_in_t = jnp.swapaxes(w_in, 1, 2)
    tm = min(1024, mp)

    xp = x_prompt.reshape(mp, D_MODEL)
    xs = x_sample.reshape(nb, D_MODEL)
    hp, glrp = _norm(xp, norm_mix3, w_in_t, 0, tm=tm)
    hs, glrs = _norm(xs, norm_mix3, w_in_t, 0, tm=nb)
    outs = [[] for _ in range(4)]
    sample_states = ()
    for l in range(DEPTH):
        next_norm = norm_final2 if l == DEPTH - 1 else norm_mix3
        yp, gp, ys, gs = _inproj(hp, hs, w_in_t, l, tm=min(512, mp), tn=1536)
        oa, sg_p = _gla_prompt(yp, glrp, w_lr_p, b_lr3, gla_norm3, l, batch, seq)
        ob, sc_p = _conv_prompt(yp, conv_w, l, batch, seq, tl=512)
        oc = _swa_prompt(yp, attn_sinks, l, batch, seq)
        yp3 = yp.reshape(batch, seq, COL_GATES)
        kp = yp3[:, seq - WINDOW:, COL_SK:COL_SK + SWA_KV]
        vp = yp3[:, seq - WINDOW:, COL_SV:COL_SV + SWA_KV]
        sa, sb, sc, sample_states = _sample_mixer(
            ys, glrs, attn_sinks, w_lr_p, b_lr3, gla_norm3, conv_w,
            state_gla, state_conv2, cache_k4, cache_v4, sample_states, l)
        xp, xs = _merge((oa, ob, oc, gp, xp), (sa, sb, sc, gs, xs), w_branch_b, w_out_b, l,
                        tm=tm, tn=512)
        res = _mlp(xp, xs, norm_mlp3, w_up_b, w_down_b, next_norm, w_in_t, l, tm=tm, tf=512)
        if l == DEPTH - 1:
            xp, xs = res
        else:
            xp, hp, glrp, xs, hs, glrs = res
        for lst, val in zip(outs, (
                sg_p, sc_p,
                kp.reshape(batch, WINDOW, SWA_KV_HEADS, SWA_HD),
                vp.reshape(batch, WINDOW, SWA_KV_HEADS, SWA_HD))):
            lst.append(val)
    y_prompt = xp.reshape(batch, seq, D_MODEL)
    y_sample = xs.reshape(nb, 1, D_MODEL)
    sg_p, sc_p, kp, vp = (jnp.stack(o) for o in outs)
    sg_s, sc_s, ks, vs = sample_states
    return (y_prompt, y_sample, sg_p, sg_s, sc_p,
            sc_s.reshape(DEPTH, nb, CONV_K - 1, CONV_WIDTH), kp,
            ks.reshape(DEPTH, nb, WINDOW, SWA_KV_HEADS, SWA_HD), vp,
            vs.reshape(DEPTH, nb, WINDOW, SWA_KV_HEADS, SWA_HD))
```

```python
import functools

import jax
import jax.numpy as jnp
from jax import lax
from jax.experimental import pallas as pl
from jax.experimental.pallas import tpu as pltpu

F32 = jnp.float32
BF16 = jnp.bfloat16

D_MODEL = 2048
DEPTH = 4
PAST_LEN = 16384
BRANCH_WIDTH = D_MODEL // 2
N_BRANCH = 3
GLA_HEADS = 4
GLA_DV = BRANCH_WIDTH // GLA_HEADS
GLA_DK = GLA_DV // 2
GLA_RANK = 16
GLA_TAU = 16.0
GLA_CHUNK = 64
CONV_WIDTH = BRANCH_WIDTH
CONV_K = 3
SWA_HD = 64
SWA_HEADS = BRANCH_WIDTH // SWA_HD
SWA_KV_HEADS = SWA_HEADS // 4
SWA_GROUP = SWA_HEADS // SWA_KV_HEADS
WINDOW = 128
D_FF = 4 * D_MODEL
EPS = 1e-6

LANES = 128
SUBLANES = 8
VMEM_LIMIT = 56 * 1024 * 1024

GLA_QK = GLA_HEADS * GLA_DK
GLA_V = GLA_HEADS * GLA_DV
SWA_Q = SWA_HEADS * SWA_HD
SWA_KV = SWA_KV_HEADS * SWA_HD

COL_Q = 0
COL_K = COL_Q + GLA_QK
COL_V = COL_K + GLA_QK
COL_GR = COL_V + GLA_V
COL_CB = COL_GR + GLA_V
COL_CC = COL_CB + CONV_WIDTH
COL_CH = COL_CC + CONV_WIDTH
COL_SQ = COL_CH + CONV_WIDTH
COL_SK = COL_SQ + SWA_Q
COL_SV = COL_SK + SWA_KV
COL_GATES = COL_SV + SWA_KV
PACKED_COLS = COL_GATES + N_BRANCH * D_MODEL
LR_SRC = COL_CB

ALIBI_SLOPES = tuple(2.0 ** (-8.0 * (h + 1) / SWA_HEADS) for h in range(SWA_HEADS))
SWA_SCALE = SWA_HD ** -0.5

NT_DIMS = (((1,), (1,)), ((), ()))
TN_DIMS = (((0,), (0,)), ((), ()))


def _params(sem):
    return pltpu.CompilerParams(dimension_semantics=sem, vmem_limit_bytes=VMEM_LIMIT)


def _rmsnorm_rows(x, g):
    ms = jnp.mean(x * x, axis=-1, keepdims=True)
    return x * lax.rsqrt(ms + EPS) * g


def _log_sigmoid(z):
    return jnp.minimum(z, 0.0) - jnp.log(1.0 + jnp.exp(-jnp.abs(z)))


def _forget_gate_log(glr_ref, wlr_ref, blr_ref):
    z = jnp.dot(glr_ref[...].astype(BF16), wlr_ref[...].astype(BF16),
                preferred_element_type=F32) + blr_ref[...]
    return _log_sigmoid(z) / GLA_TAU


def _split_bf16x3(x):
    hi = x.astype(BF16)
    r = x - hi.astype(F32)
    mid = r.astype(BF16)
    lo = (r - mid.astype(F32)).astype(BF16)
    return hi, mid, lo


def _silu(x):
    return x * jax.nn.sigmoid(x)


def _lr_weight_spec(layer):
    return pl.BlockSpec((None, LANES, D_MODEL), lambda *a: (layer, LR_SRC // LANES, 0))


def _lr_project(h, wlr_ref):
    return lax.dot_general(h, wlr_ref[...].astype(BF16), NT_DIMS, preferred_element_type=F32)


def _norm_kernel(x_ref, g_ref, wlr_ref, h_ref, glr_ref):
    h = _rmsnorm_rows(x_ref[...], g_ref[...]).astype(BF16)
    h_ref[...] = h
    glr_ref[...] = _lr_project(h, wlr_ref)


def _norm(x, norm_w, w_in_t, layer, tm):
    m = x.shape[0]
    return pl.pallas_call(
        _norm_kernel,
        grid=(m // tm,),
        in_specs=[
            pl.BlockSpec((tm, D_MODEL), lambda i: (i, 0)),
            pl.BlockSpec((None, 1, D_MODEL), lambda i: (layer, 0, 0)),
            _lr_weight_spec(layer),
        ],
        out_specs=[
            pl.BlockSpec((tm, D_MODEL), lambda i: (i, 0)),
            pl.BlockSpec((tm, LANES), lambda i: (i, 0)),
        ],
        out_shape=[
            jax.ShapeDtypeStruct((m, D_MODEL), BF16),
            jax.ShapeDtypeStruct((m, LANES), F32),
        ],
        compiler_params=_params(("parallel",)),
        name="norm",
    )(x, norm_w, w_in_t)


def _inproj_kernel(hp_ref, hs_ref, w_ref, yp_ref, gp_ref, ys_ref, gs_ref, wbf_ref, *,
                   n_main, n_ptiles):
    j = pl.program_id(0)
    i = pl.program_id(1)

    @pl.when(i == 0)
    def _():
        wbf_ref[...] = w_ref[...].astype(BF16)

    def project(h_ref):
        return lax.dot_general(h_ref[...], wbf_ref[...], NT_DIMS, preferred_element_type=F32)

    is_main = j < n_main
    is_prompt = i < n_ptiles

    @pl.when(jnp.logical_and(is_main, is_prompt))
    def _():
        yp_ref[...] = project(hp_ref)

    @pl.when(jnp.logical_and(jnp.logical_not(is_main), is_prompt))
    def _():
        gp_ref[...] = project(hp_ref).astype(BF16)

    @pl.when(jnp.logical_and(is_main, jnp.logical_not(is_prompt)))
    def _():
        ys_ref[...] = project(hs_ref)

    @pl.when(jnp.logical_and(jnp.logical_not(is_main), jnp.logical_not(is_prompt)))
    def _():
        gs_ref[...] = project(hs_ref).astype(BF16)


def _inproj(hp, hs, w_in_t, layer, tm, tn):
    mp, ms = hp.shape[0], hs.shape[0]
    n_ptiles = mp // tm
    n_main = COL_GATES // tn
    n_gate = (N_BRANCH * D_MODEL) // tn
    n_plain = LR_SRC // tn
    last_p = n_ptiles - 1
    prow = lambda i: jnp.minimum(i, last_p)
    w_row = lambda j: pl.multiple_of(j * tn + jnp.where(j >= n_plain, GLA_RANK, 0), GLA_RANK)
    return pl.pallas_call(
        functools.partial(_inproj_kernel, n_main=n_main, n_ptiles=n_ptiles),
        grid=(n_main + n_gate, n_ptiles + 1),
        in_specs=[
            pl.BlockSpec((tm, D_MODEL), lambda j, i: (prow(i), 0)),
            pl.BlockSpec((ms, D_MODEL), lambda j, i: (0, 0)),
            pl.BlockSpec((None, pl.Element(tn), pl.Element(D_MODEL)),
                         lambda j, i: (layer, w_row(j), 0)),
        ],
        out_specs=[
            pl.BlockSpec((tm, tn), lambda j, i: (jnp.where(j < n_main, prow(i), last_p),
                                                 jnp.minimum(j, n_main - 1))),
            pl.BlockSpec((tm, tn), lambda j, i: (jnp.where(j < n_main, 0, prow(i)),
                                                 jnp.maximum(j - n_main, 0))),
            pl.BlockSpec((ms, tn), lambda j, i: (0, jnp.minimum(j, n_main - 1))),
            pl.BlockSpec((ms, tn), lambda j, i: (0, jnp.maximum(j - n_main, 0))),
        ],
        out_shape=[
            jax.ShapeDtypeStruct((mp, COL_GATES), F32),
            jax.ShapeDtypeStruct((mp, N_BRANCH * D_MODEL), BF16),
            jax.ShapeDtypeStruct((ms, COL_GATES), F32),
            jax.ShapeDtypeStruct((ms, N_BRANCH * D_MODEL), BF16),
        ],
        scratch_shapes=[pltpu.VMEM((tn, D_MODEL), BF16)],
        compiler_params=_params(("arbitrary", "arbitrary")),
        name="inproj",
    )(hp, hs, w_in_t)


GLA_STEP_CHUNKS = 4


def _gla_kernel(q_ref, k_ref, v_ref, gr_ref, glr_ref, wlr_ref, blr_ref, gn_ref,
                oa_ref, sfin_ref, st_ref):
    c = pl.program_id(1)
    cs = GLA_CHUNK

    @pl.when(c == 0)
    def _():
        st_ref[...] = jnp.zeros_like(st_ref)

    log_a = _forget_gate_log(glr_ref, wlr_ref, blr_ref)
    tr = GLA_STEP_CHUNKS * cs
    row = lax.broadcasted_iota(jnp.int32, (tr, tr), 0)
    col = lax.broadcasted_iota(jnp.int32, (tr, tr), 1)
    log2_cs = cs.bit_length() - 1
    same_chunk = jnp.right_shift(row, log2_cs) == jnp.right_shift(col, log2_cs)
    causal = jnp.logical_and(same_chunk, row >= col)
    tri = jnp.where(causal, 1.0, 0.0).astype(BF16)
    gn = gn_ref[...]

    b = sum(jnp.dot(tri, part, preferred_element_type=F32) for part in _split_bf16x3(log_a))
    b_last = jnp.concatenate(
        [jnp.broadcast_to(b[(n + 1) * cs - 1:(n + 1) * cs, :], (cs, b.shape[1]))
         for n in range(GLA_STEP_CHUNKS)], axis=0)
    q = q_ref[...] * (GLA_DK ** -0.5)
    k = k_ref[...]
    qt = (q * jnp.exp(b)).astype(BF16)
    kt = (k * jnp.exp(-b)).astype(BF16)
    kd = (k * jnp.exp(b_last - b)).astype(BF16)
    g_last = jnp.exp(b_last)
    v = v_ref[...].astype(BF16)

    for h in range(GLA_HEADS):
        ks = slice(h * GLA_DK, (h + 1) * GLA_DK)
        vs = slice(h * GLA_DV, (h + 1) * GLA_DV)
        a = lax.dot_general(qt[:, ks], kt[:, ks], NT_DIMS, preferred_element_type=F32)
        a = jnp.where(causal, a, 0.0).astype(BF16)
        o_intra = jnp.dot(a, v[:, vs], preferred_element_type=F32)
        u_t = [lax.dot_general(v[n * cs:(n + 1) * cs, vs], kd[n * cs:(n + 1) * cs, ks], TN_DIMS,
                               preferred_element_type=F32) for n in range(GLA_STEP_CHUNKS)]
        s_t = st_ref[h]
        for n in range(GLA_STEP_CHUNKS):
            rs = slice(n * cs, (n + 1) * cs)
            o = lax.dot_general(qt[rs, ks], s_t.astype(BF16), NT_DIMS,
                                preferred_element_type=F32) + o_intra[rs, :]
            s_t = g_last[n * cs:n * cs + 1, ks] * s_t + u_t[n]
            oa_ref[rs, vs] = (_rmsnorm_rows(o, gn) * _silu(gr_ref[rs, vs])).astype(BF16)
        st_ref[h] = s_t

    @pl.when(c == pl.num_programs(1) - 1)
    def _():
        for h in range(GLA_HEADS):
            sfin_ref[0, h] = st_ref[h].T


def _gla_prompt(y, glr, w_lr_p, b_lr, gla_norm, layer, batch, seq):
    tr = GLA_STEP_CHUNKS * GLA_CHUNK
    ns = seq // tr
    rows = lambda b, c: b * ns + c
    return pl.pallas_call(
        _gla_kernel,
        grid=(batch, ns),
        in_specs=[
            pl.BlockSpec((tr, GLA_QK), lambda b, c: (rows(b, c), COL_Q // GLA_QK)),
            pl.BlockSpec((tr, GLA_QK), lambda b, c: (rows(b, c), COL_K // GLA_QK)),
            pl.BlockSpec((tr, GLA_V), lambda b, c: (rows(b, c), COL_V // GLA_V)),
            pl.BlockSpec((tr, GLA_V), lambda b, c: (rows(b, c), COL_GR // GLA_V)),
            pl.BlockSpec((tr, LANES), lambda b, c: (rows(b, c), 0)),
            pl.BlockSpec((None, LANES, GLA_QK), lambda b, c: (layer, 0, 0)),
            pl.BlockSpec((None, 1, GLA_QK), lambda b, c: (layer, 0, 0)),
            pl.BlockSpec((None, 1, GLA_DV), lambda b, c: (layer, 0, 0)),
        ],
        out_specs=[
            pl.BlockSpec((tr, GLA_V), lambda b, c: (rows(b, c), 0)),
            pl.BlockSpec((1, GLA_HEADS, GLA_DK, GLA_DV), lambda b, c: (b, 0, 0, 0)),
        ],
        out_shape=[
            jax.ShapeDtypeStruct((batch * seq, GLA_V), BF16),
            jax.ShapeDtypeStruct((batch, GLA_HEADS, GLA_DK, GLA_DV), F32),
        ],
        scratch_shapes=[pltpu.VMEM((GLA_HEADS, GLA_DV, GLA_DK), F32)],
        compiler_params=_params(("parallel", "arbitrary")),
        name="gla_prompt",
    )(y, y, y, y, glr, w_lr_p, b_lr, gla_norm)


def _conv_tile(cb_ref, cc_ref, ch_ref, w_ref, ob_ref, cs_ref, prev_ref):
    @pl.when(pl.program_id(1) == 0)
    def _():
        prev_ref[...] = jnp.zeros_like(prev_ref)

    u = cc_ref[...] * ch_ref[...]
    tl = u.shape[0]
    prev = prev_ref[...]
    p_m1 = prev[SUBLANES - 1:SUBLANES, :]
    p_m2 = prev[SUBLANES - 2:SUBLANES - 1, :]
    row = lax.broadcasted_iota(jnp.int32, u.shape, 0)
    u1 = jnp.where(row == 0, p_m1, pltpu.roll(u, 1, 0))
    u2 = jnp.where(row == 0, p_m2, jnp.where(row == 1, p_m1, pltpu.roll(u, 2, 0)))
    w = w_ref[...]
    conv = w[0:1, :] * u2 + w[1:2, :] * u1 + w[2:3, :] * u
    ob_ref[...] = (cb_ref[...] * conv).astype(BF16)
    prev_ref[...] = u[tl - SUBLANES:tl, :]
    cs_ref[0] = u[tl - (CONV_K - 1):tl, :]


def _swa_conv_kernel(sinks_ref, q_ref, kc_ref, kp_ref, vc_ref, vp_ref,
                     cb_ref, cc_ref, ch_ref, cw_ref,
                     o_ref, ob_ref, cs_ref, bias_ref, prev_ref, *, layer):
    _conv_tile(cb_ref, cc_ref, ch_ref, cw_ref, ob_ref, cs_ref, prev_ref)
    blk = pl.program_id(1)

    key = lax.broadcasted_iota(jnp.int32, (2 * WINDOW, WINDOW), 0)

    @pl.when(blk == 0)
    def _():
        qry = lax.broadcasted_iota(jnp.int32, (2 * WINDOW, WINDOW), 1)
        dist = qry - key + WINDOW
        valid = jnp.logical_and(dist >= 0, dist < WINDOW)
        dist_f = dist.astype(F32)
        for h in range(SWA_HEADS):
            bias_ref[h] = jnp.where(valid, -ALIBI_SLOPES[h] * dist_f, -jnp.inf)

    key_exists = jnp.logical_or(blk > 0, key >= WINDOW)
    q = (q_ref[...] * SWA_SCALE).astype(BF16)
    k2 = jnp.concatenate([kp_ref[...], kc_ref[...]], axis=0).astype(BF16)
    v2_t = jnp.concatenate([vp_ref[...], vc_ref[...]], axis=0).T.astype(BF16)
    scores = []
    for h in range(SWA_HEADS):
        j = h // SWA_GROUP
        scores.append(lax.dot_general(k2[:, j * SWA_HD:(j + 1) * SWA_HD],
                                      q[:, h * SWA_HD:(h + 1) * SWA_HD], NT_DIMS,
                                      preferred_element_type=F32))
    probs = []
    for h in range(SWA_HEADS):
        s = jnp.where(key_exists, scores[h] + bias_ref[h], -jnp.inf)
        sink = sinks_ref[layer, h]
        m = jnp.maximum(jnp.max(s, axis=0, keepdims=True), sink)
        e = jnp.exp(s - m)
        den = jnp.sum(e, axis=0, keepdims=True) + jnp.exp(sink - m)
        probs.append((e * (1.0 / den)).astype(BF16))
    outs_t = []
    for h in range(SWA_HEADS):
        j = h // SWA_GROUP
        outs_t.append(jnp.dot(v2_t[j * SWA_HD:(j + 1) * SWA_HD, :], probs[h],
                              preferred_element_type=F32))
    o_ref[...] = jnp.concatenate(outs_t, axis=0).T.astype(BF16)


def _swa_conv_prompt(y, sinks, conv_w, layer, batch, seq):
    nb = seq // WINDOW
    rows = lambda b, i: b * nb + i
    prev_rows = lambda b, i: b * nb + jnp.maximum(i - 1, 0)
    conv_seg = lambda col: pl.BlockSpec((WINDOW, CONV_WIDTH),
                                        lambda b, i: (rows(b, i), col // CONV_WIDTH))
    return pl.pallas_call(
        functools.partial(_swa_conv_kernel, layer=layer),
        grid=(batch, nb),
        in_specs=[
            pl.BlockSpec(memory_space=pltpu.SMEM),
            pl.BlockSpec((WINDOW, SWA_Q), lambda b, i: (rows(b, i), COL_SQ // SWA_Q)),
            pl.BlockSpec((WINDOW, SWA_KV), lambda b, i: (rows(b, i), COL_SK // SWA_KV)),
            pl.BlockSpec((WINDOW, SWA_KV), lambda b, i: (prev_rows(b, i), COL_SK // SWA_KV)),
            pl.BlockSpec((WINDOW, SWA_KV), lambda b, i: (rows(b, i), COL_SV // SWA_KV)),
            pl.BlockSpec((WINDOW, SWA_KV), lambda b, i: (prev_rows(b, i), COL_SV // SWA_KV)),
            conv_seg(COL_CB), conv_seg(COL_CC), conv_seg(COL_CH),
            pl.BlockSpec((None, CONV_K, CONV_WIDTH), lambda b, i: (layer, 0, 0)),
        ],
        out_specs=[
            pl.BlockSpec((WINDOW, SWA_Q), lambda b, i: (rows(b, i), 0)),
            pl.BlockSpec((WINDOW, CONV_WIDTH), lambda b, i: (rows(b, i), 0)),
            pl.BlockSpec((1, CONV_K - 1, CONV_WIDTH), lambda b, i: (b, 0, 0)),
        ],
        out_shape=[
            jax.ShapeDtypeStruct((batch * seq, SWA_Q), BF16),
            jax.ShapeDtypeStruct((batch * seq, CONV_WIDTH), BF16),
            jax.ShapeDtypeStruct((batch, CONV_K - 1, CONV_WIDTH), F32),
        ],
        scratch_shapes=[pltpu.VMEM((SWA_HEADS, 2 * WINDOW, WINDOW), F32),
                        pltpu.VMEM((SUBLANES, CONV_WIDTH), F32)],
        compiler_params=_params(("parallel", "arbitrary")),
        name="swa_conv_prompt",
    )(sinks, y, y, y, y, y, y, y, y, conv_w)


SAMPLE_STEP = SUBLANES


def _sample_kernel(sinks_ref, q_ref, k_ref, v_ref, gr_ref, glr_ref, cb_ref, cc_ref, ch_ref,
                   sq_ref, sk_ref, sv_ref, wlr_ref, blr_ref, gn_ref, cw_ref,
                   s_ref, cs_ref, kc_ref, vc_ref, *rest, layer, n_carried):
    oa_ref, ob_ref, oc_ref, so_ref, cso_ref, kco_ref, vco_ref, o_scr = rest[n_carried:]
    nb = SAMPLE_STEP
    b = _forget_gate_log(glr_ref, wlr_ref, blr_ref)
    q = q_ref[...] * (GLA_DK ** -0.5)
    k = k_ref[...]
    v = v_ref[...]
    qt = q * jnp.exp(b)
    kt = k * jnp.exp(-b)
    kd = k * jnp.exp(b - b)
    g_last = jnp.exp(b)
    qt_t = qt.T
    kd_t = kd.T
    gl_t = g_last.T
    for h in range(GLA_HEADS):
        ks = slice(h * GLA_DK, (h + 1) * GLA_DK)
        vs = slice(h * GLA_DV, (h + 1) * GLA_DV)
        a = jnp.sum(qt[:, ks] * kt[:, ks], axis=-1, keepdims=True)
        o_intra = a * v[:, vs]
        for i in range(nb):
            s_in = s_ref[i, h]
            v_row = v[i:i + 1, vs]
            o = jnp.sum(qt_t[ks, i:i + 1] * s_in, axis=0, keepdims=True)
            o_scr[i:i + 1, vs] = o + o_intra[i:i + 1, :]
            so_ref[i, h] = gl_t[ks, i:i + 1] * s_in + kd_t[ks, i:i + 1] * v_row
    gn = gn_ref[...]
    for h in range(GLA_HEADS):
        vs = slice(h * GLA_DV, (h + 1) * GLA_DV)
        oa_ref[:, vs] = _rmsnorm_rows(o_scr[:, vs], gn) * _silu(gr_ref[:, vs])

    u_new = cc_ref[...] * ch_ref[...]
    w = cw_ref[...]
    p0 = cs_ref[:, :CONV_WIDTH]
    p1 = cs_ref[:, CONV_WIDTH:]
    conv = w[0:1, :] * p0 + w[1:2, :] * p1 + w[2:3, :] * u_new
    ob_ref[...] = cb_ref[...] * conv
    cso_ref[:, :CONV_WIDTH] = p1
    cso_ref[:, CONV_WIDTH:] = u_new

    slot = lax.broadcasted_iota(jnp.int32, (SWA_HD, WINDOW), 1)
    key = lax.broadcasted_iota(jnp.int32, (SUBLANES, WINDOW), 1)
    grp = lax.broadcasted_iota(jnp.int32, (SUBLANES, WINDOW), 0)
    dist = (WINDOW - 1 - key).astype(F32)
    bias, sink = [], []
    for j in range(SWA_KV_HEADS):
        bj = jnp.zeros((SUBLANES, WINDOW), F32)
        sj = jnp.zeros((SUBLANES, 1), F32)
        for g in range(SWA_GROUP):
            bj = jnp.where(grp == g, -ALIBI_SLOPES[j * SWA_GROUP + g] * dist, bj)
            sj = jnp.where(grp[:, 0:1] == g, sinks_ref[layer, j * SWA_GROUP + g], sj)
        bias.append(bj)
        sink.append(sj)
    sk_t = sk_ref[...].T
    sv_t = sv_ref[...].T
    k_new, v_new = [], []
    for i in range(nb):
        for j in range(SWA_KV_HEADS):
            hd = slice(j * SWA_HD, (j + 1) * SWA_HD)
            kn = jnp.where(slot == WINDOW - 1, sk_t[hd, i:i + 1],
                           pltpu.roll(kc_ref[i, j], WINDOW - 1, 1))
            vn = jnp.where(slot == WINDOW - 1, sv_t[hd, i:i + 1],
                           pltpu.roll(vc_ref[i, j], WINDOW - 1, 1))
            kco_ref[i, j] = kn
            vco_ref[i, j] = vn
            k_new.append(kn.astype(BF16))
            v_new.append(vn.astype(BF16))
    scores = []
    for i in range(nb):
        for j in range(SWA_KV_HEADS):
            qj = (sq_ref[i, j] * SWA_SCALE).astype(BF16)
            scores.append(jnp.dot(qj, k_new[i * SWA_KV_HEADS + j], preferred_element_type=F32))
    probs = []
    for i in range(nb):
        for j in range(SWA_KV_HEADS):
            s = scores[i * SWA_KV_HEADS + j] + bias[j]
            m = jnp.maximum(jnp.max(s, axis=-1, keepdims=True), sink[j])
            e = jnp.exp(s - m)
            den = jnp.sum(e, axis=-1, keepdims=True) + jnp.exp(sink[j] - m)
            probs.append((e * (1.0 / den)).astype(BF16))
    for i in range(nb):
        for j in range(SWA_KV_HEADS):
            n = i * SWA_KV_HEADS + j
            oc_ref[i, j] = lax.dot_general(probs[n], v_new[n], NT_DIMS,
                                           preferred_element_type=F32)


def _sample_mixer(y, glr, sinks, w_lr_p, b_lr, gla_norm, conv_w,
                  state_gla, state_conv, cache_k, cache_v, carried, layer):
    nb = y.shape[0]
    st = SAMPLE_STEP
    sq = y[:, COL_SQ:COL_SQ + SWA_Q].reshape(nb, SWA_KV_HEADS, SWA_GROUP, SWA_HD)
    sq = jnp.pad(sq, ((0, 0), (0, 0), (0, SUBLANES - SWA_GROUP), (0, 0)))

    def seg(width, col):
        return pl.BlockSpec((st, width), lambda b: (b, col // width))

    def per_layer(shape):
        return pl.BlockSpec((None,) + shape, lambda b: (layer,) + (0,) * len(shape))

    def state(shape):
        return pl.BlockSpec((None, st) + shape, lambda b: (layer, b) + (0,) * len(shape))

    def out(shape):
        return pl.BlockSpec((st,) + shape, lambda b: (b,) + (0,) * len(shape))

    cache_shape = (SWA_KV_HEADS, SWA_HD, WINDOW)
    state_shapes = ((GLA_HEADS, GLA_DK, GLA_DV), ((CONV_K - 1) * CONV_WIDTH,),
                    cache_shape, cache_shape)
    n_fixed_inputs = 20
    res = pl.pallas_call(
        functools.partial(_sample_kernel, layer=layer, n_carried=len(carried)),
        grid=(nb // st,),
        in_specs=[
            pl.BlockSpec(memory_space=pltpu.SMEM),
            seg(GLA_QK, COL_Q), seg(GLA_QK, COL_K), seg(GLA_V, COL_V), seg(GLA_V, COL_GR),
            pl.BlockSpec((st, LANES), lambda b: (b, 0)),
            seg(CONV_WIDTH, COL_CB), seg(CONV_WIDTH, COL_CC), seg(CONV_WIDTH, COL_CH),
            out((SWA_KV_HEADS, SUBLANES, SWA_HD)), seg(SWA_KV, COL_SK), seg(SWA_KV, COL_SV),
            per_layer((LANES, GLA_QK)), per_layer((1, GLA_QK)), per_layer((1, GLA_DV)),
            per_layer((CONV_K, CONV_WIDTH)),
        ] + [state(s) for s in state_shapes] + [pl.BlockSpec(memory_space=pl.ANY)] * len(carried),
        out_specs=[
            out((GLA_V,)), out((CONV_WIDTH,)), out((SWA_KV_HEADS, SUBLANES, SWA_HD)),
        ] + [state(s) for s in state_shapes],
        out_shape=[
            jax.ShapeDtypeStruct((nb, GLA_V), F32),
            jax.ShapeDtypeStruct((nb, CONV_WIDTH), F32),
            jax.ShapeDtypeStruct((nb, SWA_KV_HEADS, SUBLANES, SWA_HD), F32),
        ] + [jax.ShapeDtypeStruct((DEPTH, nb) + s, F32) for s in state_shapes],
        input_output_aliases={n_fixed_inputs + n: 3 + n for n in range(len(carried))},
        scratch_shapes=[pltpu.VMEM((st, GLA_V), F32)],
        compiler_params=_params(("parallel",)),
        name="sample_mixer",
    )(sinks, y, y, y, y, glr, y, y, y, sq, y, y,
      w_lr_p, b_lr, gla_norm, conv_w, state_gla, state_conv, cache_k, cache_v, *carried)
    oa, ob, oc = res[:3]
    oc = oc[:, :, :SWA_GROUP, :].reshape(nb, SWA_Q)
    return oa, ob, oc, tuple(res[3:])


def _merge_kernel(*refs, n_ptiles):
    (pa, pb, pc, pg0, pg1, pg2, px, sa, sb, sc, sg0, sg1, sg2, sx, wb_ref, wo_ref,
     po_ref, so_ref) = refs

    def tile(a_ref, b_ref, c_ref, g0_ref, g1_ref, g2_ref, x_ref, o_ref):
        @pl.when(pl.program_id(1) == 0)
        def _():
            o_ref[...] = x_ref[...]

        def branch(src_ref, gate_ref, n):
            br = jnp.dot(src_ref[...].astype(BF16), wb_ref[n], preferred_element_type=F32)
            return jax.nn.sigmoid(gate_ref[...].astype(F32)) * br

        mixed = branch(a_ref, g0_ref, 0) + branch(b_ref, g1_ref, 1) + branch(c_ref, g2_ref, 2)
        o_ref[...] += jnp.dot(mixed.astype(BF16), wo_ref[...], preferred_element_type=F32)

    @pl.when(pl.program_id(0) < n_ptiles)
    def _():
        tile(pa, pb, pc, pg0, pg1, pg2, px, po_ref)

    @pl.when(pl.program_id(0) == n_ptiles)
    def _():
        tile(sa, sb, sc, sg0, sg1, sg2, sx, so_ref)


def _merge(prompt, sample, w_branch, w_out, layer, tm, tn):
    mp, ms = prompt[4].shape[0], sample[4].shape[0]
    n_ptiles = mp // tm
    prow = lambda i: jnp.minimum(i, n_ptiles - 1)
    gate_col = lambda n: (n * D_MODEL) // tn
    once = pl.Buffered(1)
    p_src = pl.BlockSpec((tm, BRANCH_WIDTH), lambda i, t: (prow(i), 0), pipeline_mode=once)
    p_gate = lambda n: pl.BlockSpec((tm, tn), lambda i, t: (prow(i), gate_col(n) + t))
    s_src = pl.BlockSpec((ms, BRANCH_WIDTH), lambda i, t: (0, 0))
    s_gate = lambda n: pl.BlockSpec((ms, tn), lambda i, t: (0, gate_col(n) + t))
    return pl.pallas_call(
        functools.partial(_merge_kernel, n_ptiles=n_ptiles),
        grid=(n_ptiles + 1, D_MODEL // tn),
        in_specs=[
            p_src, p_src, p_src, p_gate(0), p_gate(1), p_gate(2),
            pl.BlockSpec((tm, D_MODEL), lambda i, t: (prow(i), 0), pipeline_mode=once),
            s_src, s_src, s_src, s_gate(0), s_gate(1), s_gate(2),
            pl.BlockSpec((ms, D_MODEL), lambda i, t: (0, 0)),
            pl.BlockSpec((None, N_BRANCH, BRANCH_WIDTH, tn), lambda i, t: (layer, 0, 0, t)),
            pl.BlockSpec((None, tn, D_MODEL), lambda i, t: (layer, t, 0)),
        ],
        out_specs=[
            pl.BlockSpec((tm, D_MODEL), lambda i, t: (prow(i), 0)),
            pl.BlockSpec((ms, D_MODEL), lambda i, t: (0, 0)),
        ],
        out_shape=[
            jax.ShapeDtypeStruct((mp, D_MODEL), F32),
            jax.ShapeDtypeStruct((ms, D_MODEL), F32),
        ],
        compiler_params=_params(("arbitrary", "arbitrary")),
        name="merge_outproj",
    )(prompt[0], prompt[1], prompt[2], prompt[3], prompt[3], prompt[3], prompt[4],
      sample[0], sample[1], sample[2], sample[3], sample[3], sample[3], sample[4],
      w_branch, w_out)


def _mlp_kernel(xp_ref, xs_ref, g_ref, wu_ref, wd_ref, gnext_ref, wlr_ref, *rest,
                last, n_ptiles):
    if last:
        po, so, hp_scr, hs_scr = rest
        p_out, s_out = (po,), (so,)
    else:
        po, php, pglr, so, shp, sglr, hp_scr, hs_scr = rest
        p_out, s_out = (po, php, pglr), (so, shp, sglr)
    f = pl.program_id(1)

    def tile(x_ref, h_ref, o_ref, hn_ref=None, glr_ref=None):
        @pl.when(f == 0)
        def _():
            x = x_ref[...]
            h_ref[...] = _rmsnorm_rows(x, g_ref[...]).astype(BF16)
            o_ref[...] = x

        up = jnp.dot(h_ref[...], wu_ref[...], preferred_element_type=F32)
        act = jnp.square(jnp.maximum(up, 0.0)).astype(BF16)
        o_ref[...] += jnp.dot(act, wd_ref[...], preferred_element_type=F32)

        @pl.when(f == pl.num_programs(1) - 1)
        def _():
            xn = _rmsnorm_rows(o_ref[...], gnext_ref[...])
            if last:
                o_ref[...] = xn
            else:
                hn = xn.astype(BF16)
                hn_ref[...] = hn
                glr_ref[...] = _lr_project(hn, wlr_ref)

    @pl.when(pl.program_id(0) < n_ptiles)
    def _():
        tile(xp_ref, hp_scr, *p_out)

    @pl.when(pl.program_id(0) == n_ptiles)
    def _():
        tile(xs_ref, hs_scr, *s_out)


def _mlp(xp, xs, norm_w, w_up, w_down, next_norm_w, w_in_t, layer, tm, tf):
    mp, ms = xp.shape[0], xs.shape[0]
    n_ptiles = mp // tm
    last = layer == DEPTH - 1
    prow = lambda i: jnp.minimum(i, n_ptiles - 1)

    def group_out(m, rows, row_map):
        specs = [pl.BlockSpec((rows, D_MODEL), lambda i, f: (row_map(i), 0))]
        shapes = [jax.ShapeDtypeStruct((m, D_MODEL), F32)]
        if not last:
            specs += [pl.BlockSpec((rows, D_MODEL), lambda i, f: (row_map(i), 0)),
                      pl.BlockSpec((rows, LANES), lambda i, f: (row_map(i), 0))]
            shapes += [jax.ShapeDtypeStruct((m, D_MODEL), BF16),
                       jax.ShapeDtypeStruct((m, LANES), F32)]
        return specs, shapes

    p_specs, p_shapes = group_out(mp, tm, prow)
    s_specs, s_shapes = group_out(ms, ms, lambda i: 0)
    if last:
        next_norm_spec = pl.BlockSpec((1, D_MODEL), lambda i, f: (0, 0))
        lr_layer = layer
    else:
        next_norm_spec = pl.BlockSpec((None, 1, D_MODEL), lambda i, f: (layer + 1, 0, 0))
        lr_layer = layer + 1
    return pl.pallas_call(
        functools.partial(_mlp_kernel, last=last, n_ptiles=n_ptiles),
        grid=(n_ptiles + 1, D_FF // tf),
        in_specs=[
            pl.BlockSpec((tm, D_MODEL), lambda i, f: (prow(i), 0), pipeline_mode=pl.Buffered(1)),
            pl.BlockSpec((ms, D_MODEL), lambda i, f: (0, 0)),
            pl.BlockSpec((None, 1, D_MODEL), lambda i, f: (layer, 0, 0)),
            pl.BlockSpec((None, D_MODEL, tf), lambda i, f: (layer, 0, f)),
            pl.BlockSpec((None, tf, D_MODEL), lambda i, f: (layer, f, 0)),
            next_norm_spec,
            _lr_weight_spec(lr_layer),
        ],
        out_specs=p_specs + s_specs,
        out_shape=p_shapes + s_shapes,
        scratch_shapes=[pltpu.VMEM((tm, D_MODEL), BF16), pltpu.VMEM((ms, D_MODEL), BF16)],
        compiler_params=_params(("arbitrary", "arbitrary")),
        name="mlp",
    )(xp, xs, norm_w, w_up, w_down, next_norm_w, w_in_t)


def kernel(x_prompt, x_sample, state_gla, state_conv, cache_k, cache_v, w_in, w_lr, b_lr,
           gla_norm, conv_w, attn_sinks, w_branch, w_out, norm_mix, norm_mlp, w_up, w_down,
           norm_final):
    batch, seq, _ = x_prompt.shape
    nb = x_sample.shape[0]
    mp = batch * seq

    w_lr_p = jnp.pad(w_lr, ((0, 0), (0, LANES - GLA_RANK), (0, 0)))
    w_branch_b = w_branch.astype(BF16)
    w_out_b = w_out.astype(BF16)
    w_up_b = w_up.astype(BF16)
    w_down_b = w_down.astype(BF16)
    b_lr3 = b_lr.reshape(DEPTH, 1, GLA_QK)
    gla_norm3 = gla_norm.reshape(DEPTH, 1, GLA_DV)
    norm_mix3 = norm_mix.reshape(DEPTH, 1, D_MODEL)
    norm_mlp3 = norm_mlp.reshape(DEPTH, 1, D_MODEL)
    norm_final2 = norm_final.reshape(1, D_MODEL)
    state_conv2 = state_conv.reshape(DEPTH, nb, (CONV_K - 1) * CONV_WIDTH)
    cache_k4 = jnp.transpose(cache_k, (0, 1, 3, 4, 2))
    cache_v4 = jnp.transpose(cache_v, (0, 1, 3, 4, 2))

    w_in_t = jnp.swapaxes(w_in, 1, 2)
    tm = min(1024, mp)

    xp = x_prompt.reshape(mp, D_MODEL)
    xs = x_sample.reshape(nb, D_MODEL)
    hp, glrp = _norm(xp, norm_mix3, w_in_t, 0, tm=tm)
    hs, glrs = _norm(xs, norm_mix3, w_in_t, 0, tm=nb)
    outs = [[] for _ in range(4)]
    sample_states = ()
    for l in range(DEPTH):
        next_norm = norm_final2 if l == DEPTH - 1 else norm_mix3
        yp, gp, ys, gs = _inproj(hp, hs, w_in_t, l, tm=min(512, mp), tn=1536)
        oa, sg_p = _gla_prompt(yp, glrp, w_lr_p, b_lr3, gla_norm3, l, batch, seq)
        oc, ob, sc_p = _swa_conv_prompt(yp, attn_sinks, conv_w, l, batch, seq)
        yp3 = yp.reshape(batch, seq, COL_GATES)
        kp = yp3[:, seq - WINDOW:, COL_SK:COL_SK + SWA_KV]
        vp = yp3[:, seq - WINDOW:, COL_SV:COL_SV + SWA_KV]
        sa, sb, sc, sample_states = _sample_mixer(
            ys, glrs, attn_sinks, w_lr_p, b_lr3, gla_norm3, conv_w,
            state_gla, state_conv2, cache_k4, cache_v4, sample_states, l)
        xp, xs = _merge((oa, ob, oc, gp, xp), (sa, sb, sc, gs, xs), w_branch_b, w_out_b, l,
                        tm=tm, tn=512)
        res = _mlp(xp, xs, norm_mlp3, w_up_b, w_down_b, next_norm, w_in_t, l, tm=tm, tf=512)
        if l == DEPTH - 1:
            xp, xs = res
        else:
            xp, hp, glrp, xs, hs, glrs = res
        for lst, val in zip(outs, (
                sg_p, sc_p,
                kp.reshape(batch, WINDOW, SWA_KV_HEADS, SWA_HD),
                vp.reshape(batch, WINDOW, SWA_KV_HEADS, SWA_HD))):
            lst.append(val)
    y_prompt = xp.reshape(batch, seq, D_MODEL)
    y_sample = xs.reshape(nb, 1, D_MODEL)
    sg_p, sc_p, kp, vp = (jnp.stack(o) for o in outs)
    sg_s, sc_s, ks, vs = sample_states
    return (y_prompt, y_sample, sg_p, sg_s, sc_p,
            sc_s.reshape(DEPTH, nb, CONV_K - 1, CONV_WIDTH), kp,
            jnp.transpose(ks, (0, 1, 4, 2, 3)), vp, jnp.transpose(vs, (0, 1, 4, 2, 3)))
```

```python
import functools

import jax
import jax.numpy as jnp
from jax import lax
from jax.experimental import pallas as pl
from jax.experimental.pallas import tpu as pltpu

F32 = jnp.float32
BF16 = jnp.bfloat16

D_MODEL = 2048
DEPTH = 4
PAST_LEN = 16384
BRANCH_WIDTH = D_MODEL // 2
N_BRANCH = 3
GLA_HEADS = 4
GLA_DV = BRANCH_WIDTH // GLA_HEADS
GLA_DK = GLA_DV // 2
GLA_RANK = 16
GLA_TAU = 16.0
GLA_CHUNK = 64
CONV_WIDTH = BRANCH_WIDTH
CONV_K = 3
SWA_HD = 64
SWA_HEADS = BRANCH_WIDTH // SWA_HD
SWA_KV_HEADS = SWA_HEADS // 4
SWA_GROUP = SWA_HEADS // SWA_KV_HEADS
WINDOW = 128
D_FF = 4 * D_MODEL
EPS = 1e-6

LANES = 128
SUBLANES = 8
VMEM_LIMIT = 56 * 1024 * 1024

GLA_QK = GLA_HEADS * GLA_DK
GLA_V = GLA_HEADS * GLA_DV
SWA_Q = SWA_HEADS * SWA_HD
SWA_KV = SWA_KV_HEADS * SWA_HD

COL_Q = 0
COL_K = COL_Q + GLA_QK
COL_V = COL_K + GLA_QK
COL_GR = COL_V + GLA_V
COL_CB = COL_GR + GLA_V
COL_CC = COL_CB + CONV_WIDTH
COL_CH = COL_CC + CONV_WIDTH
COL_SQ = COL_CH + CONV_WIDTH
COL_SK = COL_SQ + SWA_Q
COL_SV = COL_SK + SWA_KV
COL_GATES = COL_SV + SWA_KV
PACKED_COLS = COL_GATES + N_BRANCH * D_MODEL
LR_SRC = COL_CB

ALIBI_SLOPES = tuple(2.0 ** (-8.0 * (h + 1) / SWA_HEADS) for h in range(SWA_HEADS))
SWA_SCALE = SWA_HD ** -0.5

NT_DIMS = (((1,), (1,)), ((), ()))
TN_DIMS = (((0,), (0,)), ((), ()))


def _params(sem):
    return pltpu.CompilerParams(dimension_semantics=sem, vmem_limit_bytes=VMEM_LIMIT)


def _rmsnorm_rows(x, g):
    ms = jnp.mean(x * x, axis=-1, keepdims=True)
    return x * lax.rsqrt(ms + EPS) * g


def _log_sigmoid(z):
    return jnp.minimum(z, 0.0) - jnp.log(1.0 + jnp.exp(-jnp.abs(z)))


def _forget_gate_log(glr_ref, wlr_ref, blr_ref):
    z = jnp.dot(glr_ref[...].astype(BF16), wlr_ref[...].astype(BF16),
                preferred_element_type=F32) + blr_ref[...]
    return _log_sigmoid(z) / GLA_TAU


def _split_bf16x3(x):
    hi = x.astype(BF16)
    r = x - hi.astype(F32)
    mid = r.astype(BF16)
    lo = (r - mid.astype(F32)).astype(BF16)
    return hi, mid, lo


def _silu(x):
    return x * jax.nn.sigmoid(x)


def _lr_weight_spec(layer):
    return pl.BlockSpec((None, LANES, D_MODEL), lambda *a: (layer, LR_SRC // LANES, 0))


def _lr_project(h, wlr_ref):
    return lax.dot_general(h, wlr_ref[...].astype(BF16), NT_DIMS, preferred_element_type=F32)


def _norm_kernel(x_ref, g_ref, wlr_ref, h_ref, glr_ref):
    h = _rmsnorm_rows(x_ref[...], g_ref[...]).astype(BF16)
    h_ref[...] = h
    glr_ref[...] = _lr_project(h, wlr_ref)


def _norm(x, norm_w, w_in_t, layer, tm):
    m = x.shape[0]
    return pl.pallas_call(
        _norm_kernel,
        grid=(m // tm,),
        in_specs=[
            pl.BlockSpec((tm, D_MODEL), lambda i: (i, 0)),
            pl.BlockSpec((None, 1, D_MODEL), lambda i: (layer, 0, 0)),
            _lr_weight_spec(layer),
        ],
        out_specs=[
            pl.BlockSpec((tm, D_MODEL), lambda i: (i, 0)),
            pl.BlockSpec((tm, LANES), lambda i: (i, 0)),
        ],
        out_shape=[
            jax.ShapeDtypeStruct((m, D_MODEL), BF16),
            jax.ShapeDtypeStruct((m, LANES), F32),
        ],
        compiler_params=_params(("parallel",)),
        name="norm",
    )(x, norm_w, w_in_t)


def _inproj_kernel(hp_ref, hs_ref, w_ref, yp_ref, gp_ref, ys_ref, gs_ref, wbf_ref, *,
                   n_main, n_ptiles):
    j = pl.program_id(0)
    i = pl.program_id(1)

    @pl.when(i == 0)
    def _():
        wbf_ref[...] = w_ref[...].astype(BF16)

    def project(h_ref):
        return lax.dot_general(h_ref[...], wbf_ref[...], NT_DIMS, preferred_element_type=F32)

    is_main = j < n_main
    is_prompt = i < n_ptiles

    @pl.when(jnp.logical_and(is_main, is_prompt))
    def _():
        yp_ref[...] = project(hp_ref)

    @pl.when(jnp.logical_and(jnp.logical_not(is_main), is_prompt))
    def _():
        gp_ref[...] = project(hp_ref).astype(BF16)

    @pl.when(jnp.logical_and(is_main, jnp.logical_not(is_prompt)))
    def _():
        ys_ref[...] = project(hs_ref)

    @pl.when(jnp.logical_and(jnp.logical_not(is_main), jnp.logical_not(is_prompt)))
    def _():
        gs_ref[...] = project(hs_ref).astype(BF16)


def _inproj(hp, hs, w_in_t, layer, tm, tn):
    mp, ms = hp.shape[0], hs.shape[0]
    n_ptiles = mp // tm
    n_main = COL_GATES // tn
    n_gate = (N_BRANCH * D_MODEL) // tn
    n_plain = LR_SRC // tn
    last_p = n_ptiles - 1
    prow = lambda i: jnp.minimum(i, last_p)
    w_row = lambda j: pl.multiple_of(j * tn + jnp.where(j >= n_plain, GLA_RANK, 0), GLA_RANK)
    return pl.pallas_call(
        functools.partial(_inproj_kernel, n_main=n_main, n_ptiles=n_ptiles),
        grid=(n_main + n_gate, n_ptiles + 1),
        in_specs=[
            pl.BlockSpec((tm, D_MODEL), lambda j, i: (prow(i), 0)),
            pl.BlockSpec((ms, D_MODEL), lambda j, i: (0, 0)),
            pl.BlockSpec((None, pl.Element(tn), pl.Element(D_MODEL)),
                         lambda j, i: (layer, w_row(j), 0)),
        ],
        out_specs=[
            pl.BlockSpec((tm, tn), lambda j, i: (jnp.where(j < n_main, prow(i), last_p),
                                                 jnp.minimum(j, n_main - 1))),
            pl.BlockSpec((tm, tn), lambda j, i: (jnp.where(j < n_main, 0, prow(i)),
                                                 jnp.maximum(j - n_main, 0))),
            pl.BlockSpec((ms, tn), lambda j, i: (0, jnp.minimum(j, n_main - 1))),
            pl.BlockSpec((ms, tn), lambda j, i: (0, jnp.maximum(j - n_main, 0))),
        ],
        out_shape=[
            jax.ShapeDtypeStruct((mp, COL_GATES), F32),
            jax.ShapeDtypeStruct((mp, N_BRANCH * D_MODEL), BF16),
            jax.ShapeDtypeStruct((ms, COL_GATES), F32),
            jax.ShapeDtypeStruct((ms, N_BRANCH * D_MODEL), BF16),
        ],
        scratch_shapes=[pltpu.VMEM((tn, D_MODEL), BF16)],
        compiler_params=_params(("arbitrary", "arbitrary")),
        name="inproj",
    )(hp, hs, w_in_t)


GLA_STEP_CHUNKS = 4


def _gla_kernel(q_ref, k_ref, v_ref, gr_ref, glr_ref, wlr_ref, blr_ref, gn_ref,
                oa_ref, sfin_ref, st_ref):
    c = pl.program_id(1)
    cs = GLA_CHUNK

    @pl.when(c == 0)
    def _():
        st_ref[...] = jnp.zeros_like(st_ref)

    log_a = _forget_gate_log(glr_ref, wlr_ref, blr_ref)
    tr = GLA_STEP_CHUNKS * cs
    row = lax.broadcasted_iota(jnp.int32, (tr, tr), 0)
    col = lax.broadcasted_iota(jnp.int32, (tr, tr), 1)
    log2_cs = cs.bit_length() - 1
    same_chunk = jnp.right_shift(row, log2_cs) == jnp.right_shift(col, log2_cs)
    causal = jnp.logical_and(same_chunk, row >= col)
    tri = jnp.where(causal, 1.0, 0.0).astype(BF16)
    gn = gn_ref[...]

    b = sum(jnp.dot(tri, part, preferred_element_type=F32) for part in _split_bf16x3(log_a))
    b_last = jnp.concatenate(
        [jnp.broadcast_to(b[(n + 1) * cs - 1:(n + 1) * cs, :], (cs, b.shape[1]))
         for n in range(GLA_STEP_CHUNKS)], axis=0)
    q = q_ref[...] * (GLA_DK ** -0.5)
    k = k_ref[...]
    qt = (q * jnp.exp(b)).astype(BF16)
    kt = (k * jnp.exp(-b)).astype(BF16)
    kd = (k * jnp.exp(b_last - b)).astype(BF16)
    g_last = jnp.exp(b_last)
    v = v_ref[...].astype(BF16)

    for h in range(GLA_HEADS):
        ks = slice(h * GLA_DK, (h + 1) * GLA_DK)
        vs = slice(h * GLA_DV, (h + 1) * GLA_DV)
        a = lax.dot_general(qt[:, ks], kt[:, ks], NT_DIMS, preferred_element_type=F32)
        a = jnp.where(causal, a, 0.0).astype(BF16)
        o_intra = jnp.dot(a, v[:, vs], preferred_element_type=F32)
        u_t = [lax.dot_general(v[n * cs:(n + 1) * cs, vs], kd[n * cs:(n + 1) * cs, ks], TN_DIMS,
                               preferred_element_type=F32) for n in range(GLA_STEP_CHUNKS)]
        s_t = st_ref[h]
        for n in range(GLA_STEP_CHUNKS):
            rs = slice(n * cs, (n + 1) * cs)
            o = lax.dot_general(qt[rs, ks], s_t.astype(BF16), NT_DIMS,
                                preferred_element_type=F32) + o_intra[rs, :]
            s_t = g_last[n * cs:n * cs + 1, ks] * s_t + u_t[n]
            oa_ref[rs, vs] = (_rmsnorm_rows(o, gn) * _silu(gr_ref[rs, vs])).astype(BF16)
        st_ref[h] = s_t

    @pl.when(c == pl.num_programs(1) - 1)
    def _():
        for h in range(GLA_HEADS):
            sfin_ref[0, h] = st_ref[h].T


def _gla_prompt(y, glr, w_lr_p, b_lr, gla_norm, layer, batch, seq):
    tr = GLA_STEP_CHUNKS * GLA_CHUNK
    ns = seq // tr
    rows = lambda b, c: b * ns + c
    return pl.pallas_call(
        _gla_kernel,
        grid=(batch, ns),
        in_specs=[
            pl.BlockSpec((tr, GLA_QK), lambda b, c: (rows(b, c), COL_Q // GLA_QK)),
            pl.BlockSpec((tr, GLA_QK), lambda b, c: (rows(b, c), COL_K // GLA_QK)),
            pl.BlockSpec((tr, GLA_V), lambda b, c: (rows(b, c), COL_V // GLA_V)),
            pl.BlockSpec((tr, GLA_V), lambda b, c: (rows(b, c), COL_GR // GLA_V)),
            pl.BlockSpec((tr, LANES), lambda b, c: (rows(b, c), 0)),
            pl.BlockSpec((None, LANES, GLA_QK), lambda b, c: (layer, 0, 0)),
            pl.BlockSpec((None, 1, GLA_QK), lambda b, c: (layer, 0, 0)),
            pl.BlockSpec((None, 1, GLA_DV), lambda b, c: (layer, 0, 0)),
        ],
        out_specs=[
            pl.BlockSpec((tr, GLA_V), lambda b, c: (rows(b, c), 0)),
            pl.BlockSpec((1, GLA_HEADS, GLA_DK, GLA_DV), lambda b, c: (b, 0, 0, 0)),
        ],
        out_shape=[
            jax.ShapeDtypeStruct((batch * seq, GLA_V), BF16),
            jax.ShapeDtypeStruct((batch, GLA_HEADS, GLA_DK, GLA_DV), F32),
        ],
        scratch_shapes=[pltpu.VMEM((GLA_HEADS, GLA_DV, GLA_DK), F32)],
        compiler_params=_params(("parallel", "arbitrary")),
        name="gla_prompt",
    )(y, y, y, y, glr, w_lr_p, b_lr, gla_norm)


def _conv_tile(cb_ref, cc_ref, ch_ref, w_ref, ob_ref, cs_ref, prev_ref):
    @pl.when(pl.program_id(1) == 0)
    def _():
        prev_ref[...] = jnp.zeros_like(prev_ref)

    u = cc_ref[...] * ch_ref[...]
    tl = u.shape[0]
    prev = prev_ref[...]
    p_m1 = prev[SUBLANES - 1:SUBLANES, :]
    p_m2 = prev[SUBLANES - 2:SUBLANES - 1, :]
    row = lax.broadcasted_iota(jnp.int32, u.shape, 0)
    u1 = jnp.where(row == 0, p_m1, pltpu.roll(u, 1, 0))
    u2 = jnp.where(row == 0, p_m2, jnp.where(row == 1, p_m1, pltpu.roll(u, 2, 0)))
    w = w_ref[...]
    conv = w[0:1, :] * u2 + w[1:2, :] * u1 + w[2:3, :] * u
    ob_ref[...] = (cb_ref[...] * conv).astype(BF16)
    prev_ref[...] = u[tl - SUBLANES:tl, :]
    cs_ref[0] = u[tl - (CONV_K - 1):tl, :]


def _swa_conv_kernel(sinks_ref, q_ref, kc_ref, kp_ref, vc_ref, vp_ref,
                     cb_ref, cc_ref, ch_ref, cw_ref,
                     o_ref, ob_ref, cs_ref, bias_ref, prev_ref, *, layer):
    _conv_tile(cb_ref, cc_ref, ch_ref, cw_ref, ob_ref, cs_ref, prev_ref)
    blk = pl.program_id(1)

    key = lax.broadcasted_iota(jnp.int32, (2 * WINDOW, WINDOW), 0)

    @pl.when(blk == 0)
    def _():
        qry = lax.broadcasted_iota(jnp.int32, (2 * WINDOW, WINDOW), 1)
        dist = qry - key + WINDOW
        valid = jnp.logical_and(dist >= 0, dist < WINDOW)
        dist_f = dist.astype(F32)
        for h in range(SWA_HEADS):
            bias_ref[h] = jnp.where(valid, -ALIBI_SLOPES[h] * dist_f, -jnp.inf)

    key_exists = jnp.logical_or(blk > 0, key >= WINDOW)
    q = (q_ref[...] * SWA_SCALE).astype(BF16)
    k2 = jnp.concatenate([kp_ref[...], kc_ref[...]], axis=0).astype(BF16)
    v2_t = jnp.concatenate([vp_ref[...], vc_ref[...]], axis=0).T.astype(BF16)
    scores = []
    for h in range(SWA_HEADS):
        j = h // SWA_GROUP
        scores.append(lax.dot_general(k2[:, j * SWA_HD:(j + 1) * SWA_HD],
                                      q[:, h * SWA_HD:(h + 1) * SWA_HD], NT_DIMS,
                                      preferred_element_type=F32))
    probs = []
    for h in range(SWA_HEADS):
        s = jnp.where(key_exists, scores[h] + bias_ref[h], -jnp.inf)
        sink = sinks_ref[layer, h]
        m = jnp.maximum(jnp.max(s, axis=0, keepdims=True), sink)
        e = jnp.exp(s - m)
        den = jnp.sum(e, axis=0, keepdims=True) + jnp.exp(sink - m)
        probs.append((e * (1.0 / den)).astype(BF16))
    outs_t = []
    for h in range(SWA_HEADS):
        j = h // SWA_GROUP
        outs_t.append(jnp.dot(v2_t[j * SWA_HD:(j + 1) * SWA_HD, :], probs[h],
                              preferred_element_type=F32))
    o_ref[...] = jnp.concatenate(outs_t, axis=0).T.astype(BF16)


def _swa_conv_prompt(y, sinks, conv_w, layer, batch, seq):
    nb = seq // WINDOW
    rows = lambda b, i: b * nb + i
    prev_rows = lambda b, i: b * nb + jnp.maximum(i - 1, 0)
    conv_seg = lambda col: pl.BlockSpec((WINDOW, CONV_WIDTH),
                                        lambda b, i: (rows(b, i), col // CONV_WIDTH))
    return pl.pallas_call(
        functools.partial(_swa_conv_kernel, layer=layer),
        grid=(batch, nb),
        in_specs=[
            pl.BlockSpec(memory_space=pltpu.SMEM),
            pl.BlockSpec((WINDOW, SWA_Q), lambda b, i: (rows(b, i), COL_SQ // SWA_Q)),
            pl.BlockSpec((WINDOW, SWA_KV), lambda b, i: (rows(b, i), COL_SK // SWA_KV)),
            pl.BlockSpec((WINDOW, SWA_KV), lambda b, i: (prev_rows(b, i), COL_SK // SWA_KV)),
            pl.BlockSpec((WINDOW, SWA_KV), lambda b, i: (rows(b, i), COL_SV // SWA_KV)),
            pl.BlockSpec((WINDOW, SWA_KV), lambda b, i: (prev_rows(b, i), COL_SV // SWA_KV)),
            conv_seg(COL_CB), conv_seg(COL_CC), conv_seg(COL_CH),
            pl.BlockSpec((None, CONV_K, CONV_WIDTH), lambda b, i: (layer, 0, 0)),
        ],
        out_specs=[
            pl.BlockSpec((WINDOW, SWA_Q), lambda b, i: (rows(b, i), 0)),
            pl.BlockSpec((WINDOW, CONV_WIDTH), lambda b, i: (rows(b, i), 0)),
            pl.BlockSpec((1, CONV_K - 1, CONV_WIDTH), lambda b, i: (b, 0, 0)),
        ],
        out_shape=[
            jax.ShapeDtypeStruct((batch * seq, SWA_Q), BF16),
            jax.ShapeDtypeStruct((batch * seq, CONV_WIDTH), BF16),
            jax.ShapeDtypeStruct((batch, CONV_K - 1, CONV_WIDTH), F32),
        ],
        scratch_shapes=[pltpu.VMEM((SWA_HEADS, 2 * WINDOW, WINDOW), F32),
                        pltpu.VMEM((SUBLANES, CONV_WIDTH), F32)],
        compiler_params=_params(("parallel", "arbitrary")),
        name="swa_conv_prompt",
    )(sinks, y, y, y, y, y, y, y, y, conv_w)


SAMPLE_STEP = SUBLANES


def _sample_kernel(sinks_ref, q_ref, k_ref, v_ref, gr_ref, glr_ref, cb_ref, cc_ref, ch_ref,
                   sq_ref, sk_ref, sv_ref, wlr_ref, blr_ref, gn_ref, cw_ref,
                   s_ref, cs_ref, kc_ref, vc_ref, *rest, layer, n_carried):
    oa_ref, ob_ref, oc_ref, so_ref, cso_ref, kco_ref, vco_ref, o_scr = rest[n_carried:]
    nb = SAMPLE_STEP
    b = _forget_gate_log(glr_ref, wlr_ref, blr_ref)
    q = q_ref[...] * (GLA_DK ** -0.5)
    k = k_ref[...]
    v = v_ref[...]
    qt = q * jnp.exp(b)
    kt = k * jnp.exp(-b)
    kd = k * jnp.exp(b - b)
    g_last = jnp.exp(b)
    qt_t = qt.T
    kd_t = kd.T
    gl_t = g_last.T
    for h in range(GLA_HEADS):
        ks = slice(h * GLA_DK, (h + 1) * GLA_DK)
        vs = slice(h * GLA_DV, (h + 1) * GLA_DV)
        a = jnp.sum(qt[:, ks] * kt[:, ks], axis=-1, keepdims=True)
        o_intra = a * v[:, vs]
        for i in range(nb):
            s_in = s_ref[i, h]
            v_row = v[i:i + 1, vs]
            o = jnp.sum(qt_t[ks, i:i + 1] * s_in, axis=0, keepdims=True)
            o_scr[i:i + 1, vs] = o + o_intra[i:i + 1, :]
            so_ref[i, h] = gl_t[ks, i:i + 1] * s_in + kd_t[ks, i:i + 1] * v_row
    gn = gn_ref[...]
    for h in range(GLA_HEADS):
        vs = slice(h * GLA_DV, (h + 1) * GLA_DV)
        oa_ref[:, vs] = _rmsnorm_rows(o_scr[:, vs], gn) * _silu(gr_ref[:, vs])

    u_new = cc_ref[...] * ch_ref[...]
    w = cw_ref[...]
    p0 = cs_ref[:, :CONV_WIDTH]
    p1 = cs_ref[:, CONV_WIDTH:]
    conv = w[0:1, :] * p0 + w[1:2, :] * p1 + w[2:3, :] * u_new
    ob_ref[...] = cb_ref[...] * conv
    cso_ref[:, :CONV_WIDTH] = p1
    cso_ref[:, CONV_WIDTH:] = u_new

    slot = lax.broadcasted_iota(jnp.int32, (SWA_HD, WINDOW), 1)
    key = lax.broadcasted_iota(jnp.int32, (SUBLANES, WINDOW), 1)
    grp = lax.broadcasted_iota(jnp.int32, (SUBLANES, WINDOW), 0)
    dist = (WINDOW - 1 - key).astype(F32)
    bias, sink = [], []
    for j in range(SWA_KV_HEADS):
        bj = jnp.zeros((SUBLANES, WINDOW), F32)
        sj = jnp.zeros((SUBLANES, 1), F32)
        for g in range(SWA_GROUP):
            bj = jnp.where(grp == g, -ALIBI_SLOPES[j * SWA_GROUP + g] * dist, bj)
            sj = jnp.where(grp[:, 0:1] == g, sinks_ref[layer, j * SWA_GROUP + g], sj)
        bias.append(bj)
        sink.append(sj)
    sk_t = sk_ref[...].T
    sv_t = sv_ref[...].T
    k_new, v_new = [], []
    for i in range(nb):
        for j in range(SWA_KV_HEADS):
            hd = slice(j * SWA_HD, (j + 1) * SWA_HD)
            kn = jnp.where(slot == WINDOW - 1, sk_t[hd, i:i + 1],
                           pltpu.roll(kc_ref[i, j], WINDOW - 1, 1))
            vn = jnp.where(slot == WINDOW - 1, sv_t[hd, i:i + 1],
                           pltpu.roll(vc_ref[i, j], WINDOW - 1, 1))
            kco_ref[i, j] = kn
            vco_ref[i, j] = vn
            k_new.append(kn.astype(BF16))
            v_new.append(vn.astype(BF16))
    scores = []
    for i in range(nb):
        for j in range(SWA_KV_HEADS):
            qj = (sq_ref[i, j] * SWA_SCALE).astype(BF16)
            scores.append(jnp.dot(qj, k_new[i * SWA_KV_HEADS + j], preferred_element_type=F32))
    probs = []
    for i in range(nb):
        for j in range(SWA_KV_HEADS):
            s = scores[i * SWA_KV_HEADS + j] + bias[j]
            m = jnp.maximum(jnp.max(s, axis=-1, keepdims=True), sink[j])
            e = jnp.exp(s - m)
            den = jnp.sum(e, axis=-1, keepdims=True) + jnp.exp(sink[j] - m)
            probs.append((e * (1.0 / den)).astype(BF16))
    for i in range(nb):
        for j in range(SWA_KV_HEADS):
            n = i * SWA_KV_HEADS + j
            oc_ref[i, j] = lax.dot_general(probs[n], v_new[n], NT_DIMS,
                                           preferred_element_type=F32)


def _sample_mixer(y, glr, sinks, w_lr_p, b_lr, gla_norm, conv_w,
                  state_gla, state_conv, cache_k, cache_v, carried, layer):
    nb = y.shape[0]
    st = SAMPLE_STEP
    sq = y[:, COL_SQ:COL_SQ + SWA_Q].reshape(nb, SWA_KV_HEADS, SWA_GROUP, SWA_HD)
    sq = jnp.pad(sq, ((0, 0), (0, 0), (0, SUBLANES - SWA_GROUP), (0, 0)))

    def seg(width, col):
        return pl.BlockSpec((st, width), lambda b: (b, col // width))

    def per_layer(shape):
        return pl.BlockSpec((None,) + shape, lambda b: (layer,) + (0,) * len(shape))

    def state(shape):
        return pl.BlockSpec((None, st) + shape, lambda b: (layer, b) + (0,) * len(shape))

    def out(shape):
        return pl.BlockSpec((st,) + shape, lambda b: (b,) + (0,) * len(shape))

    cache_shape = (SWA_KV_HEADS, SWA_HD, WINDOW)
    state_shapes = ((GLA_HEADS, GLA_DK, GLA_DV), ((CONV_K - 1) * CONV_WIDTH,),
                    cache_shape, cache_shape)
    n_fixed_inputs = 20
    res = pl.pallas_call(
        functools.partial(_sample_kernel, layer=layer, n_carried=len(carried)),
        grid=(nb // st,),
        in_specs=[
            pl.BlockSpec(memory_space=pltpu.SMEM),
            seg(GLA_QK, COL_Q), seg(GLA_QK, COL_K), seg(GLA_V, COL_V), seg(GLA_V, COL_GR),
            pl.BlockSpec((st, LANES), lambda b: (b, 0)),
            seg(CONV_WIDTH, COL_CB), seg(CONV_WIDTH, COL_CC), seg(CONV_WIDTH, COL_CH),
            out((SWA_KV_HEADS, SUBLANES, SWA_HD)), seg(SWA_KV, COL_SK), seg(SWA_KV, COL_SV),
            per_layer((LANES, GLA_QK)), per_layer((1, GLA_QK)), per_layer((1, GLA_DV)),
            per_layer((CONV_K, CONV_WIDTH)),
        ] + [state(s) for s in state_shapes] + [pl.BlockSpec(memory_space=pl.ANY)] * len(carried),
        out_specs=[
            out((GLA_V,)), out((CONV_WIDTH,)), out((SWA_KV_HEADS, SUBLANES, SWA_HD)),
        ] + [state(s) for s in state_shapes],
        out_shape=[
            jax.ShapeDtypeStruct((nb, GLA_V), F32),
            jax.ShapeDtypeStruct((nb, CONV_WIDTH), F32),
            jax.ShapeDtypeStruct((nb, SWA_KV_HEADS, SUBLANES, SWA_HD), F32),
        ] + [jax.ShapeDtypeStruct((DEPTH, nb) + s, F32) for s in state_shapes],
        input_output_aliases={n_fixed_inputs + n: 3 + n for n in range(len(carried))},
        scratch_shapes=[pltpu.VMEM((st, GLA_V), F32)],
        compiler_params=_params(("parallel",)),
        name="sample_mixer",
    )(sinks, y, y, y, y, glr, y, y, y, sq, y, y,
      w_lr_p, b_lr, gla_norm, conv_w, state_gla, state_conv, cache_k, cache_v, *carried)
    oa, ob, oc = res[:3]
    oc = oc[:, :, :SWA_GROUP, :].reshape(nb, SWA_Q)
    return oa, ob, oc, tuple(res[3:])


def _merge_kernel(*refs, n_ptiles):
    (pa, pb, pc, pg0, pg1, pg2, px_hbm, sa, sb, sc, sg0, sg1, sg2, sx, wb_ref, wo_ref,
     po_ref, so_ref, x_sem) = refs
    i = pl.program_id(0)
    t = pl.program_id(1)
    tm = po_ref.shape[0]

    def mix(a_ref, b_ref, c_ref, g0_ref, g1_ref, g2_ref):
        def branch(src_ref, gate_ref, n):
            br = jnp.dot(src_ref[...].astype(BF16), wb_ref[n], preferred_element_type=F32)
            return jax.nn.sigmoid(gate_ref[...].astype(F32)) * br

        mixed = branch(a_ref, g0_ref, 0) + branch(b_ref, g1_ref, 1) + branch(c_ref, g2_ref, 2)
        return mixed.astype(BF16)

    @pl.when(i < n_ptiles)
    def _():
        x_copy = pltpu.make_async_copy(px_hbm.at[pl.ds(i * tm, tm), :], po_ref, x_sem)

        @pl.when(t == 0)
        def _():
            x_copy.start()

        mixed = mix(pa, pb, pc, pg0, pg1, pg2)

        @pl.when(t == 0)
        def _():
            x_copy.wait()

        po_ref[...] += jnp.dot(mixed, wo_ref[...], preferred_element_type=F32)

    @pl.when(i == n_ptiles)
    def _():
        @pl.when(t == 0)
        def _():
            so_ref[...] = sx[...]

        so_ref[...] += jnp.dot(mix(sa, sb, sc, sg0, sg1, sg2), wo_ref[...],
                               preferred_element_type=F32)


def _merge(prompt, sample, w_branch, w_out, layer, tm, tn):
    mp, ms = prompt[4].shape[0], sample[4].shape[0]
    n_ptiles = mp // tm
    prow = lambda i: jnp.minimum(i, n_ptiles - 1)
    gate_col = lambda n: (n * D_MODEL) // tn
    p_src = pl.BlockSpec((tm, BRANCH_WIDTH), lambda i, t: (prow(i), 0))
    p_gate = lambda n: pl.BlockSpec((tm, tn), lambda i, t: (prow(i), gate_col(n) + t))
    s_src = pl.BlockSpec((ms, BRANCH_WIDTH), lambda i, t: (0, 0))
    s_gate = lambda n: pl.BlockSpec((ms, tn), lambda i, t: (0, gate_col(n) + t))
    return pl.pallas_call(
        functools.partial(_merge_kernel, n_ptiles=n_ptiles),
        grid=(n_ptiles + 1, D_MODEL // tn),
        in_specs=[
            p_src, p_src, p_src, p_gate(0), p_gate(1), p_gate(2),
            pl.BlockSpec(memory_space=pl.ANY),
            s_src, s_src, s_src, s_gate(0), s_gate(1), s_gate(2),
            pl.BlockSpec((ms, D_MODEL), lambda i, t: (0, 0)),
            pl.BlockSpec((None, N_BRANCH, BRANCH_WIDTH, tn), lambda i, t: (layer, 0, 0, t)),
            pl.BlockSpec((None, tn, D_MODEL), lambda i, t: (layer, t, 0)),
        ],
        out_specs=[
            pl.BlockSpec((tm, D_MODEL), lambda i, t: (prow(i), 0)),
            pl.BlockSpec((ms, D_MODEL), lambda i, t: (0, 0)),
        ],
        out_shape=[
            jax.ShapeDtypeStruct((mp, D_MODEL), F32),
            jax.ShapeDtypeStruct((ms, D_MODEL), F32),
        ],
        scratch_shapes=[pltpu.SemaphoreType.DMA(())],
        compiler_params=_params(("arbitrary", "arbitrary")),
        name="merge_outproj",
    )(prompt[0], prompt[1], prompt[2], prompt[3], prompt[3], prompt[3], prompt[4],
      sample[0], sample[1], sample[2], sample[3], sample[3], sample[3], sample[4],
      w_branch, w_out)


def _mlp_kernel(xp_ref, xs_ref, g_ref, wu_ref, wd_ref, gnext_ref, wlr_ref, *rest,
                last, n_ptiles):
    if last:
        po, so, hp_scr, hs_scr, xp_scr, x_sem = rest
        p_out, s_out = (po,), (so,)
    else:
        po, php, pglr, so, shp, sglr, hp_scr, hs_scr, xp_scr, x_sem = rest
        p_out, s_out = (po, php, pglr), (so, shp, sglr)
    i = pl.program_id(0)
    f = pl.program_id(1)
    tm = xp_scr.shape[0]

    def x_copy(tile_index):
        return pltpu.make_async_copy(xp_ref.at[pl.ds(tile_index * tm, tm), :], xp_scr, x_sem)

    @pl.when(jnp.logical_and(i == 0, f == 0))
    def _():
        x_copy(0).start()

    @pl.when(jnp.logical_and(i < n_ptiles, f == 0))
    def _():
        x_copy(i).wait()

    def tile(x_ref, h_ref, o_ref, hn_ref=None, glr_ref=None):
        @pl.when(f == 0)
        def _():
            x = x_ref[...]
            h_ref[...] = _rmsnorm_rows(x, g_ref[...]).astype(BF16)
            o_ref[...] = x

        up = jnp.dot(h_ref[...], wu_ref[...], preferred_element_type=F32)
        act = jnp.square(jnp.maximum(up, 0.0)).astype(BF16)
        o_ref[...] += jnp.dot(act, wd_ref[...], preferred_element_type=F32)

        @pl.when(f == pl.num_programs(1) - 1)
        def _():
            xn = _rmsnorm_rows(o_ref[...], gnext_ref[...])
            if last:
                o_ref[...] = xn
            else:
                hn = xn.astype(BF16)
                hn_ref[...] = hn
                glr_ref[...] = _lr_project(hn, wlr_ref)

    @pl.when(i < n_ptiles)
    def _():
        tile(xp_scr, hp_scr, *p_out)

    @pl.when(jnp.logical_and(i + 1 < n_ptiles, f == 1))
    def _():
        x_copy(i + 1).start()

    @pl.when(i == n_ptiles)
    def _():
        tile(xs_ref, hs_scr, *s_out)


def _mlp(xp, xs, norm_w, w_up, w_down, next_norm_w, w_in_t, layer, tm, tf):
    mp, ms = xp.shape[0], xs.shape[0]
    n_ptiles = mp // tm
    last = layer == DEPTH - 1
    prow = lambda i: jnp.minimum(i, n_ptiles - 1)

    def group_out(m, rows, row_map):
        specs = [pl.BlockSpec((rows, D_MODEL), lambda i, f: (row_map(i), 0))]
        shapes = [jax.ShapeDtypeStruct((m, D_MODEL), F32)]
        if not last:
            specs += [pl.BlockSpec((rows, D_MODEL), lambda i, f: (row_map(i), 0)),
                      pl.BlockSpec((rows, LANES), lambda i, f: (row_map(i), 0))]
            shapes += [jax.ShapeDtypeStruct((m, D_MODEL), BF16),
                       jax.ShapeDtypeStruct((m, LANES), F32)]
        return specs, shapes

    p_specs, p_shapes = group_out(mp, tm, prow)
    s_specs, s_shapes = group_out(ms, ms, lambda i: 0)
    if last:
        next_norm_spec = pl.BlockSpec((1, D_MODEL), lambda i, f: (0, 0))
        lr_layer = layer
    else:
        next_norm_spec = pl.BlockSpec((None, 1, D_MODEL), lambda i, f: (layer + 1, 0, 0))
        lr_layer = layer + 1
    return pl.pallas_call(
        functools.partial(_mlp_kernel, last=last, n_ptiles=n_ptiles),
        grid=(n_ptiles + 1, D_FF // tf),
        in_specs=[
            pl.BlockSpec(memory_space=pl.ANY),
            pl.BlockSpec((ms, D_MODEL), lambda i, f: (0, 0)),
            pl.BlockSpec((None, 1, D_MODEL), lambda i, f: (layer, 0, 0)),
            pl.BlockSpec((None, D_MODEL, tf), lambda i, f: (layer, 0, f)),
            pl.BlockSpec((None, tf, D_MODEL), lambda i, f: (layer, f, 0)),
            next_norm_spec,
            _lr_weight_spec(lr_layer),
        ],
        out_specs=p_specs + s_specs,
        out_shape=p_shapes + s_shapes,
        scratch_shapes=[pltpu.VMEM((tm, D_MODEL), BF16), pltpu.VMEM((ms, D_MODEL), BF16),
                        pltpu.VMEM((tm, D_MODEL), F32), pltpu.SemaphoreType.DMA(())],
        compiler_params=_params(("arbitrary", "arbitrary")),
        name="mlp",
    )(xp, xs, norm_w, w_up, w_down, next_norm_w, w_in_t)


def kernel(x_prompt, x_sample, state_gla, state_conv, cache_k, cache_v, w_in, w_lr, b_lr,
           gla_norm, conv_w, attn_sinks, w_branch, w_out, norm_mix, norm_mlp, w_up, w_down,
           norm_final):
    batch, seq, _ = x_prompt.shape
    nb = x_sample.shape[0]
    mp = batch * seq

    w_lr_p = jnp.pad(w_lr, ((0, 0), (0, LANES - GLA_RANK), (0, 0)))
    w_branch_b = w_branch.astype(BF16)
    w_out_b = w_out.astype(BF16)
    w_up_b = w_up.astype(BF16)
    w_down_b = w_down.astype(BF16)
    b_lr3 = b_lr.reshape(DEPTH, 1, GLA_QK)
    gla_norm3 = gla_norm.reshape(DEPTH, 1, GLA_DV)
    norm_mix3 = norm_mix.reshape(DEPTH, 1, D_MODEL)
    norm_mlp3 = norm_mlp.reshape(DEPTH, 1, D_MODEL)
    norm_final2 = norm_final.reshape(1, D_MODEL)
    state_conv2 = state_conv.reshape(DEPTH, nb, (CONV_K - 1) * CONV_WIDTH)
    cache_k4 = jnp.transpose(cache_k, (0, 1, 3, 4, 2))
    cache_v4 = jnp.transpose(cache_v, (0, 1, 3, 4, 2))

    w_in_t = jnp.swapaxes(w_in, 1, 2)
    tm = min(1024, mp)

    xp = x_prompt.reshape(mp, D_MODEL)
    xs = x_sample.reshape(nb, D_MODEL)
    hp, glrp = _norm(xp, norm_mix3, w_in_t, 0, tm=tm)
    hs, glrs = _norm(xs, norm_mix3, w_in_t, 0, tm=nb)
    outs = [[] for _ in range(4)]
    sample_states = ()
    for l in range(DEPTH):
        next_norm = norm_final2 if l == DEPTH - 1 else norm_mix3
        yp, gp, ys, gs = _inproj(hp, hs, w_in_t, l, tm=min(512, mp), tn=1536)
        oa, sg_p = _gla_prompt(yp, glrp, w_lr_p, b_lr3, gla_norm3, l, batch, seq)
        oc, ob, sc_p = _swa_conv_prompt(yp, attn_sinks, conv_w, l, batch, seq)
        yp3 = yp.reshape(batch, seq, COL_GATES)
        kp = yp3[:, seq - WINDOW:, COL_SK:COL_SK + SWA_KV]
        vp = yp3[:, seq - WINDOW:, COL_SV:COL_SV + SWA_KV]
        sa, sb, sc, sample_states = _sample_mixer(
            ys, glrs, attn_sinks, w_lr_p, b_lr3, gla_norm3, conv_w,
            state_gla, state_conv2, cache_k4, cache_v4, sample_states, l)
        xp, xs = _merge((oa, ob, oc, gp, xp), (sa, sb, sc, gs, xs), w_branch_b, w_out_b, l,
                        tm=tm, tn=512)
        res = _mlp(xp, xs, norm_mlp3, w_up_b, w_down_b, next_norm, w_in_t, l, tm=tm, tf=512)
        if l == DEPTH - 1:
            xp, xs = res
        else:
            xp, hp, glrp, xs, hs, glrs = res
        for lst, val in zip(outs, (
                sg_p, sc_p,
                kp.reshape(batch, WINDOW, SWA_KV_HEADS, SWA_HD),
                vp.reshape(batch, WINDOW, SWA_KV_HEADS, SWA_HD))):
            lst.append(val)
    y_prompt = xp.reshape(batch, seq, D_MODEL)
    y_sample = xs.reshape(nb, 1, D_MODEL)
    sg_p, sc_p, kp, vp = (jnp.stack(o) for o in outs)
    sg_s, sc_s, ks, vs = sample_states
    return (y_prompt, y_sample, sg_p, sg_s, sc_p,
            sc_s.reshape(DEPTH, nb, CONV_K - 1, CONV_WIDTH), kp,
            jnp.transpose(ks, (0, 1, 4, 2, 3)), vp, jnp.transpose(vs, (0, 1, 4, 2, 3)))
```

```python
import functools

import jax
import jax.numpy as jnp
from jax import lax
from jax.experimental import pallas as pl
from jax.experimental.pallas import tpu as pltpu

F32 = jnp.float32
BF16 = jnp.bfloat16

D_MODEL = 2048
DEPTH = 4
PAST_LEN = 16384
BRANCH_WIDTH = D_MODEL // 2
N_BRANCH = 3
GLA_HEADS = 4
GLA_DV = BRANCH_WIDTH // GLA_HEADS
GLA_DK = GLA_DV // 2
GLA_RANK = 16
GLA_TAU = 16.0
GLA_CHUNK = 64
CONV_WIDTH = BRANCH_WIDTH
CONV_K = 3
SWA_HD = 64
SWA_HEADS = BRANCH_WIDTH // SWA_HD
SWA_KV_HEADS = SWA_HEADS // 4
SWA_GROUP = SWA_HEADS // SWA_KV_HEADS
WINDOW = 128
D_FF = 4 * D_MODEL
EPS = 1e-6

LANES = 128
SUBLANES = 8
VMEM_LIMIT = 56 * 1024 * 1024

GLA_QK = GLA_HEADS * GLA_DK
GLA_V = GLA_HEADS * GLA_DV
SWA_Q = SWA_HEADS * SWA_HD
SWA_KV = SWA_KV_HEADS * SWA_HD

COL_Q = 0
COL_K = COL_Q + GLA_QK
COL_V = COL_K + GLA_QK
COL_GR = COL_V + GLA_V
COL_CB = COL_GR + GLA_V
COL_CC = COL_CB + CONV_WIDTH
COL_CH = COL_CC + CONV_WIDTH
COL_SQ = COL_CH + CONV_WIDTH
COL_SK = COL_SQ + SWA_Q
COL_SV = COL_SK + SWA_KV
COL_GATES = COL_SV + SWA_KV
PACKED_COLS = COL_GATES + N_BRANCH * D_MODEL
LR_SRC = COL_CB

ALIBI_SLOPES = tuple(2.0 ** (-8.0 * (h + 1) / SWA_HEADS) for h in range(SWA_HEADS))
SWA_SCALE = SWA_HD ** -0.5

NT_DIMS = (((1,), (1,)), ((), ()))
TN_DIMS = (((0,), (0,)), ((), ()))


def _params(sem):
    return pltpu.CompilerParams(dimension_semantics=sem, vmem_limit_bytes=VMEM_LIMIT)


def _rmsnorm_rows(x, g):
    ms = jnp.mean(x * x, axis=-1, keepdims=True)
    return x * lax.rsqrt(ms + EPS) * g


def _log_sigmoid(z):
    return jnp.minimum(z, 0.0) - jnp.log(1.0 + jnp.exp(-jnp.abs(z)))


def _forget_gate_log(glr_ref, wlr_ref, blr_ref):
    z = jnp.dot(glr_ref[...].astype(BF16), wlr_ref[...].astype(BF16),
                preferred_element_type=F32) + blr_ref[...]
    return _log_sigmoid(z) / GLA_TAU


def _split_bf16x3(x):
    hi = x.astype(BF16)
    r = x - hi.astype(F32)
    mid = r.astype(BF16)
    lo = (r - mid.astype(F32)).astype(BF16)
    return hi, mid, lo


def _silu(x):
    return x * jax.nn.sigmoid(x)


def _lr_weight_spec(layer):
    return pl.BlockSpec((None, LANES, D_MODEL), lambda *a: (layer, LR_SRC // LANES, 0))


def _lr_project(h, wlr_ref):
    return lax.dot_general(h, wlr_ref[...].astype(BF16), NT_DIMS, preferred_element_type=F32)


def _norm_kernel(x_ref, g_ref, wlr_ref, h_ref, glr_ref):
    h = _rmsnorm_rows(x_ref[...], g_ref[...]).astype(BF16)
    h_ref[...] = h
    glr_ref[...] = _lr_project(h, wlr_ref)


def _norm(x, norm_w, w_in_t, layer, tm):
    m = x.shape[0]
    return pl.pallas_call(
        _norm_kernel,
        grid=(m // tm,),
        in_specs=[
            pl.BlockSpec((tm, D_MODEL), lambda i: (i, 0)),
            pl.BlockSpec((None, 1, D_MODEL), lambda i: (layer, 0, 0)),
            _lr_weight_spec(layer),
        ],
        out_specs=[
            pl.BlockSpec((tm, D_MODEL), lambda i: (i, 0)),
            pl.BlockSpec((tm, LANES), lambda i: (i, 0)),
        ],
        out_shape=[
            jax.ShapeDtypeStruct((m, D_MODEL), BF16),
            jax.ShapeDtypeStruct((m, LANES), F32),
        ],
        compiler_params=_params(("parallel",)),
        name="norm",
    )(x, norm_w, w_in_t)


def _inproj_kernel(hp_ref, hs_ref, w_hbm, yp_ref, gp_ref, ys_ref, gs_ref, wf32_scr, wbf_ref,
                   w_sem, *, layer, n_main, n_ptiles, n_plain):
    j = pl.program_id(0)
    i = pl.program_id(1)
    tn = wbf_ref.shape[0]

    def w_copy(tile):
        row = pl.multiple_of(tile * tn + jnp.where(tile >= n_plain, GLA_RANK, 0), GLA_RANK)
        return pltpu.make_async_copy(w_hbm.at[layer, pl.ds(row, tn), :], wf32_scr, w_sem)

    @pl.when(i == 0)
    def _():
        @pl.when(j == 0)
        def _():
            w_copy(0).start()

        w_copy(j).wait()
        wbf_ref[...] = wf32_scr[...].astype(BF16)

        @pl.when(j + 1 < pl.num_programs(0))
        def _():
            w_copy(j + 1).start()

    def project(h_ref):
        return lax.dot_general(h_ref[...], wbf_ref[...], NT_DIMS, preferred_element_type=F32)

    is_main = j < n_main
    is_prompt = i < n_ptiles

    @pl.when(jnp.logical_and(is_main, is_prompt))
    def _():
        yp_ref[...] = project(hp_ref)

    @pl.when(jnp.logical_and(jnp.logical_not(is_main), is_prompt))
    def _():
        gp_ref[...] = project(hp_ref).astype(BF16)

    @pl.when(jnp.logical_and(is_main, jnp.logical_not(is_prompt)))
    def _():
        ys_ref[...] = project(hs_ref)

    @pl.when(jnp.logical_and(jnp.logical_not(is_main), jnp.logical_not(is_prompt)))
    def _():
        gs_ref[...] = project(hs_ref).astype(BF16)


def _inproj(hp, hs, w_in_t, layer, tm, tn):
    mp, ms = hp.shape[0], hs.shape[0]
    n_ptiles = mp // tm
    n_main = COL_GATES // tn
    n_gate = (N_BRANCH * D_MODEL) // tn
    n_plain = LR_SRC // tn
    last_p = n_ptiles - 1
    prow = lambda i: jnp.minimum(i, last_p)
    return pl.pallas_call(
        functools.partial(_inproj_kernel, layer=layer, n_main=n_main, n_ptiles=n_ptiles,
                          n_plain=n_plain),
        grid=(n_main + n_gate, n_ptiles + 1),
        in_specs=[
            pl.BlockSpec((tm, D_MODEL), lambda j, i: (prow(i), 0)),
            pl.BlockSpec((ms, D_MODEL), lambda j, i: (0, 0)),
            pl.BlockSpec(memory_space=pl.ANY),
        ],
        out_specs=[
            pl.BlockSpec((tm, tn), lambda j, i: (jnp.where(j < n_main, prow(i), last_p),
                                                 jnp.minimum(j, n_main - 1))),
            pl.BlockSpec((tm, tn), lambda j, i: (jnp.where(j < n_main, 0, prow(i)),
                                                 jnp.maximum(j - n_main, 0))),
            pl.BlockSpec((ms, tn), lambda j, i: (0, jnp.minimum(j, n_main - 1))),
            pl.BlockSpec((ms, tn), lambda j, i: (0, jnp.maximum(j - n_main, 0))),
        ],
        out_shape=[
            jax.ShapeDtypeStruct((mp, COL_GATES), F32),
            jax.ShapeDtypeStruct((mp, N_BRANCH * D_MODEL), BF16),
            jax.ShapeDtypeStruct((ms, COL_GATES), F32),
            jax.ShapeDtypeStruct((ms, N_BRANCH * D_MODEL), BF16),
        ],
        scratch_shapes=[pltpu.VMEM((tn, D_MODEL), F32), pltpu.VMEM((tn, D_MODEL), BF16),
                        pltpu.SemaphoreType.DMA(())],
        compiler_params=_params(("arbitrary", "arbitrary")),
        name="inproj",
    )(hp, hs, w_in_t)


GLA_STEP_CHUNKS = 4


def _gla_kernel(q_ref, k_ref, v_ref, gr_ref, glr_ref, wlr_ref, blr_ref, gn_ref,
                oa_ref, sfin_ref, st_ref):
    c = pl.program_id(1)
    cs = GLA_CHUNK

    @pl.when(c == 0)
    def _():
        st_ref[...] = jnp.zeros_like(st_ref)

    log_a = _forget_gate_log(glr_ref, wlr_ref, blr_ref)
    tr = GLA_STEP_CHUNKS * cs
    row = lax.broadcasted_iota(jnp.int32, (tr, tr), 0)
    col = lax.broadcasted_iota(jnp.int32, (tr, tr), 1)
    log2_cs = cs.bit_length() - 1
    same_chunk = jnp.right_shift(row, log2_cs) == jnp.right_shift(col, log2_cs)
    causal = jnp.logical_and(same_chunk, row >= col)
    tri = jnp.where(causal, 1.0, 0.0).astype(BF16)
    gn = gn_ref[...]

    b = sum(jnp.dot(tri, part, preferred_element_type=F32) for part in _split_bf16x3(log_a))
    b_last = jnp.concatenate(
        [jnp.broadcast_to(b[(n + 1) * cs - 1:(n + 1) * cs, :], (cs, b.shape[1]))
         for n in range(GLA_STEP_CHUNKS)], axis=0)
    q = q_ref[...] * (GLA_DK ** -0.5)
    k = k_ref[...]
    qt = (q * jnp.exp(b)).astype(BF16)
    kt = (k * jnp.exp(-b)).astype(BF16)
    kd = (k * jnp.exp(b_last - b)).astype(BF16)
    g_last = jnp.exp(b_last)
    v = v_ref[...].astype(BF16)

    for h in range(GLA_HEADS):
        ks = slice(h * GLA_DK, (h + 1) * GLA_DK)
        vs = slice(h * GLA_DV, (h + 1) * GLA_DV)
        a = lax.dot_general(qt[:, ks], kt[:, ks], NT_DIMS, preferred_element_type=F32)
        a = jnp.where(causal, a, 0.0).astype(BF16)
        o_intra = jnp.dot(a, v[:, vs], preferred_element_type=F32)
        u_t = [lax.dot_general(v[n * cs:(n + 1) * cs, vs], kd[n * cs:(n + 1) * cs, ks], TN_DIMS,
                               preferred_element_type=F32) for n in range(GLA_STEP_CHUNKS)]
        s_t = st_ref[h]
        for n in range(GLA_STEP_CHUNKS):
            rs = slice(n * cs, (n + 1) * cs)
            o = lax.dot_general(qt[rs, ks], s_t.astype(BF16), NT_DIMS,
                                preferred_element_type=F32) + o_intra[rs, :]
            s_t = g_last[n * cs:n * cs + 1, ks] * s_t + u_t[n]
            oa_ref[rs, vs] = (_rmsnorm_rows(o, gn) * _silu(gr_ref[rs, vs])).astype(BF16)
        st_ref[h] = s_t

    @pl.when(c == pl.num_programs(1) - 1)
    def _():
        for h in range(GLA_HEADS):
            sfin_ref[0, h] = st_ref[h].T


def _gla_prompt(y, glr, w_lr_p, b_lr, gla_norm, layer, batch, seq):
    tr = GLA_STEP_CHUNKS * GLA_CHUNK
    ns = seq // tr
    rows = lambda b, c: b * ns + c
    return pl.pallas_call(
        _gla_kernel,
        grid=(batch, ns),
        in_specs=[
            pl.BlockSpec((tr, GLA_QK), lambda b, c: (rows(b, c), COL_Q // GLA_QK)),
            pl.BlockSpec((tr, GLA_QK), lambda b, c: (rows(b, c), COL_K // GLA_QK)),
            pl.BlockSpec((tr, GLA_V), lambda b, c: (rows(b, c), COL_V // GLA_V)),
            pl.BlockSpec((tr, GLA_V), lambda b, c: (rows(b, c), COL_GR // GLA_V)),
            pl.BlockSpec((tr, LANES), lambda b, c: (rows(b, c), 0)),
            pl.BlockSpec((None, LANES, GLA_QK), lambda b, c: (layer, 0, 0)),
            pl.BlockSpec((None, 1, GLA_QK), lambda b, c: (layer, 0, 0)),
            pl.BlockSpec((None, 1, GLA_DV), lambda b, c: (layer, 0, 0)),
        ],
        out_specs=[
            pl.BlockSpec((tr, GLA_V), lambda b, c: (rows(b, c), 0)),
            pl.BlockSpec((1, GLA_HEADS, GLA_DK, GLA_DV), lambda b, c: (b, 0, 0, 0)),
        ],
        out_shape=[
            jax.ShapeDtypeStruct((batch * seq, GLA_V), BF16),
            jax.ShapeDtypeStruct((batch, GLA_HEADS, GLA_DK, GLA_DV), F32),
        ],
        scratch_shapes=[pltpu.VMEM((GLA_HEADS, GLA_DV, GLA_DK), F32)],
        compiler_params=_params(("parallel", "arbitrary")),
        name="gla_prompt",
    )(y, y, y, y, glr, w_lr_p, b_lr, gla_norm)


def _conv_tile(cb_ref, cc_ref, ch_ref, w_ref, ob_ref, cs_ref, prev_ref):
    @pl.when(pl.program_id(1) == 0)
    def _():
        prev_ref[...] = jnp.zeros_like(prev_ref)

    u = cc_ref[...] * ch_ref[...]
    tl = u.shape[0]
    prev = prev_ref[...]
    p_m1 = prev[SUBLANES - 1:SUBLANES, :]
    p_m2 = prev[SUBLANES - 2:SUBLANES - 1, :]
    row = lax.broadcasted_iota(jnp.int32, u.shape, 0)
    u1 = jnp.where(row == 0, p_m1, pltpu.roll(u, 1, 0))
    u2 = jnp.where(row == 0, p_m2, jnp.where(row == 1, p_m1, pltpu.roll(u, 2, 0)))
    w = w_ref[...]
    conv = w[0:1, :] * u2 + w[1:2, :] * u1 + w[2:3, :] * u
    ob_ref[...] = (cb_ref[...] * conv).astype(BF16)
    prev_ref[...] = u[tl - SUBLANES:tl, :]
    cs_ref[0] = u[tl - (CONV_K - 1):tl, :]


def _swa_conv_kernel(sinks_ref, q_ref, kc_ref, kp_ref, vc_ref, vp_ref,
                     cb_ref, cc_ref, ch_ref, cw_ref,
                     o_ref, ob_ref, cs_ref, bias_ref, prev_ref, *, layer):
    _conv_tile(cb_ref, cc_ref, ch_ref, cw_ref, ob_ref, cs_ref, prev_ref)
    blk = pl.program_id(1)

    key = lax.broadcasted_iota(jnp.int32, (2 * WINDOW, WINDOW), 0)

    @pl.when(blk == 0)
    def _():
        qry = lax.broadcasted_iota(jnp.int32, (2 * WINDOW, WINDOW), 1)
        dist = qry - key + WINDOW
        valid = jnp.logical_and(dist >= 0, dist < WINDOW)
        dist_f = dist.astype(F32)
        for h in range(SWA_HEADS):
            bias_ref[h] = jnp.where(valid, -ALIBI_SLOPES[h] * dist_f, -jnp.inf)

    key_exists = jnp.logical_or(blk > 0, key >= WINDOW)
    q = (q_ref[...] * SWA_SCALE).astype(BF16)
    k2 = jnp.concatenate([kp_ref[...], kc_ref[...]], axis=0).astype(BF16)
    v2_t = jnp.concatenate([vp_ref[...], vc_ref[...]], axis=0).T.astype(BF16)
    scores = []
    for h in range(SWA_HEADS):
        j = h // SWA_GROUP
        scores.append(lax.dot_general(k2[:, j * SWA_HD:(j + 1) * SWA_HD],
                                      q[:, h * SWA_HD:(h + 1) * SWA_HD], NT_DIMS,
                                      preferred_element_type=F32))
    probs = []
    for h in range(SWA_HEADS):
        s = jnp.where(key_exists, scores[h] + bias_ref[h], -jnp.inf)
        sink = sinks_ref[layer, h]
        m = jnp.maximum(jnp.max(s, axis=0, keepdims=True), sink)
        e = jnp.exp(s - m)
        den = jnp.sum(e, axis=0, keepdims=True) + jnp.exp(sink - m)
        probs.append((e * (1.0 / den)).astype(BF16))
    outs_t = []
    for h in range(SWA_HEADS):
        j = h // SWA_GROUP
        outs_t.append(jnp.dot(v2_t[j * SWA_HD:(j + 1) * SWA_HD, :], probs[h],
                              preferred_element_type=F32))
    o_ref[...] = jnp.concatenate(outs_t, axis=0).T.astype(BF16)


def _swa_conv_prompt(y, sinks, conv_w, layer, batch, seq):
    nb = seq // WINDOW
    rows = lambda b, i: b * nb + i
    prev_rows = lambda b, i: b * nb + jnp.maximum(i - 1, 0)
    conv_seg = lambda col: pl.BlockSpec((WINDOW, CONV_WIDTH),
                                        lambda b, i: (rows(b, i), col // CONV_WIDTH))
    return pl.pallas_call(
        functools.partial(_swa_conv_kernel, layer=layer),
        grid=(batch, nb),
        in_specs=[
            pl.BlockSpec(memory_space=pltpu.SMEM),
            pl.BlockSpec((WINDOW, SWA_Q), lambda b, i: (rows(b, i), COL_SQ // SWA_Q)),
            pl.BlockSpec((WINDOW, SWA_KV), lambda b, i: (rows(b, i), COL_SK // SWA_KV)),
            pl.BlockSpec((WINDOW, SWA_KV), lambda b, i: (prev_rows(b, i), COL_SK // SWA_KV)),
            pl.BlockSpec((WINDOW, SWA_KV), lambda b, i: (rows(b, i), COL_SV // SWA_KV)),
            pl.BlockSpec((WINDOW, SWA_KV), lambda b, i: (prev_rows(b, i), COL_SV // SWA_KV)),
            conv_seg(COL_CB), conv_seg(COL_CC), conv_seg(COL_CH),
            pl.BlockSpec((None, CONV_K, CONV_WIDTH), lambda b, i: (layer, 0, 0)),
        ],
        out_specs=[
            pl.BlockSpec((WINDOW, SWA_Q), lambda b, i: (rows(b, i), 0)),
            pl.BlockSpec((WINDOW, CONV_WIDTH), lambda b, i: (rows(b, i), 0)),
            pl.BlockSpec((1, CONV_K - 1, CONV_WIDTH), lambda b, i: (b, 0, 0)),
        ],
        out_shape=[
            jax.ShapeDtypeStruct((batch * seq, SWA_Q), BF16),
            jax.ShapeDtypeStruct((batch * seq, CONV_WIDTH), BF16),
            jax.ShapeDtypeStruct((batch, CONV_K - 1, CONV_WIDTH), F32),
        ],
        scratch_shapes=[pltpu.VMEM((SWA_HEADS, 2 * WINDOW, WINDOW), F32),
                        pltpu.VMEM((SUBLANES, CONV_WIDTH), F32)],
        compiler_params=_params(("parallel", "arbitrary")),
        name="swa_conv_prompt",
    )(sinks, y, y, y, y, y, y, y, y, conv_w)


SAMPLE_STEP = SUBLANES


def _sample_kernel(sinks_ref, q_ref, k_ref, v_ref, gr_ref, glr_ref, cb_ref, cc_ref, ch_ref,
                   sq_ref, sk_ref, sv_ref, wlr_ref, blr_ref, gn_ref, cw_ref,
                   s_ref, cs_ref, kc_ref, vc_ref, *rest, layer, n_carried):
    oa_ref, ob_ref, oc_ref, so_ref, cso_ref, kco_ref, vco_ref, o_scr = rest[n_carried:]
    nb = SAMPLE_STEP
    b = _forget_gate_log(glr_ref, wlr_ref, blr_ref)
    q = q_ref[...] * (GLA_DK ** -0.5)
    k = k_ref[...]
    v = v_ref[...]
    qt = q * jnp.exp(b)
    kt = k * jnp.exp(-b)
    kd = k * jnp.exp(b - b)
    g_last = jnp.exp(b)
    qt_t = qt.T
    kd_t = kd.T
    gl_t = g_last.T
    for h in range(GLA_HEADS):
        ks = slice(h * GLA_DK, (h + 1) * GLA_DK)
        vs = slice(h * GLA_DV, (h + 1) * GLA_DV)
        a = jnp.sum(qt[:, ks] * kt[:, ks], axis=-1, keepdims=True)
        o_intra = a * v[:, vs]
        for i in range(nb):
            s_in = s_ref[i, h]
            v_row = v[i:i + 1, vs]
            o = jnp.sum(qt_t[ks, i:i + 1] * s_in, axis=0, keepdims=True)
            o_scr[i:i + 1, vs] = o + o_intra[i:i + 1, :]
            so_ref[i, h] = gl_t[ks, i:i + 1] * s_in + kd_t[ks, i:i + 1] * v_row
    gn = gn_ref[...]
    for h in range(GLA_HEADS):
        vs = slice(h * GLA_DV, (h + 1) * GLA_DV)
        oa_ref[:, vs] = _rmsnorm_rows(o_scr[:, vs], gn) * _silu(gr_ref[:, vs])

    u_new = cc_ref[...] * ch_ref[...]
    w = cw_ref[...]
    p0 = cs_ref[:, :CONV_WIDTH]
    p1 = cs_ref[:, CONV_WIDTH:]
    conv = w[0:1, :] * p0 + w[1:2, :] * p1 + w[2:3, :] * u_new
    ob_ref[...] = cb_ref[...] * conv
    cso_ref[:, :CONV_WIDTH] = p1
    cso_ref[:, CONV_WIDTH:] = u_new

    slot = lax.broadcasted_iota(jnp.int32, (SWA_HD, WINDOW), 1)
    key = lax.broadcasted_iota(jnp.int32, (SUBLANES, WINDOW), 1)
    grp = lax.broadcasted_iota(jnp.int32, (SUBLANES, WINDOW), 0)
    dist = (WINDOW - 1 - key).astype(F32)
    bias, sink = [], []
    for j in range(SWA_KV_HEADS):
        bj = jnp.zeros((SUBLANES, WINDOW), F32)
        sj = jnp.zeros((SUBLANES, 1), F32)
        for g in range(SWA_GROUP):
            bj = jnp.where(grp == g, -ALIBI_SLOPES[j * SWA_GROUP + g] * dist, bj)
            sj = jnp.where(grp[:, 0:1] == g, sinks_ref[layer, j * SWA_GROUP + g], sj)
        bias.append(bj)
        sink.append(sj)
    sk_t = sk_ref[...].T
    sv_t = sv_ref[...].T
    k_new, v_new = [], []
    for i in range(nb):
        for j in range(SWA_KV_HEADS):
            hd = slice(j * SWA_HD, (j + 1) * SWA_HD)
            kn = jnp.where(slot == WINDOW - 1, sk_t[hd, i:i + 1],
                           pltpu.roll(kc_ref[i, j], WINDOW - 1, 1))
            vn = jnp.where(slot == WINDOW - 1, sv_t[hd, i:i + 1],
                           pltpu.roll(vc_ref[i, j], WINDOW - 1, 1))
            kco_ref[i, j] = kn
            vco_ref[i, j] = vn
            k_new.append(kn.astype(BF16))
            v_new.append(vn.astype(BF16))
    scores = []
    for i in range(nb):
        for j in range(SWA_KV_HEADS):
            qj = (sq_ref[i, j] * SWA_SCALE).astype(BF16)
            scores.append(jnp.dot(qj, k_new[i * SWA_KV_HEADS + j], preferred_element_type=F32))
    probs = []
    for i in range(nb):
        for j in range(SWA_KV_HEADS):
            s = scores[i * SWA_KV_HEADS + j] + bias[j]
            m = jnp.maximum(jnp.max(s, axis=-1, keepdims=True), sink[j])
            e = jnp.exp(s - m)
            den = jnp.sum(e, axis=-1, keepdims=True) + jnp.exp(sink[j] - m)
            probs.append((e * (1.0 / den)).astype(BF16))
    for i in range(nb):
        for j in range(SWA_KV_HEADS):
            n = i * SWA_KV_HEADS + j
            oc_ref[i, j] = lax.dot_general(probs[n], v_new[n], NT_DIMS,
                                           preferred_element_type=F32)


def _sample_mixer(y, glr, sinks, w_lr_p, b_lr, gla_norm, conv_w,
                  state_gla, state_conv, cache_k, cache_v, carried, layer):
    nb = y.shape[0]
    st = SAMPLE_STEP
    sq = y[:, COL_SQ:COL_SQ + SWA_Q].reshape(nb, SWA_KV_HEADS, SWA_GROUP, SWA_HD)
    sq = jnp.pad(sq, ((0, 0), (0, 0), (0, SUBLANES - SWA_GROUP), (0, 0)))

    def seg(width, col):
        return pl.BlockSpec((st, width), lambda b: (b, col // width))

    def per_layer(shape):
        return pl.BlockSpec((None,) + shape, lambda b: (layer,) + (0,) * len(shape))

    def state(shape):
        return pl.BlockSpec((None, st) + shape, lambda b: (layer, b) + (0,) * len(shape))

    def out(shape):
        return pl.BlockSpec((st,) + shape, lambda b: (b,) + (0,) * len(shape))

    cache_shape = (SWA_KV_HEADS, SWA_HD, WINDOW)
    state_shapes = ((GLA_HEADS, GLA_DK, GLA_DV), ((CONV_K - 1) * CONV_WIDTH,),
                    cache_shape, cache_shape)
    n_fixed_inputs = 20
    res = pl.pallas_call(
        functools.partial(_sample_kernel, layer=layer, n_carried=len(carried)),
        grid=(nb // st,),
        in_specs=[
            pl.BlockSpec(memory_space=pltpu.SMEM),
            seg(GLA_QK, COL_Q), seg(GLA_QK, COL_K), seg(GLA_V, COL_V), seg(GLA_V, COL_GR),
            pl.BlockSpec((st, LANES), lambda b: (b, 0)),
            seg(CONV_WIDTH, COL_CB), seg(CONV_WIDTH, COL_CC), seg(CONV_WIDTH, COL_CH),
            out((SWA_KV_HEADS, SUBLANES, SWA_HD)), seg(SWA_KV, COL_SK), seg(SWA_KV, COL_SV),
            per_layer((LANES, GLA_QK)), per_layer((1, GLA_QK)), per_layer((1, GLA_DV)),
            per_layer((CONV_K, CONV_WIDTH)),
        ] + [state(s) for s in state_shapes] + [pl.BlockSpec(memory_space=pl.ANY)] * len(carried),
        out_specs=[
            out((GLA_V,)), out((CONV_WIDTH,)), out((SWA_KV_HEADS, SUBLANES, SWA_HD)),
        ] + [state(s) for s in state_shapes],
        out_shape=[
            jax.ShapeDtypeStruct((nb, GLA_V), F32),
            jax.ShapeDtypeStruct((nb, CONV_WIDTH), F32),
            jax.ShapeDtypeStruct((nb, SWA_KV_HEADS, SUBLANES, SWA_HD), F32),
        ] + [jax.ShapeDtypeStruct((DEPTH, nb) + s, F32) for s in state_shapes],
        input_output_aliases={n_fixed_inputs + n: 3 + n for n in range(len(carried))},
        scratch_shapes=[pltpu.VMEM((st, GLA_V), F32)],
        compiler_params=_params(("parallel",)),
        name="sample_mixer",
    )(sinks, y, y, y, y, glr, y, y, y, sq, y, y,
      w_lr_p, b_lr, gla_norm, conv_w, state_gla, state_conv, cache_k, cache_v, *carried)
    oa, ob, oc = res[:3]
    oc = oc[:, :, :SWA_GROUP, :].reshape(nb, SWA_Q)
    return oa, ob, oc, tuple(res[3:])


def _merge_kernel(*refs, n_ptiles):
    (pa, pb, pc, pg0, pg1, pg2, px_hbm, sa, sb, sc, sg0, sg1, sg2, sx, wb_ref, wo_ref,
     po_ref, so_ref, x_sem) = refs
    i = pl.program_id(0)
    t = pl.program_id(1)
    tm = po_ref.shape[0]

    def mix(a_ref, b_ref, c_ref, g0_ref, g1_ref, g2_ref):
        def branch(src_ref, gate_ref, n):
            br = jnp.dot(src_ref[...].astype(BF16), wb_ref[n], preferred_element_type=F32)
            return jax.nn.sigmoid(gate_ref[...].astype(F32)) * br

        mixed = branch(a_ref, g0_ref, 0) + branch(b_ref, g1_ref, 1) + branch(c_ref, g2_ref, 2)
        return mixed.astype(BF16)

    @pl.when(i < n_ptiles)
    def _():
        x_copy = pltpu.make_async_copy(px_hbm.at[pl.ds(i * tm, tm), :], po_ref, x_sem)

        @pl.when(t == 0)
        def _():
            x_copy.start()

        mixed = mix(pa, pb, pc, pg0, pg1, pg2)

        @pl.when(t == 0)
        def _():
            x_copy.wait()

        po_ref[...] += jnp.dot(mixed, wo_ref[...], preferred_element_type=F32)

    @pl.when(i == n_ptiles)
    def _():
        @pl.when(t == 0)
        def _():
            so_ref[...] = sx[...]

        so_ref[...] += jnp.dot(mix(sa, sb, sc, sg0, sg1, sg2), wo_ref[...],
                               preferred_element_type=F32)


def _merge(prompt, sample, w_branch, w_out, layer, tm, tn):
    mp, ms = prompt[4].shape[0], sample[4].shape[0]
    n_ptiles = mp // tm
    prow = lambda i: jnp.minimum(i, n_ptiles - 1)
    gate_col = lambda n: (n * D_MODEL) // tn
    p_src = pl.BlockSpec((tm, BRANCH_WIDTH), lambda i, t: (prow(i), 0))
    p_gate = lambda n: pl.BlockSpec((tm, tn), lambda i, t: (prow(i), gate_col(n) + t))
    s_src = pl.BlockSpec((ms, BRANCH_WIDTH), lambda i, t: (0, 0))
    s_gate = lambda n: pl.BlockSpec((ms, tn), lambda i, t: (0, gate_col(n) + t))
    return pl.pallas_call(
        functools.partial(_merge_kernel, n_ptiles=n_ptiles),
        grid=(n_ptiles + 1, D_MODEL // tn),
        in_specs=[
            p_src, p_src, p_src, p_gate(0), p_gate(1), p_gate(2),
            pl.BlockSpec(memory_space=pl.ANY),
            s_src, s_src, s_src, s_gate(0), s_gate(1), s_gate(2),
            pl.BlockSpec((ms, D_MODEL), lambda i, t: (0, 0)),
            pl.BlockSpec((None, N_BRANCH, BRANCH_WIDTH, tn), lambda i, t: (layer, 0, 0, t)),
            pl.BlockSpec((None, tn, D_MODEL), lambda i, t: (layer, t, 0)),
        ],
        out_specs=[
            pl.BlockSpec((tm, D_MODEL), lambda i, t: (prow(i), 0)),
            pl.BlockSpec((ms, D_MODEL), lambda i, t: (0, 0)),
        ],
        out_shape=[
            jax.ShapeDtypeStruct((mp, D_MODEL), F32),
            jax.ShapeDtypeStruct((ms, D_MODEL), F32),
        ],
        scratch_shapes=[pltpu.SemaphoreType.DMA(())],
        compiler_params=_params(("arbitrary", "arbitrary")),
        name="merge_outproj",
    )(prompt[0], prompt[1], prompt[2], prompt[3], prompt[3], prompt[3], prompt[4],
      sample[0], sample[1], sample[2], sample[3], sample[3], sample[3], sample[4],
      w_branch, w_out)


def _mlp_kernel(xp_ref, xs_ref, g_ref, wu_ref, wd_ref, gnext_ref, wlr_ref, *rest,
                last, n_ptiles):
    if last:
        po, so, hp_scr, hs_scr, xp_scr, x_sem = rest
        p_out, s_out = (po,), (so,)
    else:
        po, php, pglr, so, shp, sglr, hp_scr, hs_scr, xp_scr, x_sem = rest
        p_out, s_out = (po, php, pglr), (so, shp, sglr)
    i = pl.program_id(0)
    f = pl.program_id(1)
    tm = xp_scr.shape[0]

    def x_copy(tile_index):
        return pltpu.make_async_copy(xp_ref.at[pl.ds(tile_index * tm, tm), :], xp_scr, x_sem)

    @pl.when(jnp.logical_and(i == 0, f == 0))
    def _():
        x_copy(0).start()

    @pl.when(jnp.logical_and(i < n_ptiles, f == 0))
    def _():
        x_copy(i).wait()

    def tile(x_ref, h_ref, o_ref, hn_ref=None, glr_ref=None):
        @pl.when(f == 0)
        def _():
            x = x_ref[...]
            h_ref[...] = _rmsnorm_rows(x, g_ref[...]).astype(BF16)
            o_ref[...] = x

        up = jnp.dot(h_ref[...], wu_ref[...], preferred_element_type=F32)
        act = jnp.square(jnp.maximum(up, 0.0)).astype(BF16)
        o_ref[...] += jnp.dot(act, wd_ref[...], preferred_element_type=F32)

        @pl.when(f == pl.num_programs(1) - 1)
        def _():
            xn = _rmsnorm_rows(o_ref[...], gnext_ref[...])
            if last:
                o_ref[...] = xn
            else:
                hn = xn.astype(BF16)
                hn_ref[...] = hn
                glr_ref[...] = _lr_project(hn, wlr_ref)

    @pl.when(i < n_ptiles)
    def _():
        tile(xp_scr, hp_scr, *p_out)

    @pl.when(jnp.logical_and(i + 1 < n_ptiles, f == 1))
    def _():
        x_copy(i + 1).start()

    @pl.when(i == n_ptiles)
    def _():
        tile(xs_ref, hs_scr, *s_out)


def _mlp(xp, xs, norm_w, w_up, w_down, next_norm_w, w_in_t, layer, tm, tf):
    mp, ms = xp.shape[0], xs.shape[0]
    n_ptiles = mp // tm
    last = layer == DEPTH - 1
    prow = lambda i: jnp.minimum(i, n_ptiles - 1)

    def group_out(m, rows, row_map):
        specs = [pl.BlockSpec((rows, D_MODEL), lambda i, f: (row_map(i), 0))]
        shapes = [jax.ShapeDtypeStruct((m, D_MODEL), F32)]
        if not last:
            specs += [pl.BlockSpec((rows, D_MODEL), lambda i, f: (row_map(i), 0)),
                      pl.BlockSpec((rows, LANES), lambda i, f: (row_map(i), 0))]
            shapes += [jax.ShapeDtypeStruct((m, D_MODEL), BF16),
                       jax.ShapeDtypeStruct((m, LANES), F32)]
        return specs, shapes

    p_specs, p_shapes = group_out(mp, tm, prow)
    s_specs, s_shapes = group_out(ms, ms, lambda i: 0)
    if last:
        next_norm_spec = pl.BlockSpec((1, D_MODEL), lambda i, f: (0, 0))
        lr_layer = layer
    else:
        next_norm_spec = pl.BlockSpec((None, 1, D_MODEL), lambda i, f: (layer + 1, 0, 0))
        lr_layer = layer + 1
    return pl.pallas_call(
        functools.partial(_mlp_kernel, last=last, n_ptiles=n_ptiles),
        grid=(n_ptiles + 1, D_FF // tf),
        in_specs=[
            pl.BlockSpec(memory_space=pl.ANY),
            pl.BlockSpec((ms, D_MODEL), lambda i, f: (0, 0)),
            pl.BlockSpec((None, 1, D_MODEL), lambda i, f: (layer, 0, 0)),
            pl.BlockSpec((None, D_MODEL, tf), lambda i, f: (layer, 0, f)),
            pl.BlockSpec((None, tf, D_MODEL), lambda i, f: (layer, f, 0)),
            next_norm_spec,
            _lr_weight_spec(lr_layer),
        ],
        out_specs=p_specs + s_specs,
        out_shape=p_shapes + s_shapes,
        scratch_shapes=[pltpu.VMEM((tm, D_MODEL), BF16), pltpu.VMEM((ms, D_MODEL), BF16),
                        pltpu.VMEM((tm, D_MODEL), F32), pltpu.SemaphoreType.DMA(())],
        compiler_params=_params(("arbitrary", "arbitrary")),
        name="mlp",
    )(xp, xs, norm_w, w_up, w_down, next_norm_w, w_in_t)


def kernel(x_prompt, x_sample, state_gla, state_conv, cache_k, cache_v, w_in, w_lr, b_lr,
           gla_norm, conv_w, attn_sinks, w_branch, w_out, norm_mix, norm_mlp, w_up, w_down,
           norm_final):
    batch, seq, _ = x_prompt.shape
    nb = x_sample.shape[0]
    mp = batch * seq

    w_lr_p = jnp.pad(w_lr, ((0, 0), (0, LANES - GLA_RANK), (0, 0)))
    w_branch_b = w_branch.astype(BF16)
    w_out_b = w_out.astype(BF16)
    w_up_b = w_up.astype(BF16)
    w_down_b = w_down.astype(BF16)
    b_lr3 = b_lr.reshape(DEPTH, 1, GLA_QK)
    gla_norm3 = gla_norm.reshape(DEPTH, 1, GLA_DV)
    norm_mix3 = norm_mix.reshape(DEPTH, 1, D_MODEL)
    norm_mlp3 = norm_mlp.reshape(DEPTH, 1, D_MODEL)
    norm_final2 = norm_final.reshape(1, D_MODEL)
    state_conv2 = state_conv.reshape(DEPTH, nb, (CONV_K - 1) * CONV_WIDTH)
    cache_k4 = jnp.transpose(cache_k, (0, 1, 3, 4, 2))
    cache_v4 = jnp.transpose(cache_v, (0, 1, 3, 4, 2))

    w_in_t = jnp.swapaxes(w_in, 1, 2)
    tm = min(1024, mp)

    xp = x_prompt.reshape(mp, D_MODEL)
    xs = x_sample.reshape(nb, D_MODEL)
    hp, glrp = _norm(xp, norm_mix3, w_in_t, 0, tm=tm)
    hs, glrs = _norm(xs, norm_mix3, w_in_t, 0, tm=nb)
    outs = [[] for _ in range(4)]
    sample_states = ()
    for l in range(DEPTH):
        next_norm = norm_final2 if l == DEPTH - 1 else norm_mix3
        yp, gp, ys, gs = _inproj(hp, hs, w_in_t, l, tm=tm, tn=1536)
        oa, sg_p = _gla_prompt(yp, glrp, w_lr_p, b_lr3, gla_norm3, l, batch, seq)
        oc, ob, sc_p = _swa_conv_prompt(yp, attn_sinks, conv_w, l, batch, seq)
        yp3 = yp.reshape(batch, seq, COL_GATES)
        kp = yp3[:, seq - WINDOW:, COL_SK:COL_SK + SWA_KV]
        vp = yp3[:, seq - WINDOW:, COL_SV:COL_SV + SWA_KV]
        sa, sb, sc, sample_states = _sample_mixer(
            ys, glrs, attn_sinks, w_lr_p, b_lr3, gla_norm3, conv_w,
            state_gla, state_conv2, cache_k4, cache_v4, sample_states, l)
        xp, xs = _merge((oa, ob, oc, gp, xp), (sa, sb, sc, gs, xs), w_branch_b, w_out_b, l,
                        tm=tm, tn=512)
        res = _mlp(xp, xs, norm_mlp3, w_up_b, w_down_b, next_norm, w_in_t, l, tm=tm, tf=512)
        if l == DEPTH - 1:
            xp, xs = res
        else:
            xp, hp, glrp, xs, hs, glrs = res
        for lst, val in zip(outs, (
                sg_p, sc_p,
                kp.reshape(batch, WINDOW, SWA_KV_HEADS, SWA_HD),
                vp.reshape(batch, WINDOW, SWA_KV_HEADS, SWA_HD))):
            lst.append(val)
    y_prompt = xp.reshape(batch, seq, D_MODEL)
    y_sample = xs.reshape(nb, 1, D_MODEL)
    sg_p, sc_p, kp, vp = (jnp.stack(o) for o in outs)
    sg_s, sc_s, ks, vs = sample_states
    return (y_prompt, y_sample, sg_p, sg_s, sc_p,
            sc_s.reshape(DEPTH, nb, CONV_K - 1, CONV_WIDTH), kp,
            jnp.transpose(ks, (0, 1, 4, 2, 3)), vp, jnp.transpose(vs, (0, 1, 4, 2, 3)))
```

```python
import functools

import jax
import jax.numpy as jnp
from jax import lax
from jax.experimental import pallas as pl
from jax.experimental.pallas import tpu as pltpu

F32 = jnp.float32
BF16 = jnp.bfloat16

D_MODEL = 2048
DEPTH = 4
PAST_LEN = 16384
BRANCH_WIDTH = D_MODEL // 2
N_BRANCH = 3
GLA_HEADS = 4
GLA_DV = BRANCH_WIDTH // GLA_HEADS
GLA_DK = GLA_DV // 2
GLA_RANK = 16
GLA_TAU = 16.0
GLA_CHUNK = 64
CONV_WIDTH = BRANCH_WIDTH
CONV_K = 3
SWA_HD = 64
SWA_HEADS = BRANCH_WIDTH // SWA_HD
SWA_KV_HEADS = SWA_HEADS // 4
SWA_GROUP = SWA_HEADS // SWA_KV_HEADS
WINDOW = 128
D_FF = 4 * D_MODEL
EPS = 1e-6

LANES = 128
SUBLANES = 8
VMEM_LIMIT = 56 * 1024 * 1024

GLA_QK = GLA_HEADS * GLA_DK
GLA_V = GLA_HEADS * GLA_DV
SWA_Q = SWA_HEADS * SWA_HD
SWA_KV = SWA_KV_HEADS * SWA_HD

COL_Q = 0
COL_K = COL_Q + GLA_QK
COL_V = COL_K + GLA_QK
COL_GR = COL_V + GLA_V
COL_CB = COL_GR + GLA_V
COL_CC = COL_CB + CONV_WIDTH
COL_CH = COL_CC + CONV_WIDTH
COL_SQ = COL_CH + CONV_WIDTH
COL_SK = COL_SQ + SWA_Q
COL_SV = COL_SK + SWA_KV
COL_GATES = COL_SV + SWA_KV
PACKED_COLS = COL_GATES + N_BRANCH * D_MODEL
LR_SRC = COL_CB

ALIBI_SLOPES = tuple(2.0 ** (-8.0 * (h + 1) / SWA_HEADS) for h in range(SWA_HEADS))
SWA_SCALE = SWA_HD ** -0.5

NT_DIMS = (((1,), (1,)), ((), ()))
TN_DIMS = (((0,), (0,)), ((), ()))


def _params(sem):
    return pltpu.CompilerParams(dimension_semantics=sem, vmem_limit_bytes=VMEM_LIMIT)


def _rmsnorm_rows(x, g):
    ms = jnp.mean(x * x, axis=-1, keepdims=True)
    return x * lax.rsqrt(ms + EPS) * g


def _log_sigmoid(z):
    return jnp.minimum(z, 0.0) - jnp.log(1.0 + jnp.exp(-jnp.abs(z)))


def _forget_gate_log(glr_ref, wlr_ref, blr_ref):
    z = jnp.dot(glr_ref[...].astype(BF16), wlr_ref[...].astype(BF16),
                preferred_element_type=F32) + blr_ref[...]
    return _log_sigmoid(z) / GLA_TAU


def _split_bf16x3(x):
    hi = x.astype(BF16)
    r = x - hi.astype(F32)
    mid = r.astype(BF16)
    lo = (r - mid.astype(F32)).astype(BF16)
    return hi, mid, lo


def _silu(x):
    return x * jax.nn.sigmoid(x)


def _lr_weight_spec(layer):
    return pl.BlockSpec((None, LANES, D_MODEL), lambda *a: (layer, LR_SRC // LANES, 0))


def _lr_project(h, wlr_ref):
    return lax.dot_general(h, wlr_ref[...].astype(BF16), NT_DIMS, preferred_element_type=F32)


def _norm_kernel(x_ref, g_ref, wlr_ref, h_ref, glr_ref):
    h = _rmsnorm_rows(x_ref[...], g_ref[...]).astype(BF16)
    h_ref[...] = h
    glr_ref[...] = _lr_project(h, wlr_ref)


def _norm(x, norm_w, w_in_t, layer, tm):
    m = x.shape[0]
    return pl.pallas_call(
        _norm_kernel,
        grid=(m // tm,),
        in_specs=[
            pl.BlockSpec((tm, D_MODEL), lambda i: (i, 0)),
            pl.BlockSpec((None, 1, D_MODEL), lambda i: (layer, 0, 0)),
            _lr_weight_spec(layer),
        ],
        out_specs=[
            pl.BlockSpec((tm, D_MODEL), lambda i: (i, 0)),
            pl.BlockSpec((tm, LANES), lambda i: (i, 0)),
        ],
        out_shape=[
            jax.ShapeDtypeStruct((m, D_MODEL), BF16),
            jax.ShapeDtypeStruct((m, LANES), F32),
        ],
        compiler_params=_params(("parallel",)),
        name="norm",
    )(x, norm_w, w_in_t)


def _inproj_kernel(hp_ref, hs_ref, w_hbm, yp_ref, gp_ref, ys_ref, gs_ref, wf32_scr, wbf_ref,
                   w_sem, *, layer, n_main, n_ptiles, n_plain):
    j = pl.program_id(0)
    i = pl.program_id(1)
    tn = wbf_ref.shape[0]

    def w_copy(tile):
        row = pl.multiple_of(tile * tn + jnp.where(tile >= n_plain, GLA_RANK, 0), GLA_RANK)
        return pltpu.make_async_copy(w_hbm.at[layer, pl.ds(row, tn), :], wf32_scr, w_sem)

    @pl.when(i == 0)
    def _():
        @pl.when(j == 0)
        def _():
            w_copy(0).start()

        w_copy(j).wait()
        wbf_ref[...] = wf32_scr[...].astype(BF16)

        @pl.when(j + 1 < pl.num_programs(0))
        def _():
            w_copy(j + 1).start()

    def project(h_ref):
        return lax.dot_general(h_ref[...], wbf_ref[...], NT_DIMS, preferred_element_type=F32)

    is_main = j < n_main
    is_prompt = i < n_ptiles

    @pl.when(jnp.logical_and(is_main, is_prompt))
    def _():
        yp_ref[...] = project(hp_ref)

    @pl.when(jnp.logical_and(jnp.logical_not(is_main), is_prompt))
    def _():
        gp_ref[...] = project(hp_ref).astype(BF16)

    @pl.when(jnp.logical_and(is_main, jnp.logical_not(is_prompt)))
    def _():
        ys_ref[...] = project(hs_ref)

    @pl.when(jnp.logical_and(jnp.logical_not(is_main), jnp.logical_not(is_prompt)))
    def _():
        gs_ref[...] = project(hs_ref).astype(BF16)


def _inproj(hp, hs, w_in_t, layer, tm, tn):
    mp, ms = hp.shape[0], hs.shape[0]
    n_ptiles = mp // tm
    n_main = COL_GATES // tn
    n_gate = (N_BRANCH * D_MODEL) // tn
    n_plain = LR_SRC // tn
    last_p = n_ptiles - 1
    prow = lambda i: jnp.minimum(i, last_p)
    return pl.pallas_call(
        functools.partial(_inproj_kernel, layer=layer, n_main=n_main, n_ptiles=n_ptiles,
                          n_plain=n_plain),
        grid=(n_main + n_gate, n_ptiles + 1),
        in_specs=[
            pl.BlockSpec((tm, D_MODEL), lambda j, i: (prow(i), 0)),
            pl.BlockSpec((ms, D_MODEL), lambda j, i: (0, 0)),
            pl.BlockSpec(memory_space=pl.ANY),
        ],
        out_specs=[
            pl.BlockSpec((tm, tn), lambda j, i: (jnp.where(j < n_main, prow(i), last_p),
                                                 jnp.minimum(j, n_main - 1))),
            pl.BlockSpec((tm, tn), lambda j, i: (jnp.where(j < n_main, 0, prow(i)),
                                                 jnp.maximum(j - n_main, 0))),
            pl.BlockSpec((ms, tn), lambda j, i: (0, jnp.minimum(j, n_main - 1))),
            pl.BlockSpec((ms, tn), lambda j, i: (0, jnp.maximum(j - n_main, 0))),
        ],
        out_shape=[
            jax.ShapeDtypeStruct((mp, COL_GATES), F32),
            jax.ShapeDtypeStruct((mp, N_BRANCH * D_MODEL), BF16),
            jax.ShapeDtypeStruct((ms, COL_GATES), F32),
            jax.ShapeDtypeStruct((ms, N_BRANCH * D_MODEL), BF16),
        ],
        scratch_shapes=[pltpu.VMEM((tn, D_MODEL), F32), pltpu.VMEM((tn, D_MODEL), BF16),
                        pltpu.SemaphoreType.DMA(())],
        compiler_params=_params(("arbitrary", "arbitrary")),
        name="inproj",
    )(hp, hs, w_in_t)


GLA_STEP_CHUNKS = 4


def _gla_kernel(q_ref, k_ref, v_ref, gr_ref, glr_ref, wlr_ref, blr_ref, gn_ref,
                oa_ref, sfin_ref, st_ref):
    c = pl.program_id(1)
    cs = GLA_CHUNK

    @pl.when(c == 0)
    def _():
        st_ref[...] = jnp.zeros_like(st_ref)

    log_a = _forget_gate_log(glr_ref, wlr_ref, blr_ref)
    tr = GLA_STEP_CHUNKS * cs
    row = lax.broadcasted_iota(jnp.int32, (tr, tr), 0)
    col = lax.broadcasted_iota(jnp.int32, (tr, tr), 1)
    log2_cs = cs.bit_length() - 1
    same_chunk = jnp.right_shift(row, log2_cs) == jnp.right_shift(col, log2_cs)
    causal = jnp.logical_and(same_chunk, row >= col)
    tri = jnp.where(causal, 1.0, 0.0).astype(BF16)
    gn = gn_ref[...]

    b = sum(jnp.dot(tri, part, preferred_element_type=F32) for part in _split_bf16x3(log_a))
    b_last = jnp.concatenate(
        [jnp.broadcast_to(b[(n + 1) * cs - 1:(n + 1) * cs, :], (cs, b.shape[1]))
         for n in range(GLA_STEP_CHUNKS)], axis=0)
    q = q_ref[...] * (GLA_DK ** -0.5)
    k = k_ref[...]
    qt = (q * jnp.exp(b)).astype(BF16)
    kt = (k * jnp.exp(-b)).astype(BF16)
    kd = (k * jnp.exp(b_last - b)).astype(BF16)
    g_last = jnp.exp(b_last)
    v = v_ref[...].astype(BF16)

    for h in range(GLA_HEADS):
        ks = slice(h * GLA_DK, (h + 1) * GLA_DK)
        vs = slice(h * GLA_DV, (h + 1) * GLA_DV)
        a = lax.dot_general(qt[:, ks], kt[:, ks], NT_DIMS, preferred_element_type=F32)
        a = jnp.where(causal, a, 0.0).astype(BF16)
        o_intra = jnp.dot(a, v[:, vs], preferred_element_type=F32)
        u_t = [lax.dot_general(v[n * cs:(n + 1) * cs, vs], kd[n * cs:(n + 1) * cs, ks], TN_DIMS,
                               preferred_element_type=F32) for n in range(GLA_STEP_CHUNKS)]
        s_t = st_ref[h]
        for n in range(GLA_STEP_CHUNKS):
            rs = slice(n * cs, (n + 1) * cs)
            o = lax.dot_general(qt[rs, ks], s_t.astype(BF16), NT_DIMS,
                                preferred_element_type=F32) + o_intra[rs, :]
            s_t = g_last[n * cs:n * cs + 1, ks] * s_t + u_t[n]
            oa_ref[rs, vs] = (_rmsnorm_rows(o, gn) * _silu(gr_ref[rs, vs])).astype(BF16)
        st_ref[h] = s_t

    @pl.when(c == pl.num_programs(1) - 1)
    def _():
        for h in range(GLA_HEADS):
            sfin_ref[0, h] = st_ref[h].T


def _gla_prompt(y, glr, w_lr_p, b_lr, gla_norm, layer, batch, seq):
    tr = GLA_STEP_CHUNKS * GLA_CHUNK
    ns = seq // tr
    rows = lambda b, c: b * ns + c
    return pl.pallas_call(
        _gla_kernel,
        grid=(batch, ns),
        in_specs=[
            pl.BlockSpec((tr, GLA_QK), lambda b, c: (rows(b, c), COL_Q // GLA_QK)),
            pl.BlockSpec((tr, GLA_QK), lambda b, c: (rows(b, c), COL_K // GLA_QK)),
            pl.BlockSpec((tr, GLA_V), lambda b, c: (rows(b, c), COL_V // GLA_V)),
            pl.BlockSpec((tr, GLA_V), lambda b, c: (rows(b, c), COL_GR // GLA_V)),
            pl.BlockSpec((tr, LANES), lambda b, c: (rows(b, c), 0)),
            pl.BlockSpec((None, LANES, GLA_QK), lambda b, c: (layer, 0, 0)),
            pl.BlockSpec((None, 1, GLA_QK), lambda b, c: (layer, 0, 0)),
            pl.BlockSpec((None, 1, GLA_DV), lambda b, c: (layer, 0, 0)),
        ],
        out_specs=[
            pl.BlockSpec((tr, GLA_V), lambda b, c: (rows(b, c), 0)),
            pl.BlockSpec((1, GLA_HEADS, GLA_DK, GLA_DV), lambda b, c: (b, 0, 0, 0)),
        ],
        out_shape=[
            jax.ShapeDtypeStruct((batch * seq, GLA_V), BF16),
            jax.ShapeDtypeStruct((batch, GLA_HEADS, GLA_DK, GLA_DV), F32),
        ],
        scratch_shapes=[pltpu.VMEM((GLA_HEADS, GLA_DV, GLA_DK), F32)],
        compiler_params=_params(("parallel", "arbitrary")),
        name="gla_prompt",
    )(y, y, y, y, glr, w_lr_p, b_lr, gla_norm)


def _conv_tile(cb_ref, cc_ref, ch_ref, w_ref, ob_ref, cs_ref, prev_ref):
    @pl.when(pl.program_id(1) == 0)
    def _():
        prev_ref[...] = jnp.zeros_like(prev_ref)

    u = cc_ref[...] * ch_ref[...]
    tl = u.shape[0]
    prev = prev_ref[...]
    p_m1 = prev[SUBLANES - 1:SUBLANES, :]
    p_m2 = prev[SUBLANES - 2:SUBLANES - 1, :]
    row = lax.broadcasted_iota(jnp.int32, u.shape, 0)
    u1 = jnp.where(row == 0, p_m1, pltpu.roll(u, 1, 0))
    u2 = jnp.where(row == 0, p_m2, jnp.where(row == 1, p_m1, pltpu.roll(u, 2, 0)))
    w = w_ref[...]
    conv = w[0:1, :] * u2 + w[1:2, :] * u1 + w[2:3, :] * u
    ob_ref[...] = (cb_ref[...] * conv).astype(BF16)
    prev_ref[...] = u[tl - SUBLANES:tl, :]
    cs_ref[0] = u[tl - (CONV_K - 1):tl, :]


def _swa_conv_kernel(sinks_ref, q_ref, kc_ref, kp_ref, vc_ref, vp_ref,
                     cb_ref, cc_ref, ch_ref, cw_ref,
                     o_ref, ob_ref, cs_ref, bias_ref, prev_ref, *, layer):
    _conv_tile(cb_ref, cc_ref, ch_ref, cw_ref, ob_ref, cs_ref, prev_ref)
    blk = pl.program_id(1)

    key = lax.broadcasted_iota(jnp.int32, (2 * WINDOW, WINDOW), 0)

    @pl.when(blk == 0)
    def _():
        qry = lax.broadcasted_iota(jnp.int32, (2 * WINDOW, WINDOW), 1)
        dist = qry - key + WINDOW
        valid = jnp.logical_and(dist >= 0, dist < WINDOW)
        dist_f = dist.astype(F32)
        for h in range(SWA_HEADS):
            bias_ref[h] = jnp.where(valid, -ALIBI_SLOPES[h] * dist_f, -jnp.inf)

    key_exists = jnp.logical_or(blk > 0, key >= WINDOW)
    q = (q_ref[...] * SWA_SCALE).astype(BF16)
    k2 = jnp.concatenate([kp_ref[...], kc_ref[...]], axis=0).astype(BF16)
    v2_t = jnp.concatenate([vp_ref[...], vc_ref[...]], axis=0).T.astype(BF16)
    scores = []
    for h in range(SWA_HEADS):
        j = h // SWA_GROUP
        scores.append(lax.dot_general(k2[:, j * SWA_HD:(j + 1) * SWA_HD],
                                      q[:, h * SWA_HD:(h + 1) * SWA_HD], NT_DIMS,
                                      preferred_element_type=F32))
    probs = []
    for h in range(SWA_HEADS):
        s = jnp.where(key_exists, scores[h] + bias_ref[h], -jnp.inf)
        sink = sinks_ref[layer, h]
        m = jnp.maximum(jnp.max(s, axis=0, keepdims=True), sink)
        e = jnp.exp(s - m)
        den = jnp.sum(e, axis=0, keepdims=True) + jnp.exp(sink - m)
        probs.append((e * (1.0 / den)).astype(BF16))
    outs_t = []
    for h in range(SWA_HEADS):
        j = h // SWA_GROUP
        outs_t.append(jnp.dot(v2_t[j * SWA_HD:(j + 1) * SWA_HD, :], probs[h],
                              preferred_element_type=F32))
    o_ref[...] = jnp.concatenate(outs_t, axis=0).T.astype(BF16)


def _swa_conv_prompt(y, sinks, conv_w, layer, batch, seq):
    nb = seq // WINDOW
    rows = lambda b, i: b * nb + i
    prev_rows = lambda b, i: b * nb + jnp.maximum(i - 1, 0)
    conv_seg = lambda col: pl.BlockSpec((WINDOW, CONV_WIDTH),
                                        lambda b, i: (rows(b, i), col // CONV_WIDTH))
    return pl.pallas_call(
        functools.partial(_swa_conv_kernel, layer=layer),
        grid=(batch, nb),
        in_specs=[
            pl.BlockSpec(memory_space=pltpu.SMEM),
            pl.BlockSpec((WINDOW, SWA_Q), lambda b, i: (rows(b, i), COL_SQ // SWA_Q)),
            pl.BlockSpec((WINDOW, SWA_KV), lambda b, i: (rows(b, i), COL_SK // SWA_KV)),
            pl.BlockSpec((WINDOW, SWA_KV), lambda b, i: (prev_rows(b, i), COL_SK // SWA_KV)),
            pl.BlockSpec((WINDOW, SWA_KV), lambda b, i: (rows(b, i), COL_SV // SWA_KV)),
            pl.BlockSpec((WINDOW, SWA_KV), lambda b, i: (prev_rows(b, i), COL_SV // SWA_KV)),
            conv_seg(COL_CB), conv_seg(COL_CC), conv_seg(COL_CH),
            pl.BlockSpec((None, CONV_K, CONV_WIDTH), lambda b, i: (layer, 0, 0)),
        ],
        out_specs=[
            pl.BlockSpec((WINDOW, SWA_Q), lambda b, i: (rows(b, i), 0)),
            pl.BlockSpec((WINDOW, CONV_WIDTH), lambda b, i: (rows(b, i), 0)),
            pl.BlockSpec((1, CONV_K - 1, CONV_WIDTH), lambda b, i: (b, 0, 0)),
        ],
        out_shape=[
            jax.ShapeDtypeStruct((batch * seq, SWA_Q), BF16),
            jax.ShapeDtypeStruct((batch * seq, CONV_WIDTH), BF16),
            jax.ShapeDtypeStruct((batch, CONV_K - 1, CONV_WIDTH), F32),
        ],
        scratch_shapes=[pltpu.VMEM((SWA_HEADS, 2 * WINDOW, WINDOW), F32),
                        pltpu.VMEM((SUBLANES, CONV_WIDTH), F32)],
        compiler_params=_params(("parallel", "arbitrary")),
        name="swa_conv_prompt",
    )(sinks, y, y, y, y, y, y, y, y, conv_w)


SAMPLE_STEP = SUBLANES


def _sample_kernel(sinks_ref, q_ref, k_ref, v_ref, gr_ref, glr_ref, cb_ref, cc_ref, ch_ref,
                   sq_ref, sk_ref, sv_ref, wlr_ref, blr_ref, gn_ref, cw_ref,
                   s_ref, cs_ref, kc_ref, vc_ref, *rest, layer, n_carried):
    oa_ref, ob_ref, oc_ref, so_ref, cso_ref, kco_ref, vco_ref, o_scr = rest[n_carried:]
    nb = SAMPLE_STEP
    b = _forget_gate_log(glr_ref, wlr_ref, blr_ref)
    q = q_ref[...] * (GLA_DK ** -0.5)
    k = k_ref[...]
    v = v_ref[...]
    qt = q * jnp.exp(b)
    kt = k * jnp.exp(-b)
    kd = k * jnp.exp(b - b)
    g_last = jnp.exp(b)
    qt_t = qt.T
    kd_t = kd.T
    gl_t = g_last.T
    for h in range(GLA_HEADS):
        ks = slice(h * GLA_DK, (h + 1) * GLA_DK)
        vs = slice(h * GLA_DV, (h + 1) * GLA_DV)
        a = jnp.sum(qt[:, ks] * kt[:, ks], axis=-1, keepdims=True)
        o_intra = a * v[:, vs]
        for i in range(nb):
            s_in = s_ref[i, h]
            v_row = v[i:i + 1, vs]
            o = jnp.sum(qt_t[ks, i:i + 1] * s_in, axis=0, keepdims=True)
            o_scr[i:i + 1, vs] = o + o_intra[i:i + 1, :]
            so_ref[i, h] = gl_t[ks, i:i + 1] * s_in + kd_t[ks, i:i + 1] * v_row
    gn = gn_ref[...]
    for h in range(GLA_HEADS):
        vs = slice(h * GLA_DV, (h + 1) * GLA_DV)
        oa_ref[:, vs] = _rmsnorm_rows(o_scr[:, vs], gn) * _silu(gr_ref[:, vs])

    u_new = cc_ref[...] * ch_ref[...]
    w = cw_ref[...]
    p0 = cs_ref[:, :CONV_WIDTH]
    p1 = cs_ref[:, CONV_WIDTH:]
    conv = w[0:1, :] * p0 + w[1:2, :] * p1 + w[2:3, :] * u_new
    ob_ref[...] = cb_ref[...] * conv
    cso_ref[:, :CONV_WIDTH] = p1
    cso_ref[:, CONV_WIDTH:] = u_new

    slot = lax.broadcasted_iota(jnp.int32, (SWA_HD, WINDOW), 1)
    key = lax.broadcasted_iota(jnp.int32, (SUBLANES, WINDOW), 1)
    grp = lax.broadcasted_iota(jnp.int32, (SUBLANES, WINDOW), 0)
    dist = (WINDOW - 1 - key).astype(F32)
    bias, sink = [], []
    for j in range(SWA_KV_HEADS):
        bj = jnp.zeros((SUBLANES, WINDOW), F32)
        sj = jnp.zeros((SUBLANES, 1), F32)
        for g in range(SWA_GROUP):
            bj = jnp.where(grp == g, -ALIBI_SLOPES[j * SWA_GROUP + g] * dist, bj)
            sj = jnp.where(grp[:, 0:1] == g, sinks_ref[layer, j * SWA_GROUP + g], sj)
        bias.append(bj)
        sink.append(sj)
    sk_t = sk_ref[...].T
    sv_t = sv_ref[...].T
    k_new, v_new = [], []
    for i in range(nb):
        for j in range(SWA_KV_HEADS):
            hd = slice(j * SWA_HD, (j + 1) * SWA_HD)
            kn = jnp.where(slot == WINDOW - 1, sk_t[hd, i:i + 1],
                           pltpu.roll(kc_ref[i, j], WINDOW - 1, 1))
            vn = jnp.where(slot == WINDOW - 1, sv_t[hd, i:i + 1],
                           pltpu.roll(vc_ref[i, j], WINDOW - 1, 1))
            kco_ref[i, j] = kn
            vco_ref[i, j] = vn
            k_new.append(kn.astype(BF16))
            v_new.append(vn.astype(BF16))
    scores = []
    for i in range(nb):
        for j in range(SWA_KV_HEADS):
            qj = (sq_ref[i, j] * SWA_SCALE).astype(BF16)
            scores.append(jnp.dot(qj, k_new[i * SWA_KV_HEADS + j], preferred_element_type=F32))
    probs = []
    for i in range(nb):
        for j in range(SWA_KV_HEADS):
            s = scores[i * SWA_KV_HEADS + j] + bias[j]
            m = jnp.maximum(jnp.max(s, axis=-1, keepdims=True), sink[j])
            e = jnp.exp(s - m)
            den = jnp.sum(e, axis=-1, keepdims=True) + jnp.exp(sink[j] - m)
            probs.append((e * (1.0 / den)).astype(BF16))
    for i in range(nb):
        for j in range(SWA_KV_HEADS):
            n = i * SWA_KV_HEADS + j
            oc_ref[i, j] = lax.dot_general(probs[n], v_new[n], NT_DIMS,
                                           preferred_element_type=F32)


def _sample_mixer(y, glr, sinks, w_lr_p, b_lr, gla_norm, conv_w,
                  state_gla, state_conv, cache_k, cache_v, carried, layer):
    nb = y.shape[0]
    st = SAMPLE_STEP
    sq = y[:, COL_SQ:COL_SQ + SWA_Q].reshape(nb, SWA_KV_HEADS, SWA_GROUP, SWA_HD)
    sq = jnp.pad(sq, ((0, 0), (0, 0), (0, SUBLANES - SWA_GROUP), (0, 0)))

    def seg(width, col):
        return pl.BlockSpec((st, width), lambda b: (b, col // width))

    def per_layer(shape):
        return pl.BlockSpec((None,) + shape, lambda b: (layer,) + (0,) * len(shape))

    def state(shape):
        return pl.BlockSpec((None, st) + shape, lambda b: (layer, b) + (0,) * len(shape))

    def out(shape):
        return pl.BlockSpec((st,) + shape, lambda b: (b,) + (0,) * len(shape))

    cache_shape = (SWA_KV_HEADS, SWA_HD, WINDOW)
    state_shapes = ((GLA_HEADS, GLA_DK, GLA_DV), ((CONV_K - 1) * CONV_WIDTH,),
                    cache_shape, cache_shape)
    n_fixed_inputs = 20
    res = pl.pallas_call(
        functools.partial(_sample_kernel, layer=layer, n_carried=len(carried)),
        grid=(nb // st,),
        in_specs=[
            pl.BlockSpec(memory_space=pltpu.SMEM),
            seg(GLA_QK, COL_Q), seg(GLA_QK, COL_K), seg(GLA_V, COL_V), seg(GLA_V, COL_GR),
            pl.BlockSpec((st, LANES), lambda b: (b, 0)),
            seg(CONV_WIDTH, COL_CB), seg(CONV_WIDTH, COL_CC), seg(CONV_WIDTH, COL_CH),
            out((SWA_KV_HEADS, SUBLANES, SWA_HD)), seg(SWA_KV, COL_SK), seg(SWA_KV, COL_SV),
            per_layer((LANES, GLA_QK)), per_layer((1, GLA_QK)), per_layer((1, GLA_DV)),
            per_layer((CONV_K, CONV_WIDTH)),
        ] + [state(s) for s in state_shapes] + [pl.BlockSpec(memory_space=pl.ANY)] * len(carried),
        out_specs=[
            out((GLA_V,)), out((CONV_WIDTH,)), out((SWA_KV_HEADS, SUBLANES, SWA_HD)),
        ] + [state(s) for s in state_shapes],
        out_shape=[
            jax.ShapeDtypeStruct((nb, GLA_V), F32),
            jax.ShapeDtypeStruct((nb, CONV_WIDTH), F32),
            jax.ShapeDtypeStruct((nb, SWA_KV_HEADS, SUBLANES, SWA_HD), F32),
        ] + [jax.ShapeDtypeStruct((DEPTH, nb) + s, F32) for s in state_shapes],
        input_output_aliases={n_fixed_inputs + n: 3 + n for n in range(len(carried))},
        scratch_shapes=[pltpu.VMEM((st, GLA_V), F32)],
        compiler_params=_params(("parallel",)),
        name="sample_mixer",
    )(sinks, y, y, y, y, glr, y, y, y, sq, y, y,
      w_lr_p, b_lr, gla_norm, conv_w, state_gla, state_conv, cache_k, cache_v, *carried)
    oa, ob, oc = res[:3]
    oc = oc[:, :, :SWA_GROUP, :].reshape(nb, SWA_Q)
    return oa, ob, oc, tuple(res[3:])


def _merge_kernel(*refs, n_ptiles, n_col):
    (pa, pb, pc, pg0, pg1, pg2, px, sa, sb, sc, sg0, sg1, sg2, sx, wb_ref, wo_ref,
     po_ref, so_ref, pmix_scr, smix_scr) = refs
    i = pl.program_id(0)
    s = pl.program_id(1)
    tn = wo_ref.shape[1]

    def group(a_ref, b_ref, c_ref, g0_ref, g1_ref, g2_ref, x_ref, o_ref, mix_scr):
        @pl.when(s < n_col)
        def _():
            def branch(src_ref, gate_ref, n):
                br = jnp.dot(src_ref[...].astype(BF16), wb_ref[n], preferred_element_type=F32)
                return jax.nn.sigmoid(gate_ref[...].astype(F32)) * br

            mixed = (branch(a_ref, g0_ref, 0) + branch(b_ref, g1_ref, 1)
                     + branch(c_ref, g2_ref, 2))
            mix_scr[s] = mixed.astype(BF16)

        @pl.when(s >= n_col)
        def _():
            acc = x_ref[...]
            for c in range(n_col):
                acc = acc + jnp.dot(mix_scr[c], wo_ref[c * tn:(c + 1) * tn, :],
                                    preferred_element_type=F32)
            o_ref[...] = acc

    @pl.when(i < n_ptiles)
    def _():
        group(pa, pb, pc, pg0, pg1, pg2, px, po_ref, pmix_scr)

    @pl.when(i == n_ptiles)
    def _():
        group(sa, sb, sc, sg0, sg1, sg2, sx, so_ref, smix_scr)


def _merge(prompt, sample, w_branch, w_out, layer, tm, tn):
    mp, ms = prompt[4].shape[0], sample[4].shape[0]
    n_ptiles = mp // tm
    n_col = D_MODEL // tn
    prow = lambda i: jnp.minimum(i, n_ptiles - 1)
    mix_col = lambda s: jnp.minimum(s, n_col - 1)
    out_col = lambda s: jnp.maximum(s - n_col, 0)
    p_out_col = lambda i, s: jnp.where(i < n_ptiles, out_col(s), n_col - 1)
    s_out_col = lambda i, s: jnp.where(i < n_ptiles, 0, out_col(s))
    gate_col = lambda n: (n * D_MODEL) // tn
    p_src = pl.BlockSpec((tm, BRANCH_WIDTH), lambda i, s: (prow(i), 0))
    p_gate = lambda n: pl.BlockSpec((tm, tn), lambda i, s: (prow(i), gate_col(n) + mix_col(s)))
    s_src = pl.BlockSpec((ms, BRANCH_WIDTH), lambda i, s: (0, 0))
    s_gate = lambda n: pl.BlockSpec((ms, tn), lambda i, s: (0, gate_col(n) + mix_col(s)))
    return pl.pallas_call(
        functools.partial(_merge_kernel, n_ptiles=n_ptiles, n_col=n_col),
        grid=(n_ptiles + 1, 2 * n_col),
        in_specs=[
            p_src, p_src, p_src, p_gate(0), p_gate(1), p_gate(2),
            pl.BlockSpec((tm, tn), lambda i, s: (prow(i), p_out_col(i, s))),
            s_src, s_src, s_src, s_gate(0), s_gate(1), s_gate(2),
            pl.BlockSpec((ms, tn), lambda i, s: (0, s_out_col(i, s))),
            pl.BlockSpec((None, N_BRANCH, BRANCH_WIDTH, tn),
                         lambda i, s: (layer, 0, 0, mix_col(s))),
            pl.BlockSpec((None, D_MODEL, tn), lambda i, s: (layer, 0, out_col(s))),
        ],
        out_specs=[
            pl.BlockSpec((tm, tn), lambda i, s: (prow(i), p_out_col(i, s))),
            pl.BlockSpec((ms, tn), lambda i, s: (0, s_out_col(i, s))),
        ],
        out_shape=[
            jax.ShapeDtypeStruct((mp, D_MODEL), F32),
            jax.ShapeDtypeStruct((ms, D_MODEL), F32),
        ],
        scratch_shapes=[pltpu.VMEM((n_col, tm, tn), BF16), pltpu.VMEM((n_col, ms, tn), BF16)],
        compiler_params=_params(("arbitrary", "arbitrary")),
        name="merge_outproj",
    )(prompt[0], prompt[1], prompt[2], prompt[3], prompt[3], prompt[3], prompt[4],
      sample[0], sample[1], sample[2], sample[3], sample[3], sample[3], sample[4],
      w_branch, w_out)


def _mlp_kernel(xp_ref, xs_ref, g_ref, wu_ref, wd_ref, gnext_ref, wlr_ref, *rest,
                last, n_ptiles):
    if last:
        po, so, hp_scr, hs_scr, xp_scr, x_sem = rest
        p_out, s_out = (po,), (so,)
    else:
        po, php, pglr, so, shp, sglr, hp_scr, hs_scr, xp_scr, x_sem = rest
        p_out, s_out = (po, php, pglr), (so, shp, sglr)
    i = pl.program_id(0)
    f = pl.program_id(1)
    tm = xp_scr.shape[0]

    def x_copy(tile_index):
        return pltpu.make_async_copy(xp_ref.at[pl.ds(tile_index * tm, tm), :], xp_scr, x_sem)

    @pl.when(jnp.logical_and(i == 0, f == 0))
    def _():
        x_copy(0).start()

    @pl.when(jnp.logical_and(i < n_ptiles, f == 0))
    def _():
        x_copy(i).wait()

    def tile(x_ref, h_ref, o_ref, hn_ref=None, glr_ref=None):
        @pl.when(f == 0)
        def _():
            x = x_ref[...]
            h_ref[...] = _rmsnorm_rows(x, g_ref[...]).astype(BF16)
            o_ref[...] = x

        up = jnp.dot(h_ref[...], wu_ref[...], preferred_element_type=F32)
        act = jnp.square(jnp.maximum(up, 0.0)).astype(BF16)
        o_ref[...] += jnp.dot(act, wd_ref[...], preferred_element_type=F32)

        @pl.when(f == pl.num_programs(1) - 1)
        def _():
            xn = _rmsnorm_rows(o_ref[...], gnext_ref[...])
            if last:
                o_ref[...] = xn
            else:
                hn = xn.astype(BF16)
                hn_ref[...] = hn
                glr_ref[...] = _lr_project(hn, wlr_ref)

    @pl.when(i < n_ptiles)
    def _():
        tile(xp_scr, hp_scr, *p_out)

    @pl.when(jnp.logical_and(i + 1 < n_ptiles, f == 1))
    def _():
        x_copy(i + 1).start()

    @pl.when(i == n_ptiles)
    def _():
        tile(xs_ref, hs_scr, *s_out)


def _mlp(xp, xs, norm_w, w_up, w_down, next_norm_w, w_in_t, layer, tm, tf):
    mp, ms = xp.shape[0], xs.shape[0]
    n_ptiles = mp // tm
    last = layer == DEPTH - 1
    prow = lambda i: jnp.minimum(i, n_ptiles - 1)

    def group_out(m, rows, row_map):
        specs = [pl.BlockSpec((rows, D_MODEL), lambda i, f: (row_map(i), 0))]
        shapes = [jax.ShapeDtypeStruct((m, D_MODEL), F32)]
        if not last:
            specs += [pl.BlockSpec((rows, D_MODEL), lambda i, f: (row_map(i), 0)),
                      pl.BlockSpec((rows, LANES), lambda i, f: (row_map(i), 0))]
            shapes += [jax.ShapeDtypeStruct((m, D_MODEL), BF16),
                       jax.ShapeDtypeStruct((m, LANES), F32)]
        return specs, shapes

    p_specs, p_shapes = group_out(mp, tm, prow)
    s_specs, s_shapes = group_out(ms, ms, lambda i: 0)
    if last:
        next_norm_spec = pl.BlockSpec((1, D_MODEL), lambda i, f: (0, 0))
        lr_layer = layer
    else:
        next_norm_spec = pl.BlockSpec((None, 1, D_MODEL), lambda i, f: (layer + 1, 0, 0))
        lr_layer = layer + 1
    return pl.pallas_call(
        functools.partial(_mlp_kernel, last=last, n_ptiles=n_ptiles),
        grid=(n_ptiles + 1, D_FF // tf),
        in_specs=[
            pl.BlockSpec(memory_space=pl.ANY),
            pl.BlockSpec((ms, D_MODEL), lambda i, f: (0, 0)),
            pl.BlockSpec((None, 1, D_MODEL), lambda i, f: (layer, 0, 0)),
            pl.BlockSpec((None, D_MODEL, tf), lambda i, f: (layer, 0, f)),
            pl.BlockSpec((None, tf, D_MODEL), lambda i, f: (layer, f, 0)),
            next_norm_spec,
            _lr_weight_spec(lr_layer),
        ],
        out_specs=p_specs + s_specs,
        out_shape=p_shapes + s_shapes,
        scratch_shapes=[pltpu.VMEM((tm, D_MODEL), BF16), pltpu.VMEM((ms, D_MODEL), BF16),
                        pltpu.VMEM((tm, D_MODEL), F32), pltpu.SemaphoreType.DMA(())],
        compiler_params=_params(("arbitrary", "arbitrary")),
        name="mlp",
    )(xp, xs, norm_w, w_up, w_down, next_norm_w, w_in_t)


def kernel(x_prompt, x_sample, state_gla, state_conv, cache_k, cache_v, w_in, w_lr, b_lr,
           gla_norm, conv_w, attn_sinks, w_branch, w_out, norm_mix, norm_mlp, w_up, w_down,
           norm_final):
    batch, seq, _ = x_prompt.shape
    nb = x_sample.shape[0]
    mp = batch * seq

    w_lr_p = jnp.pad(w_lr, ((0, 0), (0, LANES - GLA_RANK), (0, 0)))
    w_branch_b = w_branch.astype(BF16)
    w_out_b = w_out.astype(BF16)
    w_up_b = w_up.astype(BF16)
    w_down_b = w_down.astype(BF16)
    b_lr3 = b_lr.reshape(DEPTH, 1, GLA_QK)
    gla_norm3 = gla_norm.reshape(DEPTH, 1, GLA_DV)
    norm_mix3 = norm_mix.reshape(DEPTH, 1, D_MODEL)
    norm_mlp3 = norm_mlp.reshape(DEPTH, 1, D_MODEL)
    norm_final2 = norm_final.reshape(1, D_MODEL)
    state_conv2 = state_conv.reshape(DEPTH, nb, (CONV_K - 1) * CONV_WIDTH)
    cache_k4 = jnp.transpose(cache_k, (0, 1, 3, 4, 2))
    cache_v4 = jnp.transpose(cache_v, (0, 1, 3, 4, 2))

    w_in_t = jnp.swapaxes(w_in, 1, 2)
    tm = min(1024, mp)

    xp = x_prompt.reshape(mp, D_MODEL)
    xs = x_sample.reshape(nb, D_MODEL)
    hp, glrp = _norm(xp, norm_mix3, w_in_t, 0, tm=tm)
    hs, glrs = _norm(xs, norm_mix3, w_in_t, 0, tm=nb)
    outs = [[] for _ in range(4)]
    sample_states = ()
    for l in range(DEPTH):
        next_norm = norm_final2 if l == DEPTH - 1 else norm_mix3
        yp, gp, ys, gs = _inproj(hp, hs, w_in_t, l, tm=tm, tn=1536)
        oa, sg_p = _gla_prompt(yp, glrp, w_lr_p, b_lr3, gla_norm3, l, batch, seq)
        oc, ob, sc_p = _swa_conv_prompt(yp, attn_sinks, conv_w, l, batch, seq)
        yp3 = yp.reshape(batch, seq, COL_GATES)
        kp = yp3[:, seq - WINDOW:, COL_SK:COL_SK + SWA_KV]
        vp = yp3[:, seq - WINDOW:, COL_SV:COL_SV + SWA_KV]
        sa, sb, sc, sample_states = _sample_mixer(
            ys, glrs, attn_sinks, w_lr_p, b_lr3, gla_norm3, conv_w,
            state_gla, state_conv2, cache_k4, cache_v4, sample_states, l)
        xp, xs = _merge((oa, ob, oc, gp, xp), (sa, sb, sc, gs, xs), w_branch_b, w_out_b, l,
                        tm=tm, tn=512)
        res = _mlp(xp, xs, norm_mlp3, w_up_b, w_down_b, next_norm, w_in_t, l, tm=tm, tf=512)
        if l == DEPTH - 1:
            xp, xs = res
        else:
            xp, hp, glrp, xs, hs, glrs = res
        for lst, val in zip(outs, (
                sg_p, sc_p,
                kp.reshape(batch, WINDOW, SWA_KV_HEADS, SWA_HD),
                vp.reshape(batch, WINDOW, SWA_KV_HEADS, SWA_HD))):
            lst.append(val)
    y_prompt = xp.reshape(batch, seq, D_MODEL)
    y_sample = xs.reshape(nb, 1, D_MODEL)
    sg_p, sc_p, kp, vp = (jnp.stack(o) for o in outs)
    sg_s, sc_s, ks, vs = sample_states
    return (y_prompt, y_sample, sg_p, sg_s, sc_p,
            sc_s.reshape(DEPTH, nb, CONV_K - 1, CONV_WIDTH), kp,
            jnp.transpose(ks, (0, 1, 4, 2, 3)), vp, jnp.transpose(vs, (0, 1, 4, 2, 3)))
```

```python
import functools

import jax
import jax.numpy as jnp
from jax import lax
from jax.experimental import pallas as pl
from jax.experimental.pallas import tpu as pltpu

F32 = jnp.float32
BF16 = jnp.bfloat16

D_MODEL = 2048
DEPTH = 4
PAST_LEN = 16384
BRANCH_WIDTH = D_MODEL // 2
N_BRANCH = 3
GLA_HEADS = 4
GLA_DV = BRANCH_WIDTH // GLA_HEADS
GLA_DK = GLA_DV // 2
GLA_RANK = 16
GLA_TAU = 16.0
GLA_CHUNK = 64
CONV_WIDTH = BRANCH_WIDTH
CONV_K = 3
SWA_HD = 64
SWA_HEADS = BRANCH_WIDTH // SWA_HD
SWA_KV_HEADS = SWA_HEADS // 4
SWA_GROUP = SWA_HEADS // SWA_KV_HEADS
WINDOW = 128
D_FF = 4 * D_MODEL
EPS = 1e-6

LANES = 128
SUBLANES = 8
VMEM_LIMIT = 56 * 1024 * 1024

GLA_QK = GLA_HEADS * GLA_DK
GLA_V = GLA_HEADS * GLA_DV
SWA_Q = SWA_HEADS * SWA_HD
SWA_KV = SWA_KV_HEADS * SWA_HD

COL_Q = 0
COL_K = COL_Q + GLA_QK
COL_V = COL_K + GLA_QK
COL_GR = COL_V + GLA_V
COL_CB = COL_GR + GLA_V
COL_CC = COL_CB + CONV_WIDTH
COL_CH = COL_CC + CONV_WIDTH
COL_SQ = COL_CH + CONV_WIDTH
COL_SK = COL_SQ + SWA_Q
COL_SV = COL_SK + SWA_KV
COL_GATES = COL_SV + SWA_KV
PACKED_COLS = COL_GATES + N_BRANCH * D_MODEL
LR_SRC = COL_CB

ALIBI_SLOPES = tuple(2.0 ** (-8.0 * (h + 1) / SWA_HEADS) for h in range(SWA_HEADS))
SWA_SCALE = SWA_HD ** -0.5

NT_DIMS = (((1,), (1,)), ((), ()))
TN_DIMS = (((0,), (0,)), ((), ()))


def _params(sem):
    return pltpu.CompilerParams(dimension_semantics=sem, vmem_limit_bytes=VMEM_LIMIT)


def _rmsnorm_rows(x, g):
    ms = jnp.mean(x * x, axis=-1, keepdims=True)
    return x * lax.rsqrt(ms + EPS) * g


def _log_sigmoid(z):
    return jnp.minimum(z, 0.0) - jnp.log(1.0 + jnp.exp(-jnp.abs(z)))


def _forget_gate_log(glr_ref, wlr_ref, blr_ref):
    z = jnp.dot(glr_ref[...].astype(BF16), wlr_ref[...].astype(BF16),
                preferred_element_type=F32) + blr_ref[...]
    return _log_sigmoid(z) / GLA_TAU


def _split_bf16x3(x):
    hi = x.astype(BF16)
    r = x - hi.astype(F32)
    mid = r.astype(BF16)
    lo = (r - mid.astype(F32)).astype(BF16)
    return hi, mid, lo


def _silu(x):
    return x * jax.nn.sigmoid(x)


def _lr_weight_spec(layer):
    return pl.BlockSpec((None, LANES, D_MODEL), lambda *a: (layer, LR_SRC // LANES, 0))


def _lr_project(h, wlr_ref):
    return lax.dot_general(h, wlr_ref[...].astype(BF16), NT_DIMS, preferred_element_type=F32)


def _norm_kernel(x_ref, g_ref, wlr_ref, h_ref, glr_ref):
    h = _rmsnorm_rows(x_ref[...], g_ref[...]).astype(BF16)
    h_ref[...] = h
    glr_ref[...] = _lr_project(h, wlr_ref)


def _norm(x, norm_w, w_in_t, layer, tm):
    m = x.shape[0]
    return pl.pallas_call(
        _norm_kernel,
        grid=(m // tm,),
        in_specs=[
            pl.BlockSpec((tm, D_MODEL), lambda i: (i, 0)),
            pl.BlockSpec((None, 1, D_MODEL), lambda i: (layer, 0, 0)),
            _lr_weight_spec(layer),
        ],
        out_specs=[
            pl.BlockSpec((tm, D_MODEL), lambda i: (i, 0)),
            pl.BlockSpec((tm, LANES), lambda i: (i, 0)),
        ],
        out_shape=[
            jax.ShapeDtypeStruct((m, D_MODEL), BF16),
            jax.ShapeDtypeStruct((m, LANES), F32),
        ],
        compiler_params=_params(("parallel",)),
        name="norm",
    )(x, norm_w, w_in_t)


def _inproj_kernel(hp_ref, hs_ref, w_hbm, yp_ref, gp_ref, ys_ref, gs_ref, wf32_scr, wbf_ref,
                   w_sem, *, layer, n_main, n_ptiles, n_plain):
    j = pl.program_id(0)
    i = pl.program_id(1)
    tn = wbf_ref.shape[0]

    def w_copy(tile):
        row = pl.multiple_of(tile * tn + jnp.where(tile >= n_plain, GLA_RANK, 0), GLA_RANK)
        return pltpu.make_async_copy(w_hbm.at[layer, pl.ds(row, tn), :], wf32_scr, w_sem)

    @pl.when(i == 0)
    def _():
        @pl.when(j == 0)
        def _():
            w_copy(0).start()

        w_copy(j).wait()
        wbf_ref[...] = wf32_scr[...].astype(BF16)

        @pl.when(j + 1 < pl.num_programs(0))
        def _():
            w_copy(j + 1).start()

    def project(h_ref):
        return lax.dot_general(h_ref[...], wbf_ref[...], NT_DIMS, preferred_element_type=F32)

    is_main = j < n_main
    is_prompt = i < n_ptiles

    @pl.when(jnp.logical_and(is_main, is_prompt))
    def _():
        yp_ref[...] = project(hp_ref)

    @pl.when(jnp.logical_and(jnp.logical_not(is_main), is_prompt))
    def _():
        gp_ref[...] = project(hp_ref).astype(BF16)

    @pl.when(jnp.logical_and(is_main, jnp.logical_not(is_prompt)))
    def _():
        ys_ref[...] = project(hs_ref)

    @pl.when(jnp.logical_and(jnp.logical_not(is_main), jnp.logical_not(is_prompt)))
    def _():
        gs_ref[...] = project(hs_ref).astype(BF16)


def _inproj(hp, hs, w_in_t, layer, tm, tn):
    mp, ms = hp.shape[0], hs.shape[0]
    n_ptiles = mp // tm
    n_main = COL_GATES // tn
    n_gate = (N_BRANCH * D_MODEL) // tn
    n_plain = LR_SRC // tn
    last_p = n_ptiles - 1
    prow = lambda i: jnp.minimum(i, last_p)
    return pl.pallas_call(
        functools.partial(_inproj_kernel, layer=layer, n_main=n_main, n_ptiles=n_ptiles,
                          n_plain=n_plain),
        grid=(n_main + n_gate, n_ptiles + 1),
        in_specs=[
            pl.BlockSpec((tm, D_MODEL), lambda j, i: (prow(i), 0)),
            pl.BlockSpec((ms, D_MODEL), lambda j, i: (0, 0)),
            pl.BlockSpec(memory_space=pl.ANY),
        ],
        out_specs=[
            pl.BlockSpec((tm, tn), lambda j, i: (jnp.where(j < n_main, prow(i), last_p),
                                                 jnp.minimum(j, n_main - 1))),
            pl.BlockSpec((tm, tn), lambda j, i: (jnp.where(j < n_main, 0, prow(i)),
                                                 jnp.maximum(j - n_main, 0))),
            pl.BlockSpec((ms, tn), lambda j, i: (0, jnp.minimum(j, n_main - 1))),
            pl.BlockSpec((ms, tn), lambda j, i: (0, jnp.maximum(j - n_main, 0))),
        ],
        out_shape=[
            jax.ShapeDtypeStruct((mp, COL_GATES), F32),
            jax.ShapeDtypeStruct((mp, N_BRANCH * D_MODEL), BF16),
            jax.ShapeDtypeStruct((ms, COL_GATES), F32),
            jax.ShapeDtypeStruct((ms, N_BRANCH * D_MODEL), BF16),
        ],
        scratch_shapes=[pltpu.VMEM((tn, D_MODEL), F32), pltpu.VMEM((tn, D_MODEL), BF16),
                        pltpu.SemaphoreType.DMA(())],
        compiler_params=_params(("arbitrary", "arbitrary")),
        name="inproj",
    )(hp, hs, w_in_t)


GLA_STEP_CHUNKS = 4


def _gla_tile(q_ref, k_ref, v_ref, gr_ref, glr_ref, wlr_ref, blr_ref, gn_ref, oa_ref, st_ref):
    cs = GLA_CHUNK
    log_a = _forget_gate_log(glr_ref, wlr_ref, blr_ref)
    tr = GLA_STEP_CHUNKS * cs
    row = lax.broadcasted_iota(jnp.int32, (tr, tr), 0)
    col = lax.broadcasted_iota(jnp.int32, (tr, tr), 1)
    log2_cs = cs.bit_length() - 1
    same_chunk = jnp.right_shift(row, log2_cs) == jnp.right_shift(col, log2_cs)
    causal = jnp.logical_and(same_chunk, row >= col)
    tri = jnp.where(causal, 1.0, 0.0).astype(BF16)
    gn = gn_ref[...]

    b = sum(jnp.dot(tri, part, preferred_element_type=F32) for part in _split_bf16x3(log_a))
    b_last = jnp.concatenate(
        [jnp.broadcast_to(b[(n + 1) * cs - 1:(n + 1) * cs, :], (cs, b.shape[1]))
         for n in range(GLA_STEP_CHUNKS)], axis=0)
    q = q_ref[...] * (GLA_DK ** -0.5)
    k = k_ref[...]
    qt = (q * jnp.exp(b)).astype(BF16)
    kt = (k * jnp.exp(-b)).astype(BF16)
    kd = (k * jnp.exp(b_last - b)).astype(BF16)
    g_last = jnp.exp(b_last)
    v = v_ref[...].astype(BF16)

    for h in range(GLA_HEADS):
        ks = slice(h * GLA_DK, (h + 1) * GLA_DK)
        vs = slice(h * GLA_DV, (h + 1) * GLA_DV)
        a = lax.dot_general(qt[:, ks], kt[:, ks], NT_DIMS, preferred_element_type=F32)
        a = jnp.where(causal, a, 0.0).astype(BF16)
        o_intra = jnp.dot(a, v[:, vs], preferred_element_type=F32)
        u_t = [lax.dot_general(v[n * cs:(n + 1) * cs, vs], kd[n * cs:(n + 1) * cs, ks], TN_DIMS,
                               preferred_element_type=F32) for n in range(GLA_STEP_CHUNKS)]
        s_t = st_ref[h]
        for n in range(GLA_STEP_CHUNKS):
            rs = slice(n * cs, (n + 1) * cs)
            o = lax.dot_general(qt[rs, ks], s_t.astype(BF16), NT_DIMS,
                                preferred_element_type=F32) + o_intra[rs, :]
            s_t = g_last[n * cs:n * cs + 1, ks] * s_t + u_t[n]
            oa_ref[rs, vs] = (_rmsnorm_rows(o, gn) * _silu(gr_ref[rs, vs])).astype(BF16)
        st_ref[h] = s_t


def _conv_tile(cb_ref, cc_ref, ch_ref, w_ref, ob_ref, cs_ref, prev_ref):
    u = cc_ref[...] * ch_ref[...]
    tl = u.shape[0]
    prev = prev_ref[...]
    p_m1 = prev[SUBLANES - 1:SUBLANES, :]
    p_m2 = prev[SUBLANES - 2:SUBLANES - 1, :]
    row = lax.broadcasted_iota(jnp.int32, u.shape, 0)
    u1 = jnp.where(row == 0, p_m1, pltpu.roll(u, 1, 0))
    u2 = jnp.where(row == 0, p_m2, jnp.where(row == 1, p_m1, pltpu.roll(u, 2, 0)))
    w = w_ref[...]
    conv = w[0:1, :] * u2 + w[1:2, :] * u1 + w[2:3, :] * u
    ob_ref[...] = (cb_ref[...] * conv).astype(BF16)
    prev_ref[...] = u[tl - SUBLANES:tl, :]
    cs_ref[0] = u[tl - (CONV_K - 1):tl, :]


def _swa_bias_init(bias_ref):
    key = lax.broadcasted_iota(jnp.int32, (2 * WINDOW, WINDOW), 0)
    qry = lax.broadcasted_iota(jnp.int32, (2 * WINDOW, WINDOW), 1)
    dist = qry - key + WINDOW
    valid = jnp.logical_and(dist >= 0, dist < WINDOW)
    dist_f = dist.astype(F32)
    for h in range(SWA_HEADS):
        bias = jnp.where(valid, -ALIBI_SLOPES[h] * dist_f, -jnp.inf)
        bias_ref[0, h] = jnp.where(key >= WINDOW, bias, -jnp.inf)
        bias_ref[1, h] = bias


def _swa_block(q, k2, v2_t, bias_ref, has_prev, sinks_ref, layer):
    scores = []
    for h in range(SWA_HEADS):
        j = h // SWA_GROUP
        scores.append(lax.dot_general(k2[:, j * SWA_HD:(j + 1) * SWA_HD],
                                      q[:, h * SWA_HD:(h + 1) * SWA_HD], NT_DIMS,
                                      preferred_element_type=F32))
    weights, inv_den = [], []
    for h in range(SWA_HEADS):
        s = scores[h] + bias_ref[has_prev, h]
        sink = sinks_ref[layer, h]
        m = jnp.maximum(jnp.max(s, axis=0, keepdims=True), sink)
        e = jnp.exp(s - m)
        den = jnp.sum(e, axis=0, keepdims=True) + jnp.exp(sink - m)
        weights.append(e.astype(BF16))
        inv_den.append(1.0 / den)
    outs_t = []
    for h in range(SWA_HEADS):
        j = h // SWA_GROUP
        outs_t.append(jnp.dot(v2_t[j * SWA_HD:(j + 1) * SWA_HD, :], weights[h],
                              preferred_element_type=F32) * inv_den[h])
    return jnp.concatenate(outs_t, axis=0).T.astype(BF16)


def _prompt_mixer_kernel(sinks_ref, q_ref, k_ref, v_ref, gr_ref, glr_ref, wlr_ref, blr_ref,
                         gn_ref, sq_ref, kc_ref, kp_ref, vc_ref, vp_ref,
                         cb_ref, cc_ref, ch_ref, cw_ref,
                         oa_ref, sfin_ref, oc_ref, ob_ref, cs_ref,
                         st_ref, bias_ref, prev_ref, *, layer):
    step = pl.program_id(1)

    @pl.when(step == 0)
    def _():
        st_ref[...] = jnp.zeros_like(st_ref)
        prev_ref[...] = jnp.zeros_like(prev_ref)
        _swa_bias_init(bias_ref)

    _gla_tile(q_ref, k_ref, v_ref, gr_ref, glr_ref, wlr_ref, blr_ref, gn_ref, oa_ref, st_ref)
    _conv_tile(cb_ref, cc_ref, ch_ref, cw_ref, ob_ref, cs_ref, prev_ref)

    q = (sq_ref[...] * SWA_SCALE).astype(BF16)
    k_tile = kc_ref[...]
    v_tile = vc_ref[...]
    for u in range(q.shape[0] // WINDOW):
        cur = slice(u * WINDOW, (u + 1) * WINDOW)
        if u == 0:
            k_prev, v_prev, has_prev = kp_ref[...], vp_ref[...], jnp.minimum(step, 1)
        else:
            before = slice((u - 1) * WINDOW, u * WINDOW)
            k_prev, v_prev, has_prev = k_tile[before], v_tile[before], 1
        k2 = jnp.concatenate([k_prev, k_tile[cur]], axis=0).astype(BF16)
        v2_t = jnp.concatenate([v_prev, v_tile[cur]], axis=0).T.astype(BF16)
        oc_ref[cur, :] = _swa_block(q[cur], k2, v2_t, bias_ref, has_prev, sinks_ref, layer)

    @pl.when(step == pl.num_programs(1) - 1)
    def _():
        for h in range(GLA_HEADS):
            sfin_ref[0, h] = st_ref[h].T


def _prompt_mixer(y, glr, sinks, w_lr_p, b_lr, gla_norm, conv_w, layer, batch, seq):
    tr = GLA_STEP_CHUNKS * GLA_CHUNK
    ns = seq // tr
    wpt = tr // WINDOW
    rows = lambda b, c: b * ns + c
    prev_window = lambda b, c: (b * ns + c) * wpt - jnp.minimum(c, 1)
    seg = lambda width, col: pl.BlockSpec((tr, width), lambda b, c: (rows(b, c), col // width))
    per_layer = lambda shape: pl.BlockSpec((None,) + shape, lambda b, c: (layer,) + (0,) * len(shape))
    row_out = lambda width: pl.BlockSpec((tr, width), lambda b, c: (rows(b, c), 0))
    return pl.pallas_call(
        functools.partial(_prompt_mixer_kernel, layer=layer),
        grid=(batch, ns),
        in_specs=[
            pl.BlockSpec(memory_space=pltpu.SMEM),
            seg(GLA_QK, COL_Q), seg(GLA_QK, COL_K), seg(GLA_V, COL_V), seg(GLA_V, COL_GR),
            pl.BlockSpec((tr, LANES), lambda b, c: (rows(b, c), 0)),
            per_layer((LANES, GLA_QK)), per_layer((1, GLA_QK)), per_layer((1, GLA_DV)),
            seg(SWA_Q, COL_SQ),
            seg(SWA_KV, COL_SK),
            pl.BlockSpec((WINDOW, SWA_KV), lambda b, c: (prev_window(b, c), COL_SK // SWA_KV)),
            seg(SWA_KV, COL_SV),
            pl.BlockSpec((WINDOW, SWA_KV), lambda b, c: (prev_window(b, c), COL_SV // SWA_KV)),
            seg(CONV_WIDTH, COL_CB), seg(CONV_WIDTH, COL_CC), seg(CONV_WIDTH, COL_CH),
            per_layer((CONV_K, CONV_WIDTH)),
        ],
        out_specs=[
            row_out(GLA_V),
            pl.BlockSpec((1, GLA_HEADS, GLA_DK, GLA_DV), lambda b, c: (b, 0, 0, 0)),
            row_out(SWA_Q), row_out(CONV_WIDTH),
            pl.BlockSpec((1, CONV_K - 1, CONV_WIDTH), lambda b, c: (b, 0, 0)),
        ],
        out_shape=[
            jax.ShapeDtypeStruct((batch * seq, GLA_V), BF16),
            jax.ShapeDtypeStruct((batch, GLA_HEADS, GLA_DK, GLA_DV), F32),
            jax.ShapeDtypeStruct((batch * seq, SWA_Q), BF16),
            jax.ShapeDtypeStruct((batch * seq, CONV_WIDTH), BF16),
            jax.ShapeDtypeStruct((batch, CONV_K - 1, CONV_WIDTH), F32),
        ],
        scratch_shapes=[pltpu.VMEM((GLA_HEADS, GLA_DV, GLA_DK), F32),
                        pltpu.VMEM((2, SWA_HEADS, 2 * WINDOW, WINDOW), F32),
                        pltpu.VMEM((SUBLANES, CONV_WIDTH), F32)],
        compiler_params=_params(("parallel", "arbitrary")),
        name="prompt_mixer",
    )(sinks, y, y, y, y, glr, w_lr_p, b_lr, gla_norm, y, y, y, y, y, y, y, y, conv_w)


SAMPLE_STEP = SUBLANES


def _sample_kernel(sinks_ref, q_ref, k_ref, v_ref, gr_ref, glr_ref, cb_ref, cc_ref, ch_ref,
                   sq_ref, sk_ref, sv_ref, wlr_ref, blr_ref, gn_ref, cw_ref,
                   s_ref, cs_ref, kc_ref, vc_ref, *rest, layer, n_carried):
    oa_ref, ob_ref, oc_ref, so_ref, cso_ref, kco_ref, vco_ref, o_scr = rest[n_carried:]
    nb = SAMPLE_STEP
    b = _forget_gate_log(glr_ref, wlr_ref, blr_ref)
    q = q_ref[...] * (GLA_DK ** -0.5)
    k = k_ref[...]
    v = v_ref[...]
    qt = q * jnp.exp(b)
    kt = k * jnp.exp(-b)
    kd = k * jnp.exp(b - b)
    g_last = jnp.exp(b)
    qt_t = qt.T
    kd_t = kd.T
    gl_t = g_last.T
    for h in range(GLA_HEADS):
        ks = slice(h * GLA_DK, (h + 1) * GLA_DK)
        vs = slice(h * GLA_DV, (h + 1) * GLA_DV)
        a = jnp.sum(qt[:, ks] * kt[:, ks], axis=-1, keepdims=True)
        o_intra = a * v[:, vs]
        for i in range(nb):
            s_in = s_ref[i, h]
            v_row = v[i:i + 1, vs]
            o = jnp.sum(qt_t[ks, i:i + 1] * s_in, axis=0, keepdims=True)
            o_scr[i:i + 1, vs] = o + o_intra[i:i + 1, :]
            so_ref[i, h] = gl_t[ks, i:i + 1] * s_in + kd_t[ks, i:i + 1] * v_row
    gn = gn_ref[...]
    for h in range(GLA_HEADS):
        vs = slice(h * GLA_DV, (h + 1) * GLA_DV)
        oa_ref[:, vs] = _rmsnorm_rows(o_scr[:, vs], gn) * _silu(gr_ref[:, vs])

    u_new = cc_ref[...] * ch_ref[...]
    w = cw_ref[...]
    p0 = cs_ref[:, :CONV_WIDTH]
    p1 = cs_ref[:, CONV_WIDTH:]
    conv = w[0:1, :] * p0 + w[1:2, :] * p1 + w[2:3, :] * u_new
    ob_ref[...] = cb_ref[...] * conv
    cso_ref[:, :CONV_WIDTH] = p1
    cso_ref[:, CONV_WIDTH:] = u_new

    slot = lax.broadcasted_iota(jnp.int32, (SWA_HD, WINDOW), 1)
    key = lax.broadcasted_iota(jnp.int32, (SUBLANES, WINDOW), 1)
    grp = lax.broadcasted_iota(jnp.int32, (SUBLANES, WINDOW), 0)
    dist = (WINDOW - 1 - key).astype(F32)
    bias, sink = [], []
    for j in range(SWA_KV_HEADS):
        bj = jnp.zeros((SUBLANES, WINDOW), F32)
        sj = jnp.zeros((SUBLANES, 1), F32)
        for g in range(SWA_GROUP):
            bj = jnp.where(grp == g, -ALIBI_SLOPES[j * SWA_GROUP + g] * dist, bj)
            sj = jnp.where(grp[:, 0:1] == g, sinks_ref[layer, j * SWA_GROUP + g], sj)
        bias.append(bj)
        sink.append(sj)
    sk_t = sk_ref[...].T
    sv_t = sv_ref[...].T
    k_new, v_new = [], []
    for i in range(nb):
        for j in range(SWA_KV_HEADS):
            hd = slice(j * SWA_HD, (j + 1) * SWA_HD)
            kn = jnp.where(slot == WINDOW - 1, sk_t[hd, i:i + 1],
                           pltpu.roll(kc_ref[i, j], WINDOW - 1, 1))
            vn = jnp.where(slot == WINDOW - 1, sv_t[hd, i:i + 1],
                           pltpu.roll(vc_ref[i, j], WINDOW - 1, 1))
            kco_ref[i, j] = kn
            vco_ref[i, j] = vn
            k_new.append(kn.astype(BF16))
            v_new.append(vn.astype(BF16))
    scores = []
    for i in range(nb):
        for j in range(SWA_KV_HEADS):
            qj = (sq_ref[i, j] * SWA_SCALE).astype(BF16)
            scores.append(jnp.dot(qj, k_new[i * SWA_KV_HEADS + j], preferred_element_type=F32))
    probs = []
    for i in range(nb):
        for j in range(SWA_KV_HEADS):
            s = scores[i * SWA_KV_HEADS + j] + bias[j]
            m = jnp.maximum(jnp.max(s, axis=-1, keepdims=True), sink[j])
            e = jnp.exp(s - m)
            den = jnp.sum(e, axis=-1, keepdims=True) + jnp.exp(sink[j] - m)
            probs.append((e * (1.0 / den)).astype(BF16))
    for i in range(nb):
        for j in range(SWA_KV_HEADS):
            n = i * SWA_KV_HEADS + j
            oc_ref[i, j] = lax.dot_general(probs[n], v_new[n], NT_DIMS,
                                           preferred_element_type=F32)


def _sample_mixer(y, glr, sinks, w_lr_p, b_lr, gla_norm, conv_w,
                  state_gla, state_conv, cache_k, cache_v, carried, layer):
    nb = y.shape[0]
    st = SAMPLE_STEP
    sq = y[:, COL_SQ:COL_SQ + SWA_Q].reshape(nb, SWA_KV_HEADS, SWA_GROUP, SWA_HD)
    sq = jnp.pad(sq, ((0, 0), (0, 0), (0, SUBLANES - SWA_GROUP), (0, 0)))

    def seg(width, col):
        return pl.BlockSpec((st, width), lambda b: (b, col // width))

    def per_layer(shape):
        return pl.BlockSpec((None,) + shape, lambda b: (layer,) + (0,) * len(shape))

    def state(shape):
        return pl.BlockSpec((None, st) + shape, lambda b: (layer, b) + (0,) * len(shape))

    def out(shape):
        return pl.BlockSpec((st,) + shape, lambda b: (b,) + (0,) * len(shape))

    cache_shape = (SWA_KV_HEADS, SWA_HD, WINDOW)
    state_shapes = ((GLA_HEADS, GLA_DK, GLA_DV), ((CONV_K - 1) * CONV_WIDTH,),
                    cache_shape, cache_shape)
    n_fixed_inputs = 20
    res = pl.pallas_call(
        functools.partial(_sample_kernel, layer=layer, n_carried=len(carried)),
        grid=(nb // st,),
        in_specs=[
            pl.BlockSpec(memory_space=pltpu.SMEM),
            seg(GLA_QK, COL_Q), seg(GLA_QK, COL_K), seg(GLA_V, COL_V), seg(GLA_V, COL_GR),
            pl.BlockSpec((st, LANES), lambda b: (b, 0)),
            seg(CONV_WIDTH, COL_CB), seg(CONV_WIDTH, COL_CC), seg(CONV_WIDTH, COL_CH),
            out((SWA_KV_HEADS, SUBLANES, SWA_HD)), seg(SWA_KV, COL_SK), seg(SWA_KV, COL_SV),
            per_layer((LANES, GLA_QK)), per_layer((1, GLA_QK)), per_layer((1, GLA_DV)),
            per_layer((CONV_K, CONV_WIDTH)),
        ] + [state(s) for s in state_shapes] + [pl.BlockSpec(memory_space=pl.ANY)] * len(carried),
        out_specs=[
            out((GLA_V,)), out((CONV_WIDTH,)), out((SWA_KV_HEADS, SUBLANES, SWA_HD)),
        ] + [state(s) for s in state_shapes],
        out_shape=[
            jax.ShapeDtypeStruct((nb, GLA_V), F32),
            jax.ShapeDtypeStruct((nb, CONV_WIDTH), F32),
            jax.ShapeDtypeStruct((nb, SWA_KV_HEADS, SUBLANES, SWA_HD), F32),
        ] + [jax.ShapeDtypeStruct((DEPTH, nb) + s, F32) for s in state_shapes],
        input_output_aliases={n_fixed_inputs + n: 3 + n for n in range(len(carried))},
        scratch_shapes=[pltpu.VMEM((st, GLA_V), F32)],
        compiler_params=_params(("parallel",)),
        name="sample_mixer",
    )(sinks, y, y, y, y, glr, y, y, y, sq, y, y,
      w_lr_p, b_lr, gla_norm, conv_w, state_gla, state_conv, cache_k, cache_v, *carried)
    oa, ob, oc = res[:3]
    oc = oc[:, :, :SWA_GROUP, :].reshape(nb, SWA_Q)
    return oa, ob, oc, tuple(res[3:])


def _merge_kernel(*refs, n_ptiles, n_col):
    (pa, pb, pc, pg0, pg1, pg2, px, sa, sb, sc, sg0, sg1, sg2, sx, wb_ref, wo_ref,
     po_ref, so_ref, pmix_scr, smix_scr) = refs
    i = pl.program_id(0)
    s = pl.program_id(1)
    tn = wo_ref.shape[1]

    def group(a_ref, b_ref, c_ref, g0_ref, g1_ref, g2_ref, x_ref, o_ref, mix_scr):
        @pl.when(s < n_col)
        def _():
            def branch(src_ref, gate_ref, n):
                br = jnp.dot(src_ref[...].astype(BF16), wb_ref[n], preferred_element_type=F32)
                return jax.nn.sigmoid(gate_ref[...].astype(F32)) * br

            mixed = (branch(a_ref, g0_ref, 0) + branch(b_ref, g1_ref, 1)
                     + branch(c_ref, g2_ref, 2))
            mix_scr[s] = mixed.astype(BF16)

        @pl.when(s >= n_col)
        def _():
            acc = x_ref[...]
            for c in range(n_col):
                acc = acc + jnp.dot(mix_scr[c], wo_ref[c * tn:(c + 1) * tn, :],
                                    preferred_element_type=F32)
            o_ref[...] = acc

    @pl.when(i < n_ptiles)
    def _():
        group(pa, pb, pc, pg0, pg1, pg2, px, po_ref, pmix_scr)

    @pl.when(i == n_ptiles)
    def _():
        group(sa, sb, sc, sg0, sg1, sg2, sx, so_ref, smix_scr)


def _merge(prompt, sample, w_branch, w_out, layer, tm, tn):
    mp, ms = prompt[4].shape[0], sample[4].shape[0]
    n_ptiles = mp // tm
    n_col = D_MODEL // tn
    prow = lambda i: jnp.minimum(i, n_ptiles - 1)
    mix_col = lambda s: jnp.minimum(s, n_col - 1)
    out_col = lambda s: jnp.maximum(s - n_col, 0)
    p_out_col = lambda i, s: jnp.where(i < n_ptiles, out_col(s), n_col - 1)
    s_out_col = lambda i, s: jnp.where(i < n_ptiles, 0, out_col(s))
    gate_col = lambda n: (n * D_MODEL) // tn
    p_src = pl.BlockSpec((tm, BRANCH_WIDTH), lambda i, s: (prow(i), 0))
    p_gate = lambda n: pl.BlockSpec((tm, tn), lambda i, s: (prow(i), gate_col(n) + mix_col(s)))
    s_src = pl.BlockSpec((ms, BRANCH_WIDTH), lambda i, s: (0, 0))
    s_gate = lambda n: pl.BlockSpec((ms, tn), lambda i, s: (0, gate_col(n) + mix_col(s)))
    return pl.pallas_call(
        functools.partial(_merge_kernel, n_ptiles=n_ptiles, n_col=n_col),
        grid=(n_ptiles + 1, 2 * n_col),
        in_specs=[
            p_src, p_src, p_src, p_gate(0), p_gate(1), p_gate(2),
            pl.BlockSpec((tm, tn), lambda i, s: (prow(i), p_out_col(i, s))),
            s_src, s_src, s_src, s_gate(0), s_gate(1), s_gate(2),
            pl.BlockSpec((ms, tn), lambda i, s: (0, s_out_col(i, s))),
            pl.BlockSpec((None, N_BRANCH, BRANCH_WIDTH, tn),
                         lambda i, s: (layer, 0, 0, mix_col(s))),
            pl.BlockSpec((None, D_MODEL, tn), lambda i, s: (layer, 0, out_col(s))),
        ],
        out_specs=[
            pl.BlockSpec((tm, tn), lambda i, s: (prow(i), p_out_col(i, s))),
            pl.BlockSpec((ms, tn), lambda i, s: (0, s_out_col(i, s))),
        ],
        out_shape=[
            jax.ShapeDtypeStruct((mp, D_MODEL), F32),
            jax.ShapeDtypeStruct((ms, D_MODEL), F32),
        ],
        scratch_shapes=[pltpu.VMEM((n_col, tm, tn), BF16), pltpu.VMEM((n_col, ms, tn), BF16)],
        compiler_params=_params(("arbitrary", "arbitrary")),
        name="merge_outproj",
    )(prompt[0], prompt[1], prompt[2], prompt[3], prompt[3], prompt[3], prompt[4],
      sample[0], sample[1], sample[2], sample[3], sample[3], sample[3], sample[4],
      w_branch, w_out)


def _mlp_kernel(xp_ref, xs_ref, g_ref, wu_ref, wd_ref, gnext_ref, wlr_ref, *rest,
                last, n_ptiles):
    if last:
        po, so, hp_scr, hs_scr, xp_scr, x_sem = rest
        p_out, s_out = (po,), (so,)
    else:
        po, php, pglr, so, shp, sglr, hp_scr, hs_scr, xp_scr, x_sem = rest
        p_out, s_out = (po, php, pglr), (so, shp, sglr)
    i = pl.program_id(0)
    f = pl.program_id(1)
    tm = xp_scr.shape[0]

    def x_copy(tile_index):
        return pltpu.make_async_copy(xp_ref.at[pl.ds(tile_index * tm, tm), :], xp_scr, x_sem)

    @pl.when(jnp.logical_and(i == 0, f == 0))
    def _():
        x_copy(0).start()

    @pl.when(jnp.logical_and(i < n_ptiles, f == 0))
    def _():
        x_copy(i).wait()

    def tile(x_ref, h_ref, o_ref, hn_ref=None, glr_ref=None):
        @pl.when(f == 0)
        def _():
            x = x_ref[...]
            h_ref[...] = _rmsnorm_rows(x, g_ref[...]).astype(BF16)
            o_ref[...] = x

        up = jnp.dot(h_ref[...], wu_ref[...], preferred_element_type=F32)
        act = jnp.square(jnp.maximum(up, 0.0)).astype(BF16)
        o_ref[...] += jnp.dot(act, wd_ref[...], preferred_element_type=F32)

        @pl.when(f == pl.num_programs(1) - 1)
        def _():
            xn = _rmsnorm_rows(o_ref[...], gnext_ref[...])
            if last:
                o_ref[...] = xn
            else:
                hn = xn.astype(BF16)
                hn_ref[...] = hn
                glr_ref[...] = _lr_project(hn, wlr_ref)

    @pl.when(i < n_ptiles)
    def _():
        tile(xp_scr, hp_scr, *p_out)

    @pl.when(jnp.logical_and(i + 1 < n_ptiles, f == 1))
    def _():
        x_copy(i + 1).start()

    @pl.when(i == n_ptiles)
    def _():
        tile(xs_ref, hs_scr, *s_out)


def _mlp(xp, xs, norm_w, w_up, w_down, next_norm_w, w_in_t, layer, tm, tf):
    mp, ms = xp.shape[0], xs.shape[0]
    n_ptiles = mp // tm
    last = layer == DEPTH - 1
    prow = lambda i: jnp.minimum(i, n_ptiles - 1)

    def group_out(m, rows, row_map):
        specs = [pl.BlockSpec((rows, D_MODEL), lambda i, f: (row_map(i), 0))]
        shapes = [jax.ShapeDtypeStruct((m, D_MODEL), F32)]
        if not last:
            specs += [pl.BlockSpec((rows, D_MODEL), lambda i, f: (row_map(i), 0)),
                      pl.BlockSpec((rows, LANES), lambda i, f: (row_map(i), 0))]
            shapes += [jax.ShapeDtypeStruct((m, D_MODEL), BF16),
                       jax.ShapeDtypeStruct((m, LANES), F32)]
        return specs, shapes

    p_specs, p_shapes = group_out(mp, tm, prow)
    s_specs, s_shapes = group_out(ms, ms, lambda i: 0)
    if last:
        next_norm_spec = pl.BlockSpec((1, D_MODEL), lambda i, f: (0, 0))
        lr_layer = layer
    else:
        next_norm_spec = pl.BlockSpec((None, 1, D_MODEL), lambda i, f: (layer + 1, 0, 0))
        lr_layer = layer + 1
    return pl.pallas_call(
        functools.partial(_mlp_kernel, last=last, n_ptiles=n_ptiles),
        grid=(n_ptiles + 1, D_FF // tf),
        in_specs=[
            pl.BlockSpec(memory_space=pl.ANY),
            pl.BlockSpec((ms, D_MODEL), lambda i, f: (0, 0)),
            pl.BlockSpec((None, 1, D_MODEL), lambda i, f: (layer, 0, 0)),
            pl.BlockSpec((None, D_MODEL, tf), lambda i, f: (layer, 0, f)),
            pl.BlockSpec((None, tf, D_MODEL), lambda i, f: (layer, f, 0)),
            next_norm_spec,
            _lr_weight_spec(lr_layer),
        ],
        out_specs=p_specs + s_specs,
        out_shape=p_shapes + s_shapes,
        scratch_shapes=[pltpu.VMEM((tm, D_MODEL), BF16), pltpu.VMEM((ms, D_MODEL), BF16),
                        pltpu.VMEM((tm, D_MODEL), F32), pltpu.SemaphoreType.DMA(())],
        compiler_params=_params(("arbitrary", "arbitrary")),
        name="mlp",
    )(xp, xs, norm_w, w_up, w_down, next_norm_w, w_in_t)


def kernel(x_prompt, x_sample, state_gla, state_conv, cache_k, cache_v, w_in, w_lr, b_lr,
           gla_norm, conv_w, attn_sinks, w_branch, w_out, norm_mix, norm_mlp, w_up, w_down,
           norm_final):
    batch, seq, _ = x_prompt.shape
    nb = x_sample.shape[0]
    mp = batch * seq

    w_lr_p = jnp.pad(w_lr, ((0, 0), (0, LANES - GLA_RANK), (0, 0)))
    w_branch_b = w_branch.astype(BF16)
    w_out_b = w_out.astype(BF16)
    w_up_b = w_up.astype(BF16)
    w_down_b = w_down.astype(BF16)
    b_lr3 = b_lr.reshape(DEPTH, 1, GLA_QK)
    gla_norm3 = gla_norm.reshape(DEPTH, 1, GLA_DV)
    norm_mix3 = norm_mix.reshape(DEPTH, 1, D_MODEL)
    norm_mlp3 = norm_mlp.reshape(DEPTH, 1, D_MODEL)
    norm_final2 = norm_final.reshape(1, D_MODEL)
    state_conv2 = state_conv.reshape(DEPTH, nb, (CONV_K - 1) * CONV_WIDTH)
    cache_k4 = jnp.transpose(cache_k, (0, 1, 3, 4, 2))
    cache_v4 = jnp.transpose(cache_v, (0, 1, 3, 4, 2))

    w_in_t = jnp.swapaxes(w_in, 1, 2)
    tm = min(1024, mp)

    xp = x_prompt.reshape(mp, D_MODEL)
    xs = x_sample.reshape(nb, D_MODEL)
    hp, glrp = _norm(xp, norm_mix3, w_in_t, 0, tm=tm)
    hs, glrs = _norm(xs, norm_mix3, w_in_t, 0, tm=nb)
    outs = [[] for _ in range(4)]
    sample_states = ()
    for l in range(DEPTH):
        next_norm = norm_final2 if l == DEPTH - 1 else norm_mix3
        yp, gp, ys, gs = _inproj(hp, hs, w_in_t, l, tm=tm, tn=1536)
        oa, sg_p, oc, ob, sc_p = _prompt_mixer(yp, glrp, attn_sinks, w_lr_p, b_lr3, gla_norm3,
                                               conv_w, l, batch, seq)
        yp3 = yp.reshape(batch, seq, COL_GATES)
        kp = yp3[:, seq - WINDOW:, COL_SK:COL_SK + SWA_KV]
        vp = yp3[:, seq - WINDOW:, COL_SV:COL_SV + SWA_KV]
        sa, sb, sc, sample_states = _sample_mixer(
            ys, glrs, attn_sinks, w_lr_p, b_lr3, gla_norm3, conv_w,
            state_gla, state_conv2, cache_k4, cache_v4, sample_states, l)
        xp, xs = _merge((oa, ob, oc, gp, xp), (sa, sb, sc, gs, xs), w_branch_b, w_out_b, l,
                        tm=tm, tn=512)
        res = _mlp(xp, xs, norm_mlp3, w_up_b, w_down_b, next_norm, w_in_t, l, tm=tm, tf=512)
        if l == DEPTH - 1:
            xp, xs = res
        else:
            xp, hp, glrp, xs, hs, glrs = res
        for lst, val in zip(outs, (
                sg_p, sc_p,
                kp.reshape(batch, WINDOW, SWA_KV_HEADS, SWA_HD),
                vp.reshape(batch, WINDOW, SWA_KV_HEADS, SWA_HD))):
            lst.append(val)
    y_prompt = xp.reshape(batch, seq, D_MODEL)
    y_sample = xs.reshape(nb, 1, D_MODEL)
    sg_p, sc_p, kp, vp = (jnp.stack(o) for o in outs)
    sg_s, sc_s, ks, vs = sample_states
    return (y_prompt, y_sample, sg_p, sg_s, sc_p,
            sc_s.reshape(DEPTH, nb, CONV_K - 1, CONV_WIDTH), kp,
            jnp.transpose(ks, (0, 1, 4, 2, 3)), vp, jnp.transpose(vs, (0, 1, 4, 2, 3)))
```

```python
import functools

import jax
import jax.numpy as jnp
from jax import lax
from jax.experimental import pallas as pl
from jax.experimental.pallas import tpu as pltpu

F32 = jnp.float32
BF16 = jnp.bfloat16

D_MODEL = 2048
DEPTH = 4
PAST_LEN = 16384
BRANCH_WIDTH = D_MODEL // 2
N_BRANCH = 3
GLA_HEADS = 4
GLA_DV = BRANCH_WIDTH // GLA_HEADS
GLA_DK = GLA_DV // 2
GLA_RANK = 16
GLA_TAU = 16.0
GLA_CHUNK = 64
CONV_WIDTH = BRANCH_WIDTH
CONV_K = 3
SWA_HD = 64
SWA_HEADS = BRANCH_WIDTH // SWA_HD
SWA_KV_HEADS = SWA_HEADS // 4
SWA_GROUP = SWA_HEADS // SWA_KV_HEADS
WINDOW = 128
D_FF = 4 * D_MODEL
EPS = 1e-6

LANES = 128
SUBLANES = 8
VMEM_LIMIT = 56 * 1024 * 1024

ROW_TILE = 1024
INPROJ_COL_TILE = 1536
MERGE_COL_TILE = 512
MLP_FF_TILE = 512

assert PAST_LEN >= WINDOW

GLA_QK = GLA_HEADS * GLA_DK
GLA_V = GLA_HEADS * GLA_DV
SWA_Q = SWA_HEADS * SWA_HD
SWA_KV = SWA_KV_HEADS * SWA_HD

COL_Q = 0
COL_K = COL_Q + GLA_QK
COL_V = COL_K + GLA_QK
COL_GR = COL_V + GLA_V
COL_CB = COL_GR + GLA_V
COL_CC = COL_CB + CONV_WIDTH
COL_CH = COL_CC + CONV_WIDTH
COL_SQ = COL_CH + CONV_WIDTH
COL_SK = COL_SQ + SWA_Q
COL_SV = COL_SK + SWA_KV
COL_GATES = COL_SV + SWA_KV
LR_SRC = COL_CB

ALIBI_SLOPES = tuple(2.0 ** (-8.0 * (h + 1) / SWA_HEADS) for h in range(SWA_HEADS))
SWA_SCALE = SWA_HD ** -0.5

NT_DIMS = (((1,), (1,)), ((), ()))
TN_DIMS = (((0,), (0,)), ((), ()))


def _params(sem):
    return pltpu.CompilerParams(dimension_semantics=sem, vmem_limit_bytes=VMEM_LIMIT)


def _rmsnorm_rows(x, g):
    ms = jnp.mean(x * x, axis=-1, keepdims=True)
    return x * lax.rsqrt(ms + EPS) * g


def _log_sigmoid(z):
    return jnp.minimum(z, 0.0) - jnp.log(1.0 + jnp.exp(-jnp.abs(z)))


def _forget_gate_log(glr_ref, wlr_ref, blr_ref):
    z = jnp.dot(glr_ref[...].astype(BF16), wlr_ref[...].astype(BF16),
                preferred_element_type=F32) + blr_ref[...]
    return _log_sigmoid(z) / GLA_TAU


def _split_bf16x3(x):
    hi = x.astype(BF16)
    r = x - hi.astype(F32)
    mid = r.astype(BF16)
    lo = (r - mid.astype(F32)).astype(BF16)
    return hi, mid, lo


def _silu(x):
    return x * jax.nn.sigmoid(x)


def _lr_weight_spec(layer):
    return pl.BlockSpec((None, LANES, D_MODEL), lambda *a: (layer, LR_SRC // LANES, 0))


def _lr_project(h, wlr_ref):
    return lax.dot_general(h, wlr_ref[...].astype(BF16), NT_DIMS, preferred_element_type=F32)


def _norm_kernel(x_ref, g_ref, wlr_ref, h_ref, glr_ref):
    h = _rmsnorm_rows(x_ref[...], g_ref[...]).astype(BF16)
    h_ref[...] = h
    glr_ref[...] = _lr_project(h, wlr_ref)


def _norm(x, norm_w, w_in_t, layer, tm):
    m = x.shape[0]
    return pl.pallas_call(
        _norm_kernel,
        grid=(m // tm,),
        in_specs=[
            pl.BlockSpec((tm, D_MODEL), lambda i: (i, 0)),
            pl.BlockSpec((None, 1, D_MODEL), lambda i: (layer, 0, 0)),
            _lr_weight_spec(layer),
        ],
        out_specs=[
            pl.BlockSpec((tm, D_MODEL), lambda i: (i, 0)),
            pl.BlockSpec((tm, LANES), lambda i: (i, 0)),
        ],
        out_shape=[
            jax.ShapeDtypeStruct((m, D_MODEL), BF16),
            jax.ShapeDtypeStruct((m, LANES), F32),
        ],
        compiler_params=_params(("parallel",)),
        name="norm",
    )(x, norm_w, w_in_t)


def _inproj_kernel(hp_ref, hs_ref, w_hbm, yp_ref, gp_ref, ys_ref, gs_ref, wf32_scr, wbf_ref,
                   w_sem, *, layer, n_main, n_ptiles, n_plain):
    j = pl.program_id(0)
    i = pl.program_id(1)
    tn = wbf_ref.shape[0]

    def w_copy(tile):
        row = pl.multiple_of(tile * tn + jnp.where(tile >= n_plain, GLA_RANK, 0), GLA_RANK)
        return pltpu.make_async_copy(w_hbm.at[layer, pl.ds(row, tn), :], wf32_scr, w_sem)

    @pl.when(i == 0)
    def _():
        @pl.when(j == 0)
        def _():
            w_copy(0).start()

        w_copy(j).wait()
        wbf_ref[...] = wf32_scr[...].astype(BF16)

        @pl.when(j + 1 < pl.num_programs(0))
        def _():
            w_copy(j + 1).start()

    def project(h_ref):
        return lax.dot_general(h_ref[...], wbf_ref[...], NT_DIMS, preferred_element_type=F32)

    is_main = j < n_main
    is_prompt = i < n_ptiles

    @pl.when(jnp.logical_and(is_main, is_prompt))
    def _():
        yp_ref[...] = project(hp_ref)

    @pl.when(jnp.logical_and(jnp.logical_not(is_main), is_prompt))
    def _():
        gp_ref[...] = project(hp_ref).astype(BF16)

    @pl.when(jnp.logical_and(is_main, jnp.logical_not(is_prompt)))
    def _():
        ys_ref[...] = project(hs_ref)

    @pl.when(jnp.logical_and(jnp.logical_not(is_main), jnp.logical_not(is_prompt)))
    def _():
        gs_ref[...] = project(hs_ref).astype(BF16)


def _inproj(hp, hs, w_in_t, layer, tm, tn):
    mp, ms = hp.shape[0], hs.shape[0]
    n_ptiles = mp // tm
    n_main = COL_GATES // tn
    n_gate = (N_BRANCH * D_MODEL) // tn
    n_plain = LR_SRC // tn
    last_p = n_ptiles - 1
    prow = lambda i: jnp.minimum(i, last_p)
    return pl.pallas_call(
        functools.partial(_inproj_kernel, layer=layer, n_main=n_main, n_ptiles=n_ptiles,
                          n_plain=n_plain),
        grid=(n_main + n_gate, n_ptiles + 1),
        in_specs=[
            pl.BlockSpec((tm, D_MODEL), lambda j, i: (prow(i), 0)),
            pl.BlockSpec((ms, D_MODEL), lambda j, i: (0, 0)),
            pl.BlockSpec(memory_space=pl.ANY),
        ],
        out_specs=[
            pl.BlockSpec((tm, tn), lambda j, i: (jnp.where(j < n_main, prow(i), last_p),
                                                 jnp.minimum(j, n_main - 1))),
            pl.BlockSpec((tm, tn), lambda j, i: (jnp.where(j < n_main, 0, prow(i)),
                                                 jnp.maximum(j - n_main, 0))),
            pl.BlockSpec((ms, tn), lambda j, i: (0, jnp.minimum(j, n_main - 1))),
            pl.BlockSpec((ms, tn), lambda j, i: (0, jnp.maximum(j - n_main, 0))),
        ],
        out_shape=[
            jax.ShapeDtypeStruct((mp, COL_GATES), F32),
            jax.ShapeDtypeStruct((mp, N_BRANCH * D_MODEL), BF16),
            jax.ShapeDtypeStruct((ms, COL_GATES), F32),
            jax.ShapeDtypeStruct((ms, N_BRANCH * D_MODEL), BF16),
        ],
        scratch_shapes=[pltpu.VMEM((tn, D_MODEL), F32), pltpu.VMEM((tn, D_MODEL), BF16),
                        pltpu.SemaphoreType.DMA(())],
        compiler_params=_params(("arbitrary", "arbitrary")),
        name="inproj",
    )(hp, hs, w_in_t)


GLA_STEP_CHUNKS = 4


def _gla_tile(q_ref, k_ref, v_ref, gr_ref, glr_ref, wlr_ref, blr_ref, gn_ref, oa_ref, st_ref):
    cs = GLA_CHUNK
    log_a = _forget_gate_log(glr_ref, wlr_ref, blr_ref)
    tr = GLA_STEP_CHUNKS * cs
    row = lax.broadcasted_iota(jnp.int32, (tr, tr), 0)
    col = lax.broadcasted_iota(jnp.int32, (tr, tr), 1)
    log2_cs = cs.bit_length() - 1
    same_chunk = jnp.right_shift(row, log2_cs) == jnp.right_shift(col, log2_cs)
    causal = jnp.logical_and(same_chunk, row >= col)
    tri = jnp.where(causal, 1.0, 0.0).astype(BF16)
    gn = gn_ref[...]

    b = sum(jnp.dot(tri, part, preferred_element_type=F32) for part in _split_bf16x3(log_a))
    b_last = jnp.concatenate(
        [jnp.broadcast_to(b[(n + 1) * cs - 1:(n + 1) * cs, :], (cs, b.shape[1]))
         for n in range(GLA_STEP_CHUNKS)], axis=0)
    q = q_ref[...] * (GLA_DK ** -0.5)
    k = k_ref[...]
    qt = (q * jnp.exp(b)).astype(BF16)
    kt = (k * jnp.exp(-b)).astype(BF16)
    kd = (k * jnp.exp(b_last - b)).astype(BF16)
    g_last = jnp.exp(b_last)
    v = v_ref[...].astype(BF16)

    for h in range(GLA_HEADS):
        ks = slice(h * GLA_DK, (h + 1) * GLA_DK)
        vs = slice(h * GLA_DV, (h + 1) * GLA_DV)
        a = lax.dot_general(qt[:, ks], kt[:, ks], NT_DIMS, preferred_element_type=F32)
        a = jnp.where(causal, a, 0.0).astype(BF16)
        o_intra = jnp.dot(a, v[:, vs], preferred_element_type=F32)
        u_t = [lax.dot_general(v[n * cs:(n + 1) * cs, vs], kd[n * cs:(n + 1) * cs, ks], TN_DIMS,
                               preferred_element_type=F32) for n in range(GLA_STEP_CHUNKS)]
        s_t = st_ref[h]
        for n in range(GLA_STEP_CHUNKS):
            rs = slice(n * cs, (n + 1) * cs)
            o = lax.dot_general(qt[rs, ks], s_t.astype(BF16), NT_DIMS,
                                preferred_element_type=F32) + o_intra[rs, :]
            s_t = g_last[n * cs:n * cs + 1, ks] * s_t + u_t[n]
            oa_ref[rs, vs] = (_rmsnorm_rows(o, gn) * _silu(gr_ref[rs, vs])).astype(BF16)
        st_ref[h] = s_t


def _conv_tile(cb_ref, cc_ref, ch_ref, w_ref, ob_ref, cs_ref, prev_ref):
    u = cc_ref[...] * ch_ref[...]
    tl = u.shape[0]
    prev = prev_ref[...]
    p_m1 = prev[SUBLANES - 1:SUBLANES, :]
    p_m2 = prev[SUBLANES - 2:SUBLANES - 1, :]
    row = lax.broadcasted_iota(jnp.int32, u.shape, 0)
    u1 = jnp.where(row == 0, p_m1, pltpu.roll(u, 1, 0))
    u2 = jnp.where(row == 0, p_m2, jnp.where(row == 1, p_m1, pltpu.roll(u, 2, 0)))
    w = w_ref[...]
    conv = w[0:1, :] * u2 + w[1:2, :] * u1 + w[2:3, :] * u
    ob_ref[...] = (cb_ref[...] * conv).astype(BF16)
    prev_ref[...] = u[tl - SUBLANES:tl, :]
    cs_ref[0] = u[tl - (CONV_K - 1):tl, :]


def _swa_bias_init(bias_ref):
    key = lax.broadcasted_iota(jnp.int32, (2 * WINDOW, WINDOW), 0)
    qry = lax.broadcasted_iota(jnp.int32, (2 * WINDOW, WINDOW), 1)
    dist = qry - key + WINDOW
    valid = jnp.logical_and(dist >= 0, dist < WINDOW)
    dist_f = dist.astype(F32)
    for h in range(SWA_HEADS):
        bias = jnp.where(valid, -ALIBI_SLOPES[h] * dist_f, -jnp.inf)
        bias_ref[0, h] = jnp.where(key >= WINDOW, bias, -jnp.inf)
        bias_ref[1, h] = bias


def _swa_block(q, k2, v2_t, bias_ref, has_prev, sinks_ref, layer):
    scores = []
    for h in range(SWA_HEADS):
        j = h // SWA_GROUP
        scores.append(lax.dot_general(k2[:, j * SWA_HD:(j + 1) * SWA_HD],
                                      q[:, h * SWA_HD:(h + 1) * SWA_HD], NT_DIMS,
                                      preferred_element_type=F32))
    weights, inv_den = [], []
    for h in range(SWA_HEADS):
        s = scores[h] + bias_ref[has_prev, h]
        sink = sinks_ref[layer, h]
        m = jnp.maximum(jnp.max(s, axis=0, keepdims=True), sink)
        e = jnp.exp(s - m)
        den = jnp.sum(e, axis=0, keepdims=True) + jnp.exp(sink - m)
        weights.append(e.astype(BF16))
        inv_den.append(1.0 / den)
    outs_t = []
    for h in range(SWA_HEADS):
        j = h // SWA_GROUP
        outs_t.append(jnp.dot(v2_t[j * SWA_HD:(j + 1) * SWA_HD, :], weights[h],
                              preferred_element_type=F32) * inv_den[h])
    return jnp.concatenate(outs_t, axis=0).T.astype(BF16)


def _prompt_mixer_kernel(sinks_ref, q_ref, k_ref, v_ref, gr_ref, glr_ref, wlr_ref, blr_ref,
                         gn_ref, sq_ref, kc_ref, kp_ref, vc_ref, vp_ref,
                         cb_ref, cc_ref, ch_ref, cw_ref,
                         oa_ref, sfin_ref, oc_ref, ob_ref, cs_ref,
                         st_ref, bias_ref, prev_ref, *, layer):
    step = pl.program_id(1)

    @pl.when(step == 0)
    def _():
        st_ref[...] = jnp.zeros_like(st_ref)
        prev_ref[...] = jnp.zeros_like(prev_ref)
        _swa_bias_init(bias_ref)

    _gla_tile(q_ref, k_ref, v_ref, gr_ref, glr_ref, wlr_ref, blr_ref, gn_ref, oa_ref, st_ref)
    _conv_tile(cb_ref, cc_ref, ch_ref, cw_ref, ob_ref, cs_ref, prev_ref)

    q = (sq_ref[...] * SWA_SCALE).astype(BF16)
    k_tile = kc_ref[...]
    v_tile = vc_ref[...]
    for u in range(q.shape[0] // WINDOW):
        cur = slice(u * WINDOW, (u + 1) * WINDOW)
        if u == 0:
            k_prev, v_prev, has_prev = kp_ref[...], vp_ref[...], jnp.minimum(step, 1)
        else:
            before = slice((u - 1) * WINDOW, u * WINDOW)
            k_prev, v_prev, has_prev = k_tile[before], v_tile[before], 1
        k2 = jnp.concatenate([k_prev, k_tile[cur]], axis=0).astype(BF16)
        v2_t = jnp.concatenate([v_prev, v_tile[cur]], axis=0).T.astype(BF16)
        oc_ref[cur, :] = _swa_block(q[cur], k2, v2_t, bias_ref, has_prev, sinks_ref, layer)

    @pl.when(step == pl.num_programs(1) - 1)
    def _():
        for h in range(GLA_HEADS):
            sfin_ref[0, h] = st_ref[h].T


def _prompt_mixer(y, glr, sinks, w_lr_p, b_lr, gla_norm, conv_w, layer, batch, seq):
    tr = GLA_STEP_CHUNKS * GLA_CHUNK
    ns = seq // tr
    wpt = tr // WINDOW
    rows = lambda b, c: b * ns + c
    prev_window = lambda b, c: (b * ns + c) * wpt - jnp.minimum(c, 1)
    seg = lambda width, col: pl.BlockSpec((tr, width), lambda b, c: (rows(b, c), col // width))
    per_layer = lambda shape: pl.BlockSpec((None,) + shape, lambda b, c: (layer,) + (0,) * len(shape))
    row_out = lambda width: pl.BlockSpec((tr, width), lambda b, c: (rows(b, c), 0))
    return pl.pallas_call(
        functools.partial(_prompt_mixer_kernel, layer=layer),
        grid=(batch, ns),
        in_specs=[
            pl.BlockSpec(memory_space=pltpu.SMEM),
            seg(GLA_QK, COL_Q), seg(GLA_QK, COL_K), seg(GLA_V, COL_V), seg(GLA_V, COL_GR),
            pl.BlockSpec((tr, LANES), lambda b, c: (rows(b, c), 0)),
            per_layer((LANES, GLA_QK)), per_layer((1, GLA_QK)), per_layer((1, GLA_DV)),
            seg(SWA_Q, COL_SQ),
            seg(SWA_KV, COL_SK),
            pl.BlockSpec((WINDOW, SWA_KV), lambda b, c: (prev_window(b, c), COL_SK // SWA_KV)),
            seg(SWA_KV, COL_SV),
            pl.BlockSpec((WINDOW, SWA_KV), lambda b, c: (prev_window(b, c), COL_SV // SWA_KV)),
            seg(CONV_WIDTH, COL_CB), seg(CONV_WIDTH, COL_CC), seg(CONV_WIDTH, COL_CH),
            per_layer((CONV_K, CONV_WIDTH)),
        ],
        out_specs=[
            row_out(GLA_V),
            pl.BlockSpec((1, GLA_HEADS, GLA_DK, GLA_DV), lambda b, c: (b, 0, 0, 0)),
            row_out(SWA_Q), row_out(CONV_WIDTH),
            pl.BlockSpec((1, CONV_K - 1, CONV_WIDTH), lambda b, c: (b, 0, 0)),
        ],
        out_shape=[
            jax.ShapeDtypeStruct((batch * seq, GLA_V), BF16),
            jax.ShapeDtypeStruct((batch, GLA_HEADS, GLA_DK, GLA_DV), F32),
            jax.ShapeDtypeStruct((batch * seq, SWA_Q), BF16),
            jax.ShapeDtypeStruct((batch * seq, CONV_WIDTH), BF16),
            jax.ShapeDtypeStruct((batch, CONV_K - 1, CONV_WIDTH), F32),
        ],
        scratch_shapes=[pltpu.VMEM((GLA_HEADS, GLA_DV, GLA_DK), F32),
                        pltpu.VMEM((2, SWA_HEADS, 2 * WINDOW, WINDOW), F32),
                        pltpu.VMEM((SUBLANES, CONV_WIDTH), F32)],
        compiler_params=_params(("parallel", "arbitrary")),
        name="prompt_mixer",
    )(sinks, y, y, y, y, glr, w_lr_p, b_lr, gla_norm, y, y, y, y, y, y, y, y, conv_w)


SAMPLE_STEP = SUBLANES


def _sample_kernel(sinks_ref, q_ref, k_ref, v_ref, gr_ref, glr_ref, cb_ref, cc_ref, ch_ref,
                   sq_ref, sk_ref, sv_ref, wlr_ref, blr_ref, gn_ref, cw_ref,
                   s_ref, cs_ref, kc_ref, vc_ref, *rest, layer, n_carried):
    oa_ref, ob_ref, oc_ref, so_ref, cso_ref, kco_ref, vco_ref, o_scr = rest[n_carried:]
    nb = SAMPLE_STEP
    b = _forget_gate_log(glr_ref, wlr_ref, blr_ref)
    q = q_ref[...] * (GLA_DK ** -0.5)
    k = k_ref[...]
    v = v_ref[...]
    qt = q * jnp.exp(b)
    kt = k * jnp.exp(-b)
    kd = k * jnp.exp(b - b)
    g_last = jnp.exp(b)
    qt_t = qt.T
    kd_t = kd.T
    gl_t = g_last.T
    for h in range(GLA_HEADS):
        ks = slice(h * GLA_DK, (h + 1) * GLA_DK)
        vs = slice(h * GLA_DV, (h + 1) * GLA_DV)
        a = jnp.sum(qt[:, ks] * kt[:, ks], axis=-1, keepdims=True)
        o_intra = a * v[:, vs]
        for i in range(nb):
            s_in = s_ref[i, h]
            v_row = v[i:i + 1, vs]
            o = jnp.sum(qt_t[ks, i:i + 1] * s_in, axis=0, keepdims=True)
            o_scr[i:i + 1, vs] = o + o_intra[i:i + 1, :]
            so_ref[i, h] = gl_t[ks, i:i + 1] * s_in + kd_t[ks, i:i + 1] * v_row
    gn = gn_ref[...]
    for h in range(GLA_HEADS):
        vs = slice(h * GLA_DV, (h + 1) * GLA_DV)
        oa_ref[:, vs] = _rmsnorm_rows(o_scr[:, vs], gn) * _silu(gr_ref[:, vs])

    u_new = cc_ref[...] * ch_ref[...]
    w = cw_ref[...]
    p0 = cs_ref[:, :CONV_WIDTH]
    p1 = cs_ref[:, CONV_WIDTH:]
    conv = w[0:1, :] * p0 + w[1:2, :] * p1 + w[2:3, :] * u_new
    ob_ref[...] = cb_ref[...] * conv
    cso_ref[:, :CONV_WIDTH] = p1
    cso_ref[:, CONV_WIDTH:] = u_new

    slot = lax.broadcasted_iota(jnp.int32, (SWA_HD, WINDOW), 1)
    key = lax.broadcasted_iota(jnp.int32, (SUBLANES, WINDOW), 1)
    grp = lax.broadcasted_iota(jnp.int32, (SUBLANES, WINDOW), 0)
    dist = (WINDOW - 1 - key).astype(F32)
    bias, sink = [], []
    for j in range(SWA_KV_HEADS):
        bj = jnp.zeros((SUBLANES, WINDOW), F32)
        sj = jnp.zeros((SUBLANES, 1), F32)
        for g in range(SWA_GROUP):
            bj = jnp.where(grp == g, -ALIBI_SLOPES[j * SWA_GROUP + g] * dist, bj)
            sj = jnp.where(grp[:, 0:1] == g, sinks_ref[layer, j * SWA_GROUP + g], sj)
        bias.append(bj)
        sink.append(sj)
    sk_t = sk_ref[...].T
    sv_t = sv_ref[...].T
    k_new, v_new = [], []
    for i in range(nb):
        for j in range(SWA_KV_HEADS):
            hd = slice(j * SWA_HD, (j + 1) * SWA_HD)
            kn = jnp.where(slot == WINDOW - 1, sk_t[hd, i:i + 1],
                           pltpu.roll(kc_ref[i, j], WINDOW - 1, 1))
            vn = jnp.where(slot == WINDOW - 1, sv_t[hd, i:i + 1],
                           pltpu.roll(vc_ref[i, j], WINDOW - 1, 1))
            kco_ref[i, j] = kn
            vco_ref[i, j] = vn
            k_new.append(kn.astype(BF16))
            v_new.append(vn.astype(BF16))
    scores = []
    for i in range(nb):
        for j in range(SWA_KV_HEADS):
            qj = (sq_ref[i, j] * SWA_SCALE).astype(BF16)
            scores.append(jnp.dot(qj, k_new[i * SWA_KV_HEADS + j], preferred_element_type=F32))
    probs = []
    for i in range(nb):
        for j in range(SWA_KV_HEADS):
            s = scores[i * SWA_KV_HEADS + j] + bias[j]
            m = jnp.maximum(jnp.max(s, axis=-1, keepdims=True), sink[j])
            e = jnp.exp(s - m)
            den = jnp.sum(e, axis=-1, keepdims=True) + jnp.exp(sink[j] - m)
            probs.append((e * (1.0 / den)).astype(BF16))
    for i in range(nb):
        for j in range(SWA_KV_HEADS):
            n = i * SWA_KV_HEADS + j
            oc_ref[i, j] = lax.dot_general(probs[n], v_new[n], NT_DIMS,
                                           preferred_element_type=F32)


def _sample_mixer(y, glr, sinks, w_lr_p, b_lr, gla_norm, conv_w,
                  state_gla, state_conv, cache_k, cache_v, carried, layer):
    nb = y.shape[0]
    st = SAMPLE_STEP
    sq = y[:, COL_SQ:COL_SQ + SWA_Q].reshape(nb, SWA_KV_HEADS, SWA_GROUP, SWA_HD)
    sq = jnp.pad(sq, ((0, 0), (0, 0), (0, SUBLANES - SWA_GROUP), (0, 0)))

    def seg(width, col):
        return pl.BlockSpec((st, width), lambda b: (b, col // width))

    def per_layer(shape):
        return pl.BlockSpec((None,) + shape, lambda b: (layer,) + (0,) * len(shape))

    def state(shape):
        return pl.BlockSpec((None, st) + shape, lambda b: (layer, b) + (0,) * len(shape))

    def out(shape):
        return pl.BlockSpec((st,) + shape, lambda b: (b,) + (0,) * len(shape))

    cache_shape = (SWA_KV_HEADS, SWA_HD, WINDOW)
    state_shapes = ((GLA_HEADS, GLA_DK, GLA_DV), ((CONV_K - 1) * CONV_WIDTH,),
                    cache_shape, cache_shape)
    n_fixed_inputs = 20
    res = pl.pallas_call(
        functools.partial(_sample_kernel, layer=layer, n_carried=len(carried)),
        grid=(nb // st,),
        in_specs=[
            pl.BlockSpec(memory_space=pltpu.SMEM),
            seg(GLA_QK, COL_Q), seg(GLA_QK, COL_K), seg(GLA_V, COL_V), seg(GLA_V, COL_GR),
            pl.BlockSpec((st, LANES), lambda b: (b, 0)),
            seg(CONV_WIDTH, COL_CB), seg(CONV_WIDTH, COL_CC), seg(CONV_WIDTH, COL_CH),
            out((SWA_KV_HEADS, SUBLANES, SWA_HD)), seg(SWA_KV, COL_SK), seg(SWA_KV, COL_SV),
            per_layer((LANES, GLA_QK)), per_layer((1, GLA_QK)), per_layer((1, GLA_DV)),
            per_layer((CONV_K, CONV_WIDTH)),
        ] + [state(s) for s in state_shapes] + [pl.BlockSpec(memory_space=pl.ANY)] * len(carried),
        out_specs=[
            out((GLA_V,)), out((CONV_WIDTH,)), out((SWA_KV_HEADS, SUBLANES, SWA_HD)),
        ] + [state(s) for s in state_shapes],
        out_shape=[
            jax.ShapeDtypeStruct((nb, GLA_V), F32),
            jax.ShapeDtypeStruct((nb, CONV_WIDTH), F32),
            jax.ShapeDtypeStruct((nb, SWA_KV_HEADS, SUBLANES, SWA_HD), F32),
        ] + [jax.ShapeDtypeStruct((DEPTH, nb) + s, F32) for s in state_shapes],
        input_output_aliases={n_fixed_inputs + n: 3 + n for n in range(len(carried))},
        scratch_shapes=[pltpu.VMEM((st, GLA_V), F32)],
        compiler_params=_params(("parallel",)),
        name="sample_mixer",
    )(sinks, y, y, y, y, glr, y, y, y, sq, y, y,
      w_lr_p, b_lr, gla_norm, conv_w, state_gla, state_conv, cache_k, cache_v, *carried)
    oa, ob, oc = res[:3]
    oc = oc[:, :, :SWA_GROUP, :].reshape(nb, SWA_Q)
    return oa, ob, oc, tuple(res[3:])


def _merge_kernel(*refs, n_ptiles, n_col):
    (pa, pb, pc, pg0, pg1, pg2, px, sa, sb, sc, sg0, sg1, sg2, sx, wb_ref, wo_ref,
     po_ref, so_ref, pmix_scr, smix_scr) = refs
    i = pl.program_id(0)
    s = pl.program_id(1)
    tn = wo_ref.shape[1]

    def group(a_ref, b_ref, c_ref, g0_ref, g1_ref, g2_ref, x_ref, o_ref, mix_scr):
        @pl.when(s < n_col)
        def _():
            def branch(src_ref, gate_ref, n):
                br = jnp.dot(src_ref[...].astype(BF16), wb_ref[n], preferred_element_type=F32)
                return jax.nn.sigmoid(gate_ref[...].astype(F32)) * br

            mixed = (branch(a_ref, g0_ref, 0) + branch(b_ref, g1_ref, 1)
                     + branch(c_ref, g2_ref, 2))
            mix_scr[s] = mixed.astype(BF16)

        @pl.when(s >= n_col)
        def _():
            acc = x_ref[...]
            for c in range(n_col):
                acc = acc + jnp.dot(mix_scr[c], wo_ref[c * tn:(c + 1) * tn, :],
                                    preferred_element_type=F32)
            o_ref[...] = acc

    @pl.when(i < n_ptiles)
    def _():
        group(pa, pb, pc, pg0, pg1, pg2, px, po_ref, pmix_scr)

    @pl.when(i == n_ptiles)
    def _():
        group(sa, sb, sc, sg0, sg1, sg2, sx, so_ref, smix_scr)


def _merge(prompt, sample, w_branch, w_out, layer, tm, tn):
    mp, ms = prompt[4].shape[0], sample[4].shape[0]
    n_ptiles = mp // tm
    n_col = D_MODEL // tn
    prow = lambda i: jnp.minimum(i, n_ptiles - 1)
    mix_col = lambda s: jnp.minimum(s, n_col - 1)
    out_col = lambda s: jnp.maximum(s - n_col, 0)
    p_out_col = lambda i, s: jnp.where(i < n_ptiles, out_col(s), n_col - 1)
    s_out_col = lambda i, s: jnp.where(i < n_ptiles, 0, out_col(s))
    gate_col = lambda n: (n * D_MODEL) // tn
    p_src = pl.BlockSpec((tm, BRANCH_WIDTH), lambda i, s: (prow(i), 0))
    p_gate = lambda n: pl.BlockSpec((tm, tn), lambda i, s: (prow(i), gate_col(n) + mix_col(s)))
    s_src = pl.BlockSpec((ms, BRANCH_WIDTH), lambda i, s: (0, 0))
    s_gate = lambda n: pl.BlockSpec((ms, tn), lambda i, s: (0, gate_col(n) + mix_col(s)))
    return pl.pallas_call(
        functools.partial(_merge_kernel, n_ptiles=n_ptiles, n_col=n_col),
        grid=(n_ptiles + 1, 2 * n_col),
        in_specs=[
            p_src, p_src, p_src, p_gate(0), p_gate(1), p_gate(2),
            pl.BlockSpec((tm, tn), lambda i, s: (prow(i), p_out_col(i, s))),
            s_src, s_src, s_src, s_gate(0), s_gate(1), s_gate(2),
            pl.BlockSpec((ms, tn), lambda i, s: (0, s_out_col(i, s))),
            pl.BlockSpec((None, N_BRANCH, BRANCH_WIDTH, tn),
                         lambda i, s: (layer, 0, 0, mix_col(s))),
            pl.BlockSpec((None, D_MODEL, tn), lambda i, s: (layer, 0, out_col(s))),
        ],
        out_specs=[
            pl.BlockSpec((tm, tn), lambda i, s: (prow(i), p_out_col(i, s))),
            pl.BlockSpec((ms, tn), lambda i, s: (0, s_out_col(i, s))),
        ],
        out_shape=[
            jax.ShapeDtypeStruct((mp, D_MODEL), F32),
            jax.ShapeDtypeStruct((ms, D_MODEL), F32),
        ],
        scratch_shapes=[pltpu.VMEM((n_col, tm, tn), BF16), pltpu.VMEM((n_col, ms, tn), BF16)],
        compiler_params=_params(("arbitrary", "arbitrary")),
        name="merge_outproj",
    )(prompt[0], prompt[1], prompt[2], prompt[3], prompt[3], prompt[3], prompt[4],
      sample[0], sample[1], sample[2], sample[3], sample[3], sample[3], sample[4],
      w_branch, w_out)


def _mlp_kernel(xp_ref, xs_ref, g_ref, wu_ref, wd_ref, gnext_ref, wlr_ref, *rest,
                last, n_ptiles):
    if last:
        po, so, hp_scr, hs_scr, xp_scr, x_sem = rest
        p_out, s_out = (po,), (so,)
    else:
        po, php, pglr, so, shp, sglr, hp_scr, hs_scr, xp_scr, x_sem = rest
        p_out, s_out = (po, php, pglr), (so, shp, sglr)
    i = pl.program_id(0)
    f = pl.program_id(1)
    tm = xp_scr.shape[0]

    def x_copy(tile_index):
        return pltpu.make_async_copy(xp_ref.at[pl.ds(tile_index * tm, tm), :], xp_scr, x_sem)

    @pl.when(jnp.logical_and(i == 0, f == 0))
    def _():
        x_copy(0).start()

    @pl.when(jnp.logical_and(i < n_ptiles, f == 0))
    def _():
        x_copy(i).wait()

    def tile(x_ref, h_ref, o_ref, hn_ref=None, glr_ref=None):
        @pl.when(f == 0)
        def _():
            x = x_ref[...]
            h_ref[...] = _rmsnorm_rows(x, g_ref[...]).astype(BF16)
            o_ref[...] = x

        up = jnp.dot(h_ref[...], wu_ref[...], preferred_element_type=F32)
        act = jnp.square(jnp.maximum(up, 0.0)).astype(BF16)
        o_ref[...] += jnp.dot(act, wd_ref[...], preferred_element_type=F32)

        @pl.when(f == pl.num_programs(1) - 1)
        def _():
            xn = _rmsnorm_rows(o_ref[...], gnext_ref[...])
            if last:
                o_ref[...] = xn
            else:
                hn = xn.astype(BF16)
                hn_ref[...] = hn
                glr_ref[...] = _lr_project(hn, wlr_ref)

    @pl.when(i < n_ptiles)
    def _():
        tile(xp_scr, hp_scr, *p_out)

    @pl.when(jnp.logical_and(i + 1 < n_ptiles, f == 1))
    def _():
        x_copy(i + 1).start()

    @pl.when(i == n_ptiles)
    def _():
        tile(xs_ref, hs_scr, *s_out)


def _mlp(xp, xs, norm_w, w_up, w_down, next_norm_w, w_in_t, layer, tm, tf):
    mp, ms = xp.shape[0], xs.shape[0]
    n_ptiles = mp // tm
    last = layer == DEPTH - 1
    prow = lambda i: jnp.minimum(i, n_ptiles - 1)

    def group_out(m, rows, row_map):
        specs = [pl.BlockSpec((rows, D_MODEL), lambda i, f: (row_map(i), 0))]
        shapes = [jax.ShapeDtypeStruct((m, D_MODEL), F32)]
        if not last:
            specs += [pl.BlockSpec((rows, D_MODEL), lambda i, f: (row_map(i), 0)),
                      pl.BlockSpec((rows, LANES), lambda i, f: (row_map(i), 0))]
            shapes += [jax.ShapeDtypeStruct((m, D_MODEL), BF16),
                       jax.ShapeDtypeStruct((m, LANES), F32)]
        return specs, shapes

    p_specs, p_shapes = group_out(mp, tm, prow)
    s_specs, s_shapes = group_out(ms, ms, lambda i: 0)
    if last:
        next_norm_spec = pl.BlockSpec((1, D_MODEL), lambda i, f: (0, 0))
        lr_layer = layer
    else:
        next_norm_spec = pl.BlockSpec((None, 1, D_MODEL), lambda i, f: (layer + 1, 0, 0))
        lr_layer = layer + 1
    return pl.pallas_call(
        functools.partial(_mlp_kernel, last=last, n_ptiles=n_ptiles),
        grid=(n_ptiles + 1, D_FF // tf),
        in_specs=[
            pl.BlockSpec(memory_space=pl.ANY),
            pl.BlockSpec((ms, D_MODEL), lambda i, f: (0, 0)),
            pl.BlockSpec((None, 1, D_MODEL), lambda i, f: (layer, 0, 0)),
            pl.BlockSpec((None, D_MODEL, tf), lambda i, f: (layer, 0, f)),
            pl.BlockSpec((None, tf, D_MODEL), lambda i, f: (layer, f, 0)),
            next_norm_spec,
            _lr_weight_spec(lr_layer),
        ],
        out_specs=p_specs + s_specs,
        out_shape=p_shapes + s_shapes,
        scratch_shapes=[pltpu.VMEM((tm, D_MODEL), BF16), pltpu.VMEM((ms, D_MODEL), BF16),
                        pltpu.VMEM((tm, D_MODEL), F32), pltpu.SemaphoreType.DMA(())],
        compiler_params=_params(("arbitrary", "arbitrary")),
        name="mlp",
    )(xp, xs, norm_w, w_up, w_down, next_norm_w, w_in_t)


def kernel(x_prompt, x_sample, state_gla, state_conv, cache_k, cache_v, w_in, w_lr, b_lr,
           gla_norm, conv_w, attn_sinks, w_branch, w_out, norm_mix, norm_mlp, w_up, w_down,
           norm_final):
    batch, seq, _ = x_prompt.shape
    nb = x_sample.shape[0]
    mp = batch * seq
    tm = min(ROW_TILE, mp)
    assert mp % tm == 0 and seq % (GLA_STEP_CHUNKS * GLA_CHUNK) == 0 and seq % WINDOW == 0
    assert x_sample.shape[1] == 1 and nb % SAMPLE_STEP == 0

    w_lr_p =jnp.pad(w_lr, ((0, 0), (0, LANES - GLA_RANK), (0, 0)))
    w_branch_b = w_branch.astype(BF16)
    w_out_b = w_out.astype(BF16)
    w_up_b = w_up.astype(BF16)
    w_down_b = w_down.astype(BF16)
    b_lr3 = b_lr.reshape(DEPTH, 1, GLA_QK)
    gla_norm3 = gla_norm.reshape(DEPTH, 1, GLA_DV)
    norm_mix3 = norm_mix.reshape(DEPTH, 1, D_MODEL)
    norm_mlp3 = norm_mlp.reshape(DEPTH, 1, D_MODEL)
    norm_final2 = norm_final.reshape(1, D_MODEL)
    state_conv2 = state_conv.reshape(DEPTH, nb, (CONV_K - 1) * CONV_WIDTH)
    cache_k4 = jnp.transpose(cache_k, (0, 1, 3, 4, 2))
    cache_v4 = jnp.transpose(cache_v, (0, 1, 3, 4, 2))

    w_in_t = jnp.swapaxes(w_in, 1, 2)

    xp = x_prompt.reshape(mp, D_MODEL)
    xs = x_sample.reshape(nb, D_MODEL)
    hp, glrp = _norm(xp, norm_mix3, w_in_t, 0, tm=tm)
    hs, glrs = _norm(xs, norm_mix3, w_in_t, 0, tm=nb)
    outs = [[] for _ in range(4)]
    sample_states = ()
    for l in range(DEPTH):
        next_norm = norm_final2 if l == DEPTH - 1 else norm_mix3
        yp, gp, ys, gs = _inproj(hp, hs, w_in_t, l, tm=tm, tn=INPROJ_COL_TILE)
        oa, sg_p, oc, ob, sc_p = _prompt_mixer(yp, glrp, attn_sinks, w_lr_p, b_lr3, gla_norm3,
                                               conv_w, l, batch, seq)
        yp3 = yp.reshape(batch, seq, COL_GATES)
        kp = yp3[:, seq - WINDOW:, COL_SK:COL_SK + SWA_KV]
        vp = yp3[:, seq - WINDOW:, COL_SV:COL_SV + SWA_KV]
        sa, sb, sc, sample_states = _sample_mixer(
            ys, glrs, attn_sinks, w_lr_p, b_lr3, gla_norm3, conv_w,
            state_gla, state_conv2, cache_k4, cache_v4, sample_states, l)
        xp, xs = _merge((oa, ob, oc, gp, xp), (sa, sb, sc, gs, xs), w_branch_b, w_out_b, l,
                        tm=tm, tn=MERGE_COL_TILE)
        res = _mlp(xp, xs, norm_mlp3, w_up_b, w_down_b, next_norm, w_in_t, l, tm=tm,
                   tf=MLP_FF_TILE)
        if l == DEPTH - 1:
            xp, xs = res
        else:
            xp, hp, glrp, xs, hs, glrs = res
        for lst, val in zip(outs, (
                sg_p, sc_p,
                kp.reshape(batch, WINDOW, SWA_KV_HEADS, SWA_HD),
                vp.reshape(batch, WINDOW, SWA_KV_HEADS, SWA_HD))):
            lst.append(val)
    y_prompt = xp.reshape(batch, seq, D_MODEL)
    y_sample = xs.reshape(nb, 1, D_MODEL)
    sg_p, sc_p, kp, vp = (jnp.stack(o) for o in outs)
    sg_s, sc_s, ks, vs = sample_states
    return (y_prompt, y_sample, sg_p, sg_s, sc_p,
            sc_s.reshape(DEPTH, nb, CONV_K - 1, CONV_WIDTH), kp,
            jnp.transpose(ks, (0, 1, 4, 2, 3)), vp, jnp.transpose(vs, (0, 1, 4, 2, 3)))
```

```python
import functools

import jax
import jax.numpy as jnp
from jax import lax
from jax.experimental import pallas as pl
from jax.experimental.pallas import tpu as pltpu

F32 = jnp.float32
BF16 = jnp.bfloat16

D_MODEL = 2048
DEPTH = 4
PAST_LEN = 16384
BRANCH_WIDTH = D_MODEL // 2
N_BRANCH = 3
GLA_HEADS = 4
GLA_DV = BRANCH_WIDTH // GLA_HEADS
GLA_DK = GLA_DV // 2
GLA_RANK = 16
GLA_TAU = 16.0
GLA_CHUNK = 64
CONV_WIDTH = BRANCH_WIDTH
CONV_K = 3
SWA_HD = 64
SWA_HEADS = BRANCH_WIDTH // SWA_HD
SWA_KV_HEADS = SWA_HEADS // 4
SWA_GROUP = SWA_HEADS // SWA_KV_HEADS
WINDOW = 128
D_FF = 4 * D_MODEL
EPS = 1e-6

LANES = 128
SUBLANES = 8
VMEM_LIMIT = 60 * 1024 * 1024

ROW_TILE = 1024
INPROJ_COL_TILE = 1536
MERGE_COL_TILE = 512
MERGE_OUT_TILE = 1024
MLP_FF_TILE = 1024
MLP_UP_PARTS = 2

assert PAST_LEN >= WINDOW

GLA_QK = GLA_HEADS * GLA_DK
GLA_V = GLA_HEADS * GLA_DV
SWA_Q = SWA_HEADS * SWA_HD
SWA_KV = SWA_KV_HEADS * SWA_HD

COL_Q = 0
COL_K = COL_Q + GLA_QK
COL_V = COL_K + GLA_QK
COL_GR = COL_V + GLA_V
COL_CB = COL_GR + GLA_V
COL_CC = COL_CB + CONV_WIDTH
COL_CH = COL_CC + CONV_WIDTH
COL_SQ = COL_CH + CONV_WIDTH
COL_SK = COL_SQ + SWA_Q
COL_SV = COL_SK + SWA_KV
COL_GATES = COL_SV + SWA_KV
LR_SRC = COL_CB

ALIBI_SLOPES = tuple(2.0 ** (-8.0 * (h + 1) / SWA_HEADS) for h in range(SWA_HEADS))
SWA_SCALE = SWA_HD ** -0.5

NT_DIMS = (((1,), (1,)), ((), ()))
TN_DIMS = (((0,), (0,)), ((), ()))


def _params(sem):
    return pltpu.CompilerParams(dimension_semantics=sem, vmem_limit_bytes=VMEM_LIMIT)


def _rmsnorm_rows(x, g):
    ms = jnp.mean(x * x, axis=-1, keepdims=True)
    return x * lax.rsqrt(ms + EPS) * g


def _log_sigmoid(z):
    return jnp.minimum(z, 0.0) - jnp.log(1.0 + jnp.exp(-jnp.abs(z)))


def _forget_gate_log(glr_ref, wlr_ref, blr_ref):
    z = jnp.dot(glr_ref[...].astype(BF16), wlr_ref[...].astype(BF16),
                preferred_element_type=F32) + blr_ref[...]
    return _log_sigmoid(z) / GLA_TAU


def _split_bf16x3(x):
    hi = x.astype(BF16)
    r = x - hi.astype(F32)
    mid = r.astype(BF16)
    lo = (r - mid.astype(F32)).astype(BF16)
    return hi, mid, lo


def _silu(x):
    return x * jax.nn.sigmoid(x)


def _lr_weight_spec(layer):
    return pl.BlockSpec((None, LANES, D_MODEL), lambda *a: (layer, LR_SRC // LANES, 0))


def _lr_project(h, wlr_ref):
    return lax.dot_general(h, wlr_ref[...].astype(BF16), NT_DIMS, preferred_element_type=F32)


def _norm_kernel(x_ref, g_ref, wlr_ref, h_ref, glr_ref):
    h = _rmsnorm_rows(x_ref[...], g_ref[...]).astype(BF16)
    h_ref[...] = h
    glr_ref[...] = _lr_project(h, wlr_ref)


def _norm(x, norm_w, w_in_t, layer, tm):
    m = x.shape[0]
    return pl.pallas_call(
        _norm_kernel,
        grid=(m // tm,),
        in_specs=[
            pl.BlockSpec((tm, D_MODEL), lambda i: (i, 0)),
            pl.BlockSpec((None, 1, D_MODEL), lambda i: (layer, 0, 0)),
            _lr_weight_spec(layer),
        ],
        out_specs=[
            pl.BlockSpec((tm, D_MODEL), lambda i: (i, 0)),
            pl.BlockSpec((tm, LANES), lambda i: (i, 0)),
        ],
        out_shape=[
            jax.ShapeDtypeStruct((m, D_MODEL), BF16),
            jax.ShapeDtypeStruct((m, LANES), F32),
        ],
        compiler_params=_params(("parallel",)),
        name="norm",
    )(x, norm_w, w_in_t)


def _inproj_kernel(hp_ref, hs_ref, w_hbm, yp_ref, gp_ref, ys_ref, gs_ref, wf32_scr, wbf_ref,
                   w_sem, *, layer, n_main, n_ptiles, n_plain):
    j = pl.program_id(0)
    i = pl.program_id(1)
    tn = wbf_ref.shape[0]

    def w_copy(tile):
        row = pl.multiple_of(tile * tn + jnp.where(tile >= n_plain, GLA_RANK, 0), GLA_RANK)
        return pltpu.make_async_copy(w_hbm.at[layer, pl.ds(row, tn), :], wf32_scr, w_sem)

    @pl.when(i == 0)
    def _():
        @pl.when(j == 0)
        def _():
            w_copy(0).start()

        w_copy(j).wait()
        wbf_ref[...] = wf32_scr[...].astype(BF16)

        @pl.when(j + 1 < pl.num_programs(0))
        def _():
            w_copy(j + 1).start()

    def project(h_ref):
        return lax.dot_general(h_ref[...], wbf_ref[...], NT_DIMS, preferred_element_type=F32)

    is_main = j < n_main
    is_prompt = i < n_ptiles

    @pl.when(jnp.logical_and(is_main, is_prompt))
    def _():
        yp_ref[...] = project(hp_ref)

    @pl.when(jnp.logical_and(jnp.logical_not(is_main), is_prompt))
    def _():
        gp_ref[...] = project(hp_ref).astype(BF16)

    @pl.when(jnp.logical_and(is_main, jnp.logical_not(is_prompt)))
    def _():
        ys_ref[...] = project(hs_ref)

    @pl.when(jnp.logical_and(jnp.logical_not(is_main), jnp.logical_not(is_prompt)))
    def _():
        gs_ref[...] = project(hs_ref).astype(BF16)


def _inproj(hp, hs, w_in_t, layer, tm, tn):
    mp, ms = hp.shape[0], hs.shape[0]
    n_ptiles = mp // tm
    n_main = COL_GATES // tn
    n_gate = (N_BRANCH * D_MODEL) // tn
    n_plain = LR_SRC // tn
    last_p = n_ptiles - 1
    prow = lambda i: jnp.minimum(i, last_p)
    return pl.pallas_call(
        functools.partial(_inproj_kernel, layer=layer, n_main=n_main, n_ptiles=n_ptiles,
                          n_plain=n_plain),
        grid=(n_main + n_gate, n_ptiles + 1),
        in_specs=[
            pl.BlockSpec((tm, D_MODEL), lambda j, i: (prow(i), 0)),
            pl.BlockSpec((ms, D_MODEL), lambda j, i: (0, 0)),
            pl.BlockSpec(memory_space=pl.ANY),
        ],
        out_specs=[
            pl.BlockSpec((tm, tn), lambda j, i: (jnp.where(j < n_main, prow(i), last_p),
                                                 jnp.minimum(j, n_main - 1))),
            pl.BlockSpec((tm, tn), lambda j, i: (jnp.where(j < n_main, 0, prow(i)),
                                                 jnp.maximum(j - n_main, 0))),
            pl.BlockSpec((ms, tn), lambda j, i: (0, jnp.minimum(j, n_main - 1))),
            pl.BlockSpec((ms, tn), lambda j, i: (0, jnp.maximum(j - n_main, 0))),
        ],
        out_shape=[
            jax.ShapeDtypeStruct((mp, COL_GATES), F32),
            jax.ShapeDtypeStruct((mp, N_BRANCH * D_MODEL), BF16),
            jax.ShapeDtypeStruct((ms, COL_GATES), F32),
            jax.ShapeDtypeStruct((ms, N_BRANCH * D_MODEL), BF16),
        ],
        scratch_shapes=[pltpu.VMEM((tn, D_MODEL), F32), pltpu.VMEM((tn, D_MODEL), BF16),
                        pltpu.SemaphoreType.DMA(())],
        compiler_params=_params(("arbitrary", "arbitrary")),
        name="inproj",
    )(hp, hs, w_in_t)


GLA_STEP_CHUNKS = 4


def _gla_tile(q_ref, k_ref, v_ref, gr_ref, glr_ref, wlr_ref, blr_ref, gn_ref, oa_ref, st_ref):
    cs = GLA_CHUNK
    log_a = _forget_gate_log(glr_ref, wlr_ref, blr_ref)
    tr = GLA_STEP_CHUNKS * cs
    row = lax.broadcasted_iota(jnp.int32, (tr, tr), 0)
    col = lax.broadcasted_iota(jnp.int32, (tr, tr), 1)
    log2_cs = cs.bit_length() - 1
    same_chunk = jnp.right_shift(row, log2_cs) == jnp.right_shift(col, log2_cs)
    causal = jnp.logical_and(same_chunk, row >= col)
    tri = jnp.where(causal, 1.0, 0.0).astype(BF16)
    gn = gn_ref[...]

    b = sum(jnp.dot(tri, part, preferred_element_type=F32) for part in _split_bf16x3(log_a))
    b_last = jnp.concatenate(
        [jnp.broadcast_to(b[(n + 1) * cs - 1:(n + 1) * cs, :], (cs, b.shape[1]))
         for n in range(GLA_STEP_CHUNKS)], axis=0)
    q = q_ref[...] * (GLA_DK ** -0.5)
    k = k_ref[...]
    qt = (q * jnp.exp(b)).astype(BF16)
    kt = (k * jnp.exp(-b)).astype(BF16)
    kd = (k * jnp.exp(b_last - b)).astype(BF16)
    g_last = jnp.exp(b_last)
    v = v_ref[...].astype(BF16)

    for h in range(GLA_HEADS):
        ks = slice(h * GLA_DK, (h + 1) * GLA_DK)
        vs = slice(h * GLA_DV, (h + 1) * GLA_DV)
        a = lax.dot_general(qt[:, ks], kt[:, ks], NT_DIMS, preferred_element_type=F32)
        a = jnp.where(causal, a, 0.0).astype(BF16)
        o_intra = jnp.dot(a, v[:, vs], preferred_element_type=F32)
        u_t = [lax.dot_general(v[n * cs:(n + 1) * cs, vs], kd[n * cs:(n + 1) * cs, ks], TN_DIMS,
                               preferred_element_type=F32) for n in range(GLA_STEP_CHUNKS)]
        s_t = st_ref[h]
        for n in range(GLA_STEP_CHUNKS):
            rs = slice(n * cs, (n + 1) * cs)
            o = lax.dot_general(qt[rs, ks], s_t.astype(BF16), NT_DIMS,
                                preferred_element_type=F32) + o_intra[rs, :]
            s_t = g_last[n * cs:n * cs + 1, ks] * s_t + u_t[n]
            oa_ref[rs, vs] = (_rmsnorm_rows(o, gn) * _silu(gr_ref[rs, vs])).astype(BF16)
        st_ref[h] = s_t


def _conv_tile(cb_ref, cc_ref, ch_ref, w_ref, ob_ref, cs_ref, prev_ref):
    u = cc_ref[...] * ch_ref[...]
    tl = u.shape[0]
    prev = prev_ref[...]
    p_m1 = prev[SUBLANES - 1:SUBLANES, :]
    p_m2 = prev[SUBLANES - 2:SUBLANES - 1, :]
    row = lax.broadcasted_iota(jnp.int32, u.shape, 0)
    u1 = jnp.where(row == 0, p_m1, pltpu.roll(u, 1, 0))
    u2 = jnp.where(row == 0, p_m2, jnp.where(row == 1, p_m1, pltpu.roll(u, 2, 0)))
    w = w_ref[...]
    conv = w[0:1, :] * u2 + w[1:2, :] * u1 + w[2:3, :] * u
    ob_ref[...] = (cb_ref[...] * conv).astype(BF16)
    prev_ref[...] = u[tl - SUBLANES:tl, :]
    cs_ref[0] = u[tl - (CONV_K - 1):tl, :]


def _swa_bias_init(bias_ref):
    key = lax.broadcasted_iota(jnp.int32, (2 * WINDOW, WINDOW), 0)
    qry = lax.broadcasted_iota(jnp.int32, (2 * WINDOW, WINDOW), 1)
    dist = qry - key + WINDOW
    valid = jnp.logical_and(dist >= 0, dist < WINDOW)
    dist_f = dist.astype(F32)
    for h in range(SWA_HEADS):
        bias = jnp.where(valid, -ALIBI_SLOPES[h] * dist_f, -jnp.inf)
        bias_ref[0, h] = jnp.where(key >= WINDOW, bias, -jnp.inf)
        bias_ref[1, h] = bias


def _swa_block(q, k2, v2_t, bias_ref, has_prev, sinks_ref, layer):
    scores = []
    for h in range(SWA_HEADS):
        j = h // SWA_GROUP
        scores.append(lax.dot_general(k2[:, j * SWA_HD:(j + 1) * SWA_HD],
                                      q[:, h * SWA_HD:(h + 1) * SWA_HD], NT_DIMS,
                                      preferred_element_type=F32))
    weights, inv_den = [], []
    for h in range(SWA_HEADS):
        s = scores[h] + bias_ref[has_prev, h]
        sink = sinks_ref[layer, h]
        m = jnp.maximum(jnp.max(s, axis=0, keepdims=True), sink)
        e = jnp.exp(s - m)
        den = jnp.sum(e, axis=0, keepdims=True) + jnp.exp(sink - m)
        weights.append(e.astype(BF16))
        inv_den.append(1.0 / den)
    outs_t = []
    for h in range(SWA_HEADS):
        j = h // SWA_GROUP
        outs_t.append(jnp.dot(v2_t[j * SWA_HD:(j + 1) * SWA_HD, :], weights[h],
                              preferred_element_type=F32) * inv_den[h])
    return jnp.concatenate(outs_t, axis=0).T.astype(BF16)


def _prompt_mixer_kernel(sinks_ref, q_ref, k_ref, v_ref, gr_ref, glr_ref, wlr_ref, blr_ref,
                         gn_ref, sq_ref, kc_ref, kp_ref, vc_ref, vp_ref,
                         cb_ref, cc_ref, ch_ref, cw_ref,
                         oa_ref, sfin_ref, oc_ref, ob_ref, cs_ref,
                         st_ref, bias_ref, prev_ref, *, layer):
    step = pl.program_id(1)

    @pl.when(step == 0)
    def _():
        st_ref[...] = jnp.zeros_like(st_ref)
        prev_ref[...] = jnp.zeros_like(prev_ref)
        _swa_bias_init(bias_ref)

    _gla_tile(q_ref, k_ref, v_ref, gr_ref, glr_ref, wlr_ref, blr_ref, gn_ref, oa_ref, st_ref)
    _conv_tile(cb_ref, cc_ref, ch_ref, cw_ref, ob_ref, cs_ref, prev_ref)

    q = (sq_ref[...] * SWA_SCALE).astype(BF16)
    k_tile = kc_ref[...]
    v_tile = vc_ref[...]
    for u in range(q.shape[0] // WINDOW):
        cur = slice(u * WINDOW, (u + 1) * WINDOW)
        if u == 0:
            k_prev, v_prev, has_prev = kp_ref[...], vp_ref[...], jnp.minimum(step, 1)
        else:
            before = slice((u - 1) * WINDOW, u * WINDOW)
            k_prev, v_prev, has_prev = k_tile[before], v_tile[before], 1
        k2 = jnp.concatenate([k_prev, k_tile[cur]], axis=0).astype(BF16)
        v2_t = jnp.concatenate([v_prev, v_tile[cur]], axis=0).T.astype(BF16)
        oc_ref[cur, :] = _swa_block(q[cur], k2, v2_t, bias_ref, has_prev, sinks_ref, layer)

    @pl.when(step == pl.num_programs(1) - 1)
    def _():
        for h in range(GLA_HEADS):
            sfin_ref[0, h] = st_ref[h].T


def _prompt_mixer(y, glr, sinks, w_lr_p, b_lr, gla_norm, conv_w, layer, batch, seq):
    tr = GLA_STEP_CHUNKS * GLA_CHUNK
    ns = seq // tr
    wpt = tr // WINDOW
    rows = lambda b, c: b * ns + c
    prev_window = lambda b, c: (b * ns + c) * wpt - jnp.minimum(c, 1)
    seg = lambda width, col: pl.BlockSpec((tr, width), lambda b, c: (rows(b, c), col // width))
    per_layer = lambda shape: pl.BlockSpec((None,) + shape, lambda b, c: (layer,) + (0,) * len(shape))
    row_out = lambda width: pl.BlockSpec((tr, width), lambda b, c: (rows(b, c), 0))
    return pl.pallas_call(
        functools.partial(_prompt_mixer_kernel, layer=layer),
        grid=(batch, ns),
        in_specs=[
            pl.BlockSpec(memory_space=pltpu.SMEM),
            seg(GLA_QK, COL_Q), seg(GLA_QK, COL_K), seg(GLA_V, COL_V), seg(GLA_V, COL_GR),
            pl.BlockSpec((tr, LANES), lambda b, c: (rows(b, c), 0)),
            per_layer((LANES, GLA_QK)), per_layer((1, GLA_QK)), per_layer((1, GLA_DV)),
            seg(SWA_Q, COL_SQ),
            seg(SWA_KV, COL_SK),
            pl.BlockSpec((WINDOW, SWA_KV), lambda b, c: (prev_window(b, c), COL_SK // SWA_KV)),
            seg(SWA_KV, COL_SV),
            pl.BlockSpec((WINDOW, SWA_KV), lambda b, c: (prev_window(b, c), COL_SV // SWA_KV)),
            seg(CONV_WIDTH, COL_CB), seg(CONV_WIDTH, COL_CC), seg(CONV_WIDTH, COL_CH),
            per_layer((CONV_K, CONV_WIDTH)),
        ],
        out_specs=[
            row_out(GLA_V),
            pl.BlockSpec((1, GLA_HEADS, GLA_DK, GLA_DV), lambda b, c: (b, 0, 0, 0)),
            row_out(SWA_Q), row_out(CONV_WIDTH),
            pl.BlockSpec((1, CONV_K - 1, CONV_WIDTH), lambda b, c: (b, 0, 0)),
        ],
        out_shape=[
            jax.ShapeDtypeStruct((batch * seq, GLA_V), BF16),
            jax.ShapeDtypeStruct((batch, GLA_HEADS, GLA_DK, GLA_DV), F32),
            jax.ShapeDtypeStruct((batch * seq, SWA_Q), BF16),
            jax.ShapeDtypeStruct((batch * seq, CONV_WIDTH), BF16),
            jax.ShapeDtypeStruct((batch, CONV_K - 1, CONV_WIDTH), F32),
        ],
        scratch_shapes=[pltpu.VMEM((GLA_HEADS, GLA_DV, GLA_DK), F32),
                        pltpu.VMEM((2, SWA_HEADS, 2 * WINDOW, WINDOW), F32),
                        pltpu.VMEM((SUBLANES, CONV_WIDTH), F32)],
        compiler_params=_params(("parallel", "arbitrary")),
        name="prompt_mixer",
    )(sinks, y, y, y, y, glr, w_lr_p, b_lr, gla_norm, y, y, y, y, y, y, y, y, conv_w)


SAMPLE_STEP = SUBLANES


def _sample_kernel(sinks_ref, q_ref, k_ref, v_ref, gr_ref, glr_ref, cb_ref, cc_ref, ch_ref,
                   sq_ref, sk_ref, sv_ref, wlr_ref, blr_ref, gn_ref, cw_ref,
                   s_ref, cs_ref, kc_ref, vc_ref, *rest, layer, n_carried):
    oa_ref, ob_ref, oc_ref, so_ref, cso_ref, kco_ref, vco_ref, o_scr = rest[n_carried:]
    nb = SAMPLE_STEP
    b = _forget_gate_log(glr_ref, wlr_ref, blr_ref)
    q = q_ref[...] * (GLA_DK ** -0.5)
    k = k_ref[...]
    v = v_ref[...]
    qt = q * jnp.exp(b)
    kt = k * jnp.exp(-b)
    kd = k * jnp.exp(b - b)
    g_last = jnp.exp(b)
    qt_t = qt.T
    kd_t = kd.T
    gl_t = g_last.T
    for h in range(GLA_HEADS):
        ks = slice(h * GLA_DK, (h + 1) * GLA_DK)
        vs = slice(h * GLA_DV, (h + 1) * GLA_DV)
        a = jnp.sum(qt[:, ks] * kt[:, ks], axis=-1, keepdims=True)
        o_intra = a * v[:, vs]
        for i in range(nb):
            s_in = s_ref[i, h]
            v_row = v[i:i + 1, vs]
            o = jnp.sum(qt_t[ks, i:i + 1] * s_in, axis=0, keepdims=True)
            o_scr[i:i + 1, vs] = o + o_intra[i:i + 1, :]
            so_ref[i, h] = gl_t[ks, i:i + 1] * s_in + kd_t[ks, i:i + 1] * v_row
    gn = gn_ref[...]
    for h in range(GLA_HEADS):
        vs = slice(h * GLA_DV, (h + 1) * GLA_DV)
        oa_ref[:, vs] = _rmsnorm_rows(o_scr[:, vs], gn) * _silu(gr_ref[:, vs])

    u_new = cc_ref[...] * ch_ref[...]
    w = cw_ref[...]
    p0 = cs_ref[:, :CONV_WIDTH]
    p1 = cs_ref[:, CONV_WIDTH:]
    conv = w[0:1, :] * p0 + w[1:2, :] * p1 + w[2:3, :] * u_new
    ob_ref[...] = cb_ref[...] * conv
    cso_ref[:, :CONV_WIDTH] = p1
    cso_ref[:, CONV_WIDTH:] = u_new

    slot = lax.broadcasted_iota(jnp.int32, (SWA_HD, WINDOW), 1)
    key = lax.broadcasted_iota(jnp.int32, (SUBLANES, WINDOW), 1)
    grp = lax.broadcasted_iota(jnp.int32, (SUBLANES, WINDOW), 0)
    dist = (WINDOW - 1 - key).astype(F32)
    bias, sink = [], []
    for j in range(SWA_KV_HEADS):
        bj = jnp.zeros((SUBLANES, WINDOW), F32)
        sj = jnp.zeros((SUBLANES, 1), F32)
        for g in range(SWA_GROUP):
            bj = jnp.where(grp == g, -ALIBI_SLOPES[j * SWA_GROUP + g] * dist, bj)
            sj = jnp.where(grp[:, 0:1] == g, sinks_ref[layer, j * SWA_GROUP + g], sj)
        bias.append(bj)
        sink.append(sj)
    sk_t = sk_ref[...].T
    sv_t = sv_ref[...].T
    k_new, v_new = [], []
    for i in range(nb):
        for j in range(SWA_KV_HEADS):
            hd = slice(j * SWA_HD, (j + 1) * SWA_HD)
            kn = jnp.where(slot == WINDOW - 1, sk_t[hd, i:i + 1],
                           pltpu.roll(kc_ref[i, j], WINDOW - 1, 1))
            vn = jnp.where(slot == WINDOW - 1, sv_t[hd, i:i + 1],
                           pltpu.roll(vc_ref[i, j], WINDOW - 1, 1))
            kco_ref[i, j] = kn
            vco_ref[i, j] = vn
            k_new.append(kn.astype(BF16))
            v_new.append(vn.astype(BF16))
    scores = []
    for i in range(nb):
        for j in range(SWA_KV_HEADS):
            qj = (sq_ref[i, j] * SWA_SCALE).astype(BF16)
            scores.append(jnp.dot(qj, k_new[i * SWA_KV_HEADS + j], preferred_element_type=F32))
    probs = []
    for i in range(nb):
        for j in range(SWA_KV_HEADS):
            s = scores[i * SWA_KV_HEADS + j] + bias[j]
            m = jnp.maximum(jnp.max(s, axis=-1, keepdims=True), sink[j])
            e = jnp.exp(s - m)
            den = jnp.sum(e, axis=-1, keepdims=True) + jnp.exp(sink[j] - m)
            probs.append((e * (1.0 / den)).astype(BF16))
    for i in range(nb):
        for j in range(SWA_KV_HEADS):
            n = i * SWA_KV_HEADS + j
            oc_ref[i, j] = lax.dot_general(probs[n], v_new[n], NT_DIMS,
                                           preferred_element_type=F32)


def _sample_mixer(y, glr, sinks, w_lr_p, b_lr, gla_norm, conv_w,
                  state_gla, state_conv, cache_k, cache_v, carried, layer):
    nb = y.shape[0]
    st = SAMPLE_STEP
    sq = y[:, COL_SQ:COL_SQ + SWA_Q].reshape(nb, SWA_KV_HEADS, SWA_GROUP, SWA_HD)
    sq = jnp.pad(sq, ((0, 0), (0, 0), (0, SUBLANES - SWA_GROUP), (0, 0)))

    def seg(width, col):
        return pl.BlockSpec((st, width), lambda b: (b, col // width))

    def per_layer(shape):
        return pl.BlockSpec((None,) + shape, lambda b: (layer,) + (0,) * len(shape))

    def state(shape):
        return pl.BlockSpec((None, st) + shape, lambda b: (layer, b) + (0,) * len(shape))

    def out(shape):
        return pl.BlockSpec((st,) + shape, lambda b: (b,) + (0,) * len(shape))

    cache_shape = (SWA_KV_HEADS, SWA_HD, WINDOW)
    state_shapes = ((GLA_HEADS, GLA_DK, GLA_DV), ((CONV_K - 1) * CONV_WIDTH,),
                    cache_shape, cache_shape)
    n_fixed_inputs = 20
    res = pl.pallas_call(
        functools.partial(_sample_kernel, layer=layer, n_carried=len(carried)),
        grid=(nb // st,),
        in_specs=[
            pl.BlockSpec(memory_space=pltpu.SMEM),
            seg(GLA_QK, COL_Q), seg(GLA_QK, COL_K), seg(GLA_V, COL_V), seg(GLA_V, COL_GR),
            pl.BlockSpec((st, LANES), lambda b: (b, 0)),
            seg(CONV_WIDTH, COL_CB), seg(CONV_WIDTH, COL_CC), seg(CONV_WIDTH, COL_CH),
            out((SWA_KV_HEADS, SUBLANES, SWA_HD)), seg(SWA_KV, COL_SK), seg(SWA_KV, COL_SV),
            per_layer((LANES, GLA_QK)), per_layer((1, GLA_QK)), per_layer((1, GLA_DV)),
            per_layer((CONV_K, CONV_WIDTH)),
        ] + [state(s) for s in state_shapes] + [pl.BlockSpec(memory_space=pl.ANY)] * len(carried),
        out_specs=[
            out((GLA_V,)), out((CONV_WIDTH,)), out((SWA_KV_HEADS, SUBLANES, SWA_HD)),
        ] + [state(s) for s in state_shapes],
        out_shape=[
            jax.ShapeDtypeStruct((nb, GLA_V), F32),
            jax.ShapeDtypeStruct((nb, CONV_WIDTH), F32),
            jax.ShapeDtypeStruct((nb, SWA_KV_HEADS, SUBLANES, SWA_HD), F32),
        ] + [jax.ShapeDtypeStruct((DEPTH, nb) + s, F32) for s in state_shapes],
        input_output_aliases={n_fixed_inputs + n: 3 + n for n in range(len(carried))},
        scratch_shapes=[pltpu.VMEM((st, GLA_V), F32)],
        compiler_params=_params(("parallel",)),
        name="sample_mixer",
    )(sinks, y, y, y, y, glr, y, y, y, sq, y, y,
      w_lr_p, b_lr, gla_norm, conv_w, state_gla, state_conv, cache_k, cache_v, *carried)
    oa, ob, oc = res[:3]
    oc = oc[:, :, :SWA_GROUP, :].reshape(nb, SWA_Q)
    return oa, ob, oc, tuple(res[3:])


def _merge_kernel(*refs, n_ptiles, n_col):
    (pa, pb, pc, pg0, pg1, pg2, px, sa, sb, sc, sg0, sg1, sg2, sx, wb_ref, wo_ref,
     po_ref, so_ref, pmix_scr, smix_scr) = refs
    i = pl.program_id(0)
    s = pl.program_id(1)
    tn = pmix_scr.shape[2]

    def group(a_ref, b_ref, c_ref, g0_ref, g1_ref, g2_ref, x_ref, o_ref, mix_scr):
        @pl.when(s < n_col)
        def _():
            def branch(src_ref, gate_ref, n):
                br = jnp.dot(src_ref[...].astype(BF16), wb_ref[n], preferred_element_type=F32)
                return jax.nn.sigmoid(gate_ref[...].astype(F32)) * br

            mixed = (branch(a_ref, g0_ref, 0) + branch(b_ref, g1_ref, 1)
                     + branch(c_ref, g2_ref, 2))
            mix_scr[s] = mixed.astype(BF16)

        @pl.when(s >= n_col)
        def _():
            acc = x_ref[...]
            for c in range(n_col):
                acc = acc + jnp.dot(mix_scr[c], wo_ref[c * tn:(c + 1) * tn, :],
                                    preferred_element_type=F32)
            o_ref[...] = acc

    @pl.when(i < n_ptiles)
    def _():
        group(pa, pb, pc, pg0, pg1, pg2, px, po_ref, pmix_scr)

    @pl.when(i == n_ptiles)
    def _():
        group(sa, sb, sc, sg0, sg1, sg2, sx, so_ref, smix_scr)


def _merge(prompt, sample, w_branch, w_out, layer, tm, tn, tn_out):
    mp, ms = prompt[4].shape[0], sample[4].shape[0]
    n_ptiles = mp // tm
    n_col = D_MODEL // tn
    prow = lambda i: jnp.minimum(i, n_ptiles - 1)
    mix_col = lambda s: jnp.minimum(s, n_col - 1)
    out_col = lambda s: jnp.maximum(s - n_col, 0)
    n_out = D_MODEL // tn_out
    p_out_col = lambda i, s: jnp.where(i < n_ptiles, out_col(s), n_out - 1)
    s_out_col = lambda i, s: jnp.where(i < n_ptiles, 0, out_col(s))
    gate_col = lambda n: (n * D_MODEL) // tn
    p_src = pl.BlockSpec((tm, BRANCH_WIDTH), lambda i, s: (prow(i), 0))
    p_gate = lambda n: pl.BlockSpec((tm, tn), lambda i, s: (prow(i), gate_col(n) + mix_col(s)))
    s_src = pl.BlockSpec((ms, BRANCH_WIDTH), lambda i, s: (0, 0))
    s_gate = lambda n: pl.BlockSpec((ms, tn), lambda i, s: (0, gate_col(n) + mix_col(s)))
    return pl.pallas_call(
        functools.partial(_merge_kernel, n_ptiles=n_ptiles, n_col=n_col),
        grid=(n_ptiles + 1, n_col + n_out),
        in_specs=[
            p_src, p_src, p_src, p_gate(0), p_gate(1), p_gate(2),
            pl.BlockSpec((tm, tn_out), lambda i, s: (prow(i), p_out_col(i, s))),
            s_src, s_src, s_src, s_gate(0), s_gate(1), s_gate(2),
            pl.BlockSpec((ms, tn_out), lambda i, s: (0, s_out_col(i, s))),
            pl.BlockSpec((None, N_BRANCH, BRANCH_WIDTH, tn),
                         lambda i, s: (layer, 0, 0, mix_col(s))),
            pl.BlockSpec((None, D_MODEL, tn_out), lambda i, s: (layer, 0, out_col(s))),
        ],
        out_specs=[
            pl.BlockSpec((tm, tn_out), lambda i, s: (prow(i), p_out_col(i, s))),
            pl.BlockSpec((ms, tn_out), lambda i, s: (0, s_out_col(i, s))),
        ],
        out_shape=[
            jax.ShapeDtypeStruct((mp, D_MODEL), F32),
            jax.ShapeDtypeStruct((ms, D_MODEL), F32),
        ],
        scratch_shapes=[pltpu.VMEM((n_col, tm, tn), BF16), pltpu.VMEM((n_col, ms, tn), BF16)],
        compiler_params=_params(("arbitrary", "arbitrary")),
        name="merge_outproj",
    )(prompt[0], prompt[1], prompt[2], prompt[3], prompt[3], prompt[3], prompt[4],
      sample[0], sample[1], sample[2], sample[3], sample[3], sample[3], sample[4],
      w_branch, w_out)


def _mlp_kernel(xp_ref, xs_ref, g_ref, wu_ref, wd_ref, gnext_ref, wlr_ref, *rest,
                last, n_ptiles):
    if last:
        po, so, hp_scr, hs_scr, act_scr, xp_scr, x_sem = rest
        p_out, s_out = (po,), (so,)
    else:
        po, php, pglr, so, shp, sglr, hp_scr, hs_scr, act_scr, xp_scr, x_sem = rest
        p_out, s_out = (po, php, pglr), (so, shp, sglr)
    i = pl.program_id(0)
    f = pl.program_id(1)
    tm = xp_scr.shape[0]

    def x_copy(tile_index):
        return pltpu.make_async_copy(xp_ref.at[pl.ds(tile_index * tm, tm), :], xp_scr, x_sem)

    @pl.when(jnp.logical_and(i == 0, f == 0))
    def _():
        x_copy(0).start()

    @pl.when(jnp.logical_and(i < n_ptiles, f == 0))
    def _():
        x_copy(i).wait()

    def tile(x_ref, h_ref, o_ref, hn_ref=None, glr_ref=None):
        @pl.when(f == 0)
        def _():
            x = x_ref[...]
            h_ref[...] = _rmsnorm_rows(x, g_ref[...]).astype(BF16)
            o_ref[...] = x

        rows = h_ref.shape[0]
        part = wu_ref.shape[1] // MLP_UP_PARTS
        for p in range(MLP_UP_PARTS):
            cols = slice(p * part, (p + 1) * part)
            up = jnp.dot(h_ref[...], wu_ref[:, cols], preferred_element_type=F32)
            act_scr[:rows, cols] = jnp.square(jnp.maximum(up, 0.0)).astype(BF16)
        o_ref[...] += jnp.dot(act_scr[:rows, :], wd_ref[...], preferred_element_type=F32)

        @pl.when(f == pl.num_programs(1) - 1)
        def _():
            xn = _rmsnorm_rows(o_ref[...], gnext_ref[...])
            if last:
                o_ref[...] = xn
            else:
                hn = xn.astype(BF16)
                hn_ref[...] = hn
                glr_ref[...] = _lr_project(hn, wlr_ref)

    @pl.when(i < n_ptiles)
    def _():
        tile(xp_scr, hp_scr, *p_out)

    @pl.when(jnp.logical_and(i + 1 < n_ptiles, f == 1))
    def _():
        x_copy(i + 1).start()

    @pl.when(i == n_ptiles)
    def _():
        tile(xs_ref, hs_scr, *s_out)


def _mlp(xp, xs, norm_w, w_up, w_down, next_norm_w, w_in_t, layer, tm, tf):
    mp, ms = xp.shape[0], xs.shape[0]
    n_ptiles = mp // tm
    last = layer == DEPTH - 1
    prow = lambda i: jnp.minimum(i, n_ptiles - 1)

    def group_out(m, rows, row_map):
        specs = [pl.BlockSpec((rows, D_MODEL), lambda i, f: (row_map(i), 0))]
        shapes = [jax.ShapeDtypeStruct((m, D_MODEL), F32)]
        if not last:
            specs += [pl.BlockSpec((rows, D_MODEL), lambda i, f: (row_map(i), 0)),
                      pl.BlockSpec((rows, LANES), lambda i, f: (row_map(i), 0))]
            shapes += [jax.ShapeDtypeStruct((m, D_MODEL), BF16),
                       jax.ShapeDtypeStruct((m, LANES), F32)]
        return specs, shapes

    p_specs, p_shapes = group_out(mp, tm, prow)
    s_specs, s_shapes = group_out(ms, ms, lambda i: 0)
    if last:
        next_norm_spec = pl.BlockSpec((1, D_MODEL), lambda i, f: (0, 0))
        lr_layer = layer
    else:
        next_norm_spec = pl.BlockSpec((None, 1, D_MODEL), lambda i, f: (layer + 1, 0, 0))
        lr_layer = layer + 1
    return pl.pallas_call(
        functools.partial(_mlp_kernel, last=last, n_ptiles=n_ptiles),
        grid=(n_ptiles + 1, D_FF // tf),
        in_specs=[
            pl.BlockSpec(memory_space=pl.ANY),
            pl.BlockSpec((ms, D_MODEL), lambda i, f: (0, 0)),
            pl.BlockSpec((None, 1, D_MODEL), lambda i, f: (layer, 0, 0)),
            pl.BlockSpec((None, D_MODEL, tf), lambda i, f: (layer, 0, f)),
            pl.BlockSpec((None, tf, D_MODEL), lambda i, f: (layer, f, 0)),
            next_norm_spec,
            _lr_weight_spec(lr_layer),
        ],
        out_specs=p_specs + s_specs,
        out_shape=p_shapes + s_shapes,
        scratch_shapes=[pltpu.VMEM((tm, D_MODEL), BF16), pltpu.VMEM((ms, D_MODEL), BF16),
                        pltpu.VMEM((tm, tf), BF16),
                        pltpu.VMEM((tm, D_MODEL), F32), pltpu.SemaphoreType.DMA(())],
        compiler_params=_params(("arbitrary", "arbitrary")),
        name="mlp",
    )(xp, xs, norm_w, w_up, w_down, next_norm_w, w_in_t)


def kernel(x_prompt, x_sample, state_gla, state_conv, cache_k, cache_v, w_in, w_lr, b_lr,
           gla_norm, conv_w, attn_sinks, w_branch, w_out, norm_mix, norm_mlp, w_up, w_down,
           norm_final):
    batch, seq, _ = x_prompt.shape
    nb = x_sample.shape[0]
    mp = batch * seq
    tm = min(ROW_TILE, mp)
    assert mp % tm == 0 and seq % (GLA_STEP_CHUNKS * GLA_CHUNK) == 0 and seq % WINDOW == 0
    assert x_sample.shape[1] == 1 and nb % SAMPLE_STEP == 0

    w_lr_p =jnp.pad(w_lr, ((0, 0), (0, LANES - GLA_RANK), (0, 0)))
    w_branch_b = w_branch.astype(BF16)
    w_out_b = w_out.astype(BF16)
    w_up_b = w_up.astype(BF16)
    w_down_b = w_down.astype(BF16)
    b_lr3 = b_lr.reshape(DEPTH, 1, GLA_QK)
    gla_norm3 = gla_norm.reshape(DEPTH, 1, GLA_DV)
    norm_mix3 = norm_mix.reshape(DEPTH, 1, D_MODEL)
    norm_mlp3 = norm_mlp.reshape(DEPTH, 1, D_MODEL)
    norm_final2 = norm_final.reshape(1, D_MODEL)
    state_conv2 = state_conv.reshape(DEPTH, nb, (CONV_K - 1) * CONV_WIDTH)
    cache_k4 = jnp.transpose(cache_k, (0, 1, 3, 4, 2))
    cache_v4 = jnp.transpose(cache_v, (0, 1, 3, 4, 2))

    w_in_t = jnp.swapaxes(w_in, 1, 2)

    xp = x_prompt.reshape(mp, D_MODEL)
    xs = x_sample.reshape(nb, D_MODEL)
    hp, glrp = _norm(xp, norm_mix3, w_in_t, 0, tm=tm)
    hs, glrs = _norm(xs, norm_mix3, w_in_t, 0, tm=nb)
    outs = [[] for _ in range(4)]
    sample_states = ()
    for l in range(DEPTH):
        next_norm = norm_final2 if l == DEPTH - 1 else norm_mix3
        yp, gp, ys, gs = _inproj(hp, hs, w_in_t, l, tm=tm, tn=INPROJ_COL_TILE)
        oa, sg_p, oc, ob, sc_p = _prompt_mixer(yp, glrp, attn_sinks, w_lr_p, b_lr3, gla_norm3,
                                               conv_w, l, batch, seq)
        yp3 = yp.reshape(batch, seq, COL_GATES)
        kp = yp3[:, seq - WINDOW:, COL_SK:COL_SK + SWA_KV]
        vp = yp3[:, seq - WINDOW:, COL_SV:COL_SV + SWA_KV]
        sa, sb, sc, sample_states = _sample_mixer(
            ys, glrs, attn_sinks, w_lr_p, b_lr3, gla_norm3, conv_w,
            state_gla, state_conv2, cache_k4, cache_v4, sample_states, l)
        xp, xs = _merge((oa, ob, oc, gp, xp), (sa, sb, sc, gs, xs), w_branch_b, w_out_b, l,
                        tm=tm, tn=MERGE_COL_TILE, tn_out=MERGE_OUT_TILE)
        res = _mlp(xp, xs, norm_mlp3, w_up_b, w_down_b, next_norm, w_in_t, l, tm=tm,
                   tf=MLP_FF_TILE)
        if l == DEPTH - 1:
            xp, xs = res
        else:
            xp, hp, glrp, xs, hs, glrs = res
        for lst, val in zip(outs, (
                sg_p, sc_p,
                kp.reshape(batch, WINDOW, SWA_KV_HEADS, SWA_HD),
                vp.reshape(batch, WINDOW, SWA_KV_HEADS, SWA_HD))):
            lst.append(val)
    y_prompt = xp.reshape(batch, seq, D_MODEL)
    y_sample = xs.reshape(nb, 1, D_MODEL)
    sg_p, sc_p, kp, vp = (jnp.stack(o) for o in outs)
    sg_s, sc_s, ks, vs = sample_states
    return (y_prompt, y_sample, sg_p, sg_s, sc_p,
            sc_s.reshape(DEPTH, nb, CONV_K - 1, CONV_WIDTH), kp,
            jnp.transpose(ks, (0, 1, 4, 2, 3)), vp, jnp.transpose(vs, (0, 1, 4, 2, 3)))
```
